```python
import math
import jax
import jax.numpy as jnp
from jax import lax
import numpy as np

D_MODEL = 1024
BATCH = 8
SEQ = 2048
DEPTH = 2

MEM_LEN = 256
HEAD_DIM = 128
A_HEADS = 8
B_HEADS = 8
A_WIDTH = A_HEADS * HEAD_DIM
B_WIDTH = B_HEADS * HEAD_DIM
EVEN_WIDTH = A_WIDTH + B_WIDTH
EVEN_IN = 3 * A_WIDTH + 3 * B_WIDTH + EVEN_WIDTH
C_WIDTH = D_MODEL
C_BLOCKS = 8
C_BLOCK_DIM = C_WIDTH // C_BLOCKS
CONV_WIDTH = 4
RG_C = 8.0
D_HEADS = 8
D_WIDTH = D_HEADS * HEAD_DIM
ODD_WIDTH = C_WIDTH + D_WIDTH
ODD_IN = C_WIDTH + 3 * D_WIDTH + ODD_WIDTH
HGRN_CHUNK = 32
RET_CHUNK = 64
MOBA_BLOCK = 256
MOBA_TOPK = 3
MOBA_QCHUNK = 16
ROPE_THETA = 500000.0
ROPE_DIM = HEAD_DIM // 4
RET_THETA = 10000.0
XA_HEADS = 4
XA_HEAD_DIM = 128
XA_WIDTH = XA_HEADS * XA_HEAD_DIM
N_EVEN = (DEPTH + 1) // 2
N_ODD = DEPTH // 2
EPS = 1e-6

kernel_name = 'hybrid_hgrn2_moba_rglru_retention'


def rms_norm(x, gain):
    xf = x.astype(jnp.float32)
    y = xf * lax.rsqrt(jnp.mean(xf * xf, axis=-1, keepdims=True) + EPS)
    return (y * gain.astype(jnp.float32)).astype(x.dtype)


def head_rms_norm(o, gain):
    H, d = o.shape[-2:]
    y = o * lax.rsqrt(jnp.mean(o * o, axis=-1, keepdims=True) + EPS)
    return y * gain.astype(jnp.float32).reshape(H, d)


def head_group_norm(o, gain, bias):
    H, d = o.shape[-2:]
    mu = jnp.mean(o, axis=-1, keepdims=True)
    var = jnp.mean(jnp.square(o - mu), axis=-1, keepdims=True)
    y = (o - mu) * lax.rsqrt(var + EPS)
    return y * gain.astype(jnp.float32).reshape(H, d) + bias.astype(jnp.float32).reshape(H, d)


def rotary(x, pos, rot_dim, theta):
    half = rot_dim // 2
    inv = theta ** (-jnp.arange(half, dtype=jnp.float32) / half)
    ang = pos.astype(jnp.float32)[..., None] * inv
    cos = jnp.cos(ang)[:, :, None, :]
    sin = jnp.sin(ang)[:, :, None, :]
    xf = x.astype(jnp.float32)
    x1, x2, rest = xf[..., :half], xf[..., half:rot_dim], xf[..., rot_dim:]
    out = jnp.concatenate([x1 * cos - x2 * sin, x2 * cos + x1 * sin, rest], axis=-1)
    return out.astype(x.dtype)


def chunk_state_scan(decay, u):
    def step(s, inp):
        dec, uc = inp
        return dec * s + uc, s
    s0 = jnp.zeros(u.shape[:2] + u.shape[3:], u.dtype)
    _, s_prev = lax.scan(step, s0, (jnp.moveaxis(decay, 2, 0), jnp.moveaxis(u, 2, 0)))
    return jnp.moveaxis(s_prev, 0, 2)


def hgrn2_chunked(q, f_logit, i, lb):
    B, S, H, d = q.shape
    n = S // HGRN_CHUNK
    def to_chunks(t):
        return t.astype(jnp.float32).reshape(B, n, HGRN_CHUNK, H, d).transpose(0, 3, 1, 2, 4)
    lbb = lb.astype(jnp.float32)[None, :, None, None, :]
    f = lbb + (1.0 - lbb) * jax.nn.sigmoid(to_chunks(f_logit))
    k = 1.0 - f
    qc, vc = to_chunks(q), to_chunks(i)
    b = jnp.cumsum(jnp.log(f), axis=3)
    b_last = b[:, :, :, -1:, :]
    q_dec = qc * jnp.exp(b)
    k_inv = k * jnp.exp(-b)
    k_end = k * jnp.exp(b_last - b)
    causal = jnp.tril(jnp.ones((HGRN_CHUNK, HGRN_CHUNK), bool))
    att = jnp.where(causal, jnp.einsum('bhncd,bhnsd->bhncs', q_dec, k_inv), 0.0)
    o_intra = jnp.einsum('bhncs,bhnsv->bhncv', att, vc)
    u = jnp.einsum('bhnsd,bhnsv->bhndv', k_end, vc)
    decay = jnp.exp(b_last[:, :, :, 0, :])[..., None]
    s_prev = chunk_state_scan(decay, u)
    o_inter = jnp.einsum('bhncd,bhndv->bhncv', q_dec, s_prev)
    return (o_intra + o_inter).transpose(0, 2, 3, 1, 4).reshape(B, S, H, d)


def moba_attention(q, k, v):
    B, S, H, d = q.shape
    nb = -(-S // MOBA_BLOCK)
    s_pad = nb * MOBA_BLOCK
    n_sel = min(MOBA_TOPK, nb - 1)
    n_chunks = s_pad // MOBA_QCHUNK
    scale = d ** -0.5
    def to_bhsd(t):
        t = t.astype(jnp.float32).transpose(0, 2, 1, 3)
        return jnp.pad(t, ((0, 0), (0, 0), (0, s_pad - S), (0, 0)))
    q, k, v = to_bhsd(q), to_bhsd(k), to_bhsd(v)
    k_blocks = k.reshape(B, H, nb, MOBA_BLOCK, d)
    v_blocks = v.reshape(B, H, nb, MOBA_BLOCK, d)
    if n_sel > 0:
        k_mean = jnp.mean(k_blocks, axis=3)
        gate = jnp.einsum('bhsd,bhnd->bhsn', q, k_mean)
        q_block = jnp.arange(s_pad) // MOBA_BLOCK
        fully_past = jnp.arange(nb)[None, :] < q_block[:, None]
        gate = jnp.where(fully_past, gate, -jnp.inf)
        top_val, top_idx = lax.top_k(gate, n_sel)
        sel_valid = jnp.isfinite(top_val)
    else:
        top_idx = jnp.zeros((B, H, s_pad, 0), jnp.int32)
        sel_valid = jnp.zeros((B, H, s_pad, 0), bool)
    def per_chunk(t):
        return jnp.moveaxis(t.reshape(B, H, n_chunks, MOBA_QCHUNK, *t.shape[3:]), 2, 0)
    b_ix = jnp.arange(B)[:, None, None, None]
    h_ix = jnp.arange(H)[None, :, None, None]
    n_gather = n_sel * MOBA_BLOCK
    def attend(args):
        c, qc, idx, valid = args
        start = c * MOBA_QCHUNK
        own = start // MOBA_BLOCK
        k_own = lax.dynamic_slice_in_dim(k, own * MOBA_BLOCK, MOBA_BLOCK, axis=2)
        v_own = lax.dynamic_slice_in_dim(v, own * MOBA_BLOCK, MOBA_BLOCK, axis=2)
        q_pos = start + jnp.arange(MOBA_QCHUNK)
        k_pos = own * MOBA_BLOCK + jnp.arange(MOBA_BLOCK)
        s_own = jnp.einsum('bhqd,bhjd->bhqj', qc, k_own) * scale
        s_own = jnp.where(k_pos[None, :] <= q_pos[:, None], s_own, -jnp.inf)
        k_sel = k_blocks[b_ix, h_ix, idx]
        v_sel = v_blocks[b_ix, h_ix, idx]
        s_sel = jnp.einsum('bhqd,bhqnjd->bhqnj', qc, k_sel) * scale
        s_sel = jnp.where(valid[..., None], s_sel, -jnp.inf)
        p = jax.nn.softmax(jnp.concatenate([s_sel.reshape(B, H, MOBA_QCHUNK, n_gather), s_own], axis=-1), axis=-1)
        p_sel = p[..., :n_gather].reshape(B, H, MOBA_QCHUNK, n_sel, MOBA_BLOCK)
        p_own = p[..., n_gather:]
        return jnp.einsum('bhqnj,bhqnjd->bhqd', p_sel, v_sel) + jnp.einsum('bhqj,bhjd->bhqd', p_own, v_own)
    out = lax.map(attend, (jnp.arange(n_chunks), per_chunk(q), per_chunk(top_idx), per_chunk(sel_valid)))
    out = jnp.moveaxis(out, 0, 2).reshape(B, H, s_pad, d)[:, :, :S]
    return out.transpose(0, 2, 1, 3)


def rglru_branch(cx, conv_w, conv_b, wa, ba, wx, bx, lam):
    B, S, C = cx.shape
    xc = lax.conv_general_dilated(cx, conv_w[:, None, :], window_strides=(1,),
                                  padding=[(CONV_WIDTH - 1, 0)],
                                  dimension_numbers=('NWC', 'WIO', 'NWC'),
                                  feature_group_count=C)
    xf = xc.astype(jnp.float32) + conv_b.astype(jnp.float32)
    xb = xf.reshape(B, S, C_BLOCKS, C_BLOCK_DIM)
    r = jax.nn.sigmoid(jnp.einsum('bshi,hij->bshj', xb, wa.astype(jnp.float32)).reshape(B, S, C) + ba.astype(jnp.float32))
    ig = jax.nn.sigmoid(jnp.einsum('bshi,hij->bshj', xb, wx.astype(jnp.float32)).reshape(B, S, C) + bx.astype(jnp.float32))
    log_a = -RG_C * r * jax.nn.softplus(-lam.astype(jnp.float32))
    a = jnp.exp(log_a)
    u = jnp.sqrt(-jnp.expm1(2.0 * log_a)) * ig * xf
    def combine(left, right):
        a1, b1 = left
        a2, b2 = right
        return a1 * a2, a2 * b1 + b2
    _, h = lax.associative_scan(combine, (a, u), axis=1)
    return h


def retention_chunked(q, k, v, pos):
    B, S, H, d = q.shape
    n = S // RET_CHUNK
    q = rotary(q, pos, d, RET_THETA)
    k = rotary(k, pos, d, RET_THETA)
    def to_chunks(t):
        return t.astype(jnp.float32).reshape(B, n, RET_CHUNK, H, d).transpose(0, 3, 1, 2, 4)
    qc, kc, vc = to_chunks(q), to_chunks(k) * (d ** -0.5), to_chunks(v)
    log_g = jnp.log(1.0 - 2.0 ** (-5.0 - jnp.arange(H, dtype=jnp.float32)))
    idx = jnp.arange(RET_CHUNK, dtype=jnp.float32)
    diff = idx[:, None] - idx[None, :]
    dmask = jnp.where(diff >= 0, jnp.exp(jnp.maximum(diff, 0.0) * log_g[:, None, None]), 0.0)
    att = jnp.einsum('bhncd,bhnsd->bhncs', qc, kc) * dmask[None, :, None]
    o_intra = jnp.einsum('bhncs,bhnsv->bhncv', att, vc)
    q_dec = qc * jnp.exp((idx + 1.0)[None, :] * log_g[:, None])[None, :, None, :, None]
    k_end = kc * jnp.exp((RET_CHUNK - 1.0 - idx)[None, :] * log_g[:, None])[None, :, None, :, None]
    u = jnp.einsum('bhnsd,bhnsv->bhndv', k_end, vc)
    chunk_decay = jnp.broadcast_to(jnp.exp(RET_CHUNK * log_g)[None, :, None, None, None], (1, H, n, 1, 1))
    s_prev = chunk_state_scan(chunk_decay, u)
    o_inter = jnp.einsum('bhncd,bhndv->bhncv', q_dec, s_prev)
    return (o_intra + o_inter).transpose(0, 2, 3, 1, 4).reshape(B, S, H, d)


def even_mixer(xn, pos, lb, w_in, hgrn_gain, w_out):
    B, S, _ = xn.shape
    h = jnp.einsum('bsd,de->bse', xn, w_in)
    splits = np.cumsum([A_WIDTH] * 3 + [B_WIDTH] * 3).tolist()
    aq, af, ai, bq, bk, bv, z = jnp.split(h, splits, axis=-1)
    def heads(t):
        return t.reshape(B, S, -1, HEAD_DIM)
    oa = hgrn2_chunked(heads(aq), heads(af), heads(ai), lb.reshape(A_HEADS, HEAD_DIM))
    oa = head_rms_norm(oa, hgrn_gain)
    ob = moba_attention(rotary(heads(bq), pos, ROPE_DIM, ROPE_THETA),
                        rotary(heads(bk), pos, ROPE_DIM, ROPE_THETA), heads(bv))
    y = jnp.concatenate([oa.reshape(B, S, A_WIDTH), ob.reshape(B, S, B_WIDTH)], axis=-1).astype(xn.dtype)
    y = y * jax.nn.silu(z)
    return jnp.einsum('bse,ed->bsd', y, w_out)


def odd_mixer(xn, pos, w_in, conv_w, conv_b, wa, ba, wx, bx, lam, ret_gain, ret_bias, w_out):
    B, S, _ = xn.shape
    h = jnp.einsum('bsd,de->bse', xn, w_in)
    splits = np.cumsum([C_WIDTH] + [D_WIDTH] * 3).tolist()
    cx, dq, dk, dv, z = jnp.split(h, splits, axis=-1)
    oc = rglru_branch(cx, conv_w, conv_b, wa, ba, wx, bx, lam)
    def heads(t):
        return t.reshape(B, S, D_HEADS, HEAD_DIM)
    od = head_group_norm(retention_chunked(heads(dq), heads(dk), heads(dv), pos), ret_gain, ret_bias)
    y = jnp.concatenate([oc, od.reshape(B, S, D_WIDTH)], axis=-1).astype(xn.dtype)
    y = y * jax.nn.silu(z)
    return jnp.einsum('bse,ed->bsd', y, w_out)


def memory_cross_attention(xn, mem, mem_gain, wq, wkv, wo):
    B, S, _ = xn.shape
    M = mem.shape[1]
    mn = rms_norm(mem, mem_gain)
    q = jnp.einsum('bsd,de->bse', xn, wq).reshape(B, S, XA_HEADS, XA_HEAD_DIM)
    kv = jnp.einsum('bmd,de->bme', mn, wkv)
    k, v = jnp.split(kv, 2, axis=-1)
    k = k.reshape(B, M, XA_HEADS, XA_HEAD_DIM)
    v = v.reshape(B, M, XA_HEADS, XA_HEAD_DIM)
    s = jnp.einsum('bqhd,bkhd->bhqk', q.astype(jnp.float32), k.astype(jnp.float32)) * (XA_HEAD_DIM ** -0.5)
    p = jax.nn.softmax(s, axis=-1)
    o = jnp.einsum('bhqk,bkhd->bqhd', p, v.astype(jnp.float32)).reshape(B, S, XA_WIDTH).astype(xn.dtype)
    return jnp.einsum('bse,ed->bsd', o, wo)


def setup_inputs(seed: int = 0) -> dict:
    key = jax.random.key(seed)
    ks = jax.random.split(key, 26)
    def w(k, shape, fan_in, scale=1.0):
        return jax.random.normal(k, shape, jnp.float32) * (scale * fan_in ** -0.5)
    def gain(k, shape):
        return 1.0 + 0.02 * jax.random.normal(k, shape, jnp.float32)
    def small(k, shape):
        return 0.02 * jax.random.normal(k, shape, jnp.float32)
    x = jax.random.normal(ks[0], (BATCH, SEQ, D_MODEL), jnp.float32)
    mem = jax.random.normal(ks[1], (BATCH, MEM_LEN, D_MODEL), jnp.float32)
    offsets = jax.random.randint(ks[2], (BATCH, 1), 0, 4096, dtype=jnp.int32)
    positions = (jnp.arange(SEQ, dtype=jnp.int32)[None, :] + offsets).astype(jnp.int32)
    u = jax.random.uniform(ks[16], (N_ODD, C_WIDTH), jnp.float32, minval=0.9, maxval=0.999)
    a0 = u ** (1.0 / RG_C)
    lam = jnp.log(a0) - jnp.log1p(-a0)
    return {
        'x': x,
        'mem': mem,
        'positions': positions,
        'hgrn_lb_logits': 0.5 * jax.random.normal(ks[3], (DEPTH + 1, A_WIDTH), jnp.float32),
        'ev_norm': gain(ks[4], (N_EVEN, D_MODEL)),
        'ev_w_in': w(ks[5], (N_EVEN, D_MODEL, EVEN_IN), D_MODEL),
        'ev_hgrn_gain': gain(ks[6], (N_EVEN, A_WIDTH)),
        'ev_w_out': w(ks[7], (N_EVEN, EVEN_WIDTH, D_MODEL), EVEN_WIDTH),
        'od_norm': gain(ks[8], (N_ODD, D_MODEL)),
        'od_w_in': w(ks[9], (N_ODD, D_MODEL, ODD_IN), D_MODEL),
        'od_conv_w': w(ks[10], (N_ODD, CONV_WIDTH, C_WIDTH), CONV_WIDTH),
        'od_conv_b': small(ks[11], (N_ODD, C_WIDTH)),
        'od_rg_wa': w(ks[12], (N_ODD, C_BLOCKS, C_BLOCK_DIM, C_BLOCK_DIM), C_BLOCK_DIM),
        'od_rg_ba': small(ks[13], (N_ODD, C_WIDTH)),
        'od_rg_wx': w(ks[14], (N_ODD, C_BLOCKS, C_BLOCK_DIM, C_BLOCK_DIM), C_BLOCK_DIM),
        'od_rg_bx': small(ks[15], (N_ODD, C_WIDTH)),
        'od_rg_lambda': lam,
        'od_ret_gain': gain(ks[17], (N_ODD, D_WIDTH)),
        'od_ret_bias': small(ks[18], (N_ODD, D_WIDTH)),
        'od_w_out': w(ks[19], (N_ODD, ODD_WIDTH, D_MODEL), ODD_WIDTH),
        'xa_norm': gain(ks[20], (DEPTH, D_MODEL)),
        'xa_mem_norm': gain(ks[21], (DEPTH, D_MODEL)),
        'xa_wq': w(ks[22], (DEPTH, D_MODEL, XA_WIDTH), D_MODEL),
        'xa_wkv': w(ks[23], (DEPTH, D_MODEL, 2 * XA_WIDTH), D_MODEL),
        'xa_wo': w(ks[24], (DEPTH, XA_WIDTH, D_MODEL), XA_WIDTH),
        'final_norm': gain(ks[25], (D_MODEL,)),
    }


def reference(x, mem, positions, hgrn_lb_logits, ev_norm, ev_w_in, ev_hgrn_gain, ev_w_out,
              od_norm, od_w_in, od_conv_w, od_conv_b, od_rg_wa, od_rg_ba, od_rg_wx, od_rg_bx,
              od_rg_lambda, od_ret_gain, od_ret_bias, od_w_out,
              xa_norm, xa_mem_norm, xa_wq, xa_wkv, xa_wo, final_norm):
    lb_all = jnp.cumsum(jax.nn.softmax(hgrn_lb_logits.astype(jnp.float32), axis=0), axis=0)
    for layer in range(DEPTH):
        if layer % 2 == 0:
            e = layer // 2
            x = x + even_mixer(rms_norm(x, ev_norm[e]), positions, lb_all[layer],
                               ev_w_in[e], ev_hgrn_gain[e], ev_w_out[e])
        else:
            o = layer // 2
            x = x + odd_mixer(rms_norm(x, od_norm[o]), positions, od_w_in[o], od_conv_w[o], od_conv_b[o],
                              od_rg_wa[o], od_rg_ba[o], od_rg_wx[o], od_rg_bx[o], od_rg_lambda[o],
                              od_ret_gain[o], od_ret_bias[o], od_w_out[o])
        x = x + memory_cross_attention(rms_norm(x, xa_norm[layer]), mem, xa_mem_norm[layer],
                                       xa_wq[layer], xa_wkv[layer], xa_wo[layer])
    return rms_norm(x, final_norm)
```

```python
import functools

import jax
import jax.numpy as jnp
from jax import lax
from jax.experimental import pallas as pl
from jax.experimental.pallas import tpu as pltpu

F32 = jnp.float32
BF16 = jnp.bfloat16

D_MODEL = 1024
HEAD_DIM = 128
N_HEADS = 8
WIDTH = N_HEADS * HEAD_DIM
MEM_LEN = 256
HGRN_CHUNK = 32
HGRN_TILE = 128
MOBA_BLOCK = 256
MOBA_TOPK = 3
RET_CHUNK = 256
CONV_WIDTH = 4
RG_C = 8.0
ROPE_THETA = 500000.0
ROPE_DIM = HEAD_DIM // 4
RET_THETA = 10000.0
XA_HEADS = 4
XA_WIDTH = XA_HEADS * HEAD_DIM
EPS = 1e-6
VMEM_LIMIT = 56 * 1024 * 1024

_NT = (((1,), (1,)), ((), ()))
_TN = (((0,), (0,)), ((), ()))


def _dot(a, b):
    return jnp.dot(a, b, preferred_element_type=F32)


def _dot_nt(a, b):
    return lax.dot_general(a, b, _NT, preferred_element_type=F32)


def _dot_tn(a, b):
    return lax.dot_general(a, b, _TN, preferred_element_type=F32)


def _sigmoid(x):
    return 1.0 / (1.0 + jnp.exp(-x))


def _silu(x):
    return x * _sigmoid(x)


def _params(*sem):
    return pltpu.CompilerParams(dimension_semantics=sem, vmem_limit_bytes=VMEM_LIMIT)


def _norm_matmul_kernel(x_ref, g_ref, w_ref, o_ref, xn_ref):
    @pl.when(pl.program_id(1) == 0)
    def _():
        x = x_ref[...]
        ms = jnp.mean(x * x, axis=-1, keepdims=True)
        xn_ref[...] = (x * lax.rsqrt(ms + EPS) * g_ref[...]).astype(BF16)

    o_ref[...] = _dot(xn_ref[...], w_ref[...]).astype(o_ref.dtype)


def norm_matmul(x, gain, w, out_dtype, tm, tn, name):
    t, d = x.shape
    n = w.shape[1]
    return pl.pallas_call(
        _norm_matmul_kernel,
        out_shape=jax.ShapeDtypeStruct((t, n), out_dtype),
        grid=(t // tm, n // tn),
        in_specs=[
            pl.BlockSpec((tm, d), lambda i, j: (i, 0)),
            pl.BlockSpec((1, d), lambda i, j: (0, 0)),
            pl.BlockSpec((d, tn), lambda i, j: (0, j)),
        ],
        out_specs=pl.BlockSpec((tm, tn), lambda i, j: (i, j)),
        scratch_shapes=[pltpu.VMEM((tm, d), BF16)],
        compiler_params=_params("arbitrary", "arbitrary"),
        name=name,
    )(x, gain.reshape(1, d), w)


def _split3(x):
    hi = x.astype(BF16)
    r1 = x - hi.astype(F32)
    mid = r1.astype(BF16)
    lo = (r1 - mid.astype(F32)).astype(BF16)
    return hi, mid, lo


def _hgrn2_kernel(q_ref, f_ref, i_ref, z_ref, lb_ref, gain_ref, o_ref):
    seq = q_ref.shape[0]
    tile, chunk = HGRN_TILE, HGRN_CHUNK
    lb = lb_ref[...]
    gain = gain_ref[...]

    r = lax.broadcasted_iota(jnp.int32, (2 * tile, tile), 0)
    c = lax.broadcasted_iota(jnp.int32, (2 * tile, tile), 1)
    rr = jnp.where(r >= tile, r - tile, r)
    same_chunk = (rr // chunk) == (c // chunk)
    sum_mat = jnp.where(same_chunk & ((r >= tile) | (c <= rr)), 1.0, 0.0).astype(BF16)
    ti = lax.broadcasted_iota(jnp.int32, (tile, tile), 0)
    tj = lax.broadcasted_iota(jnp.int32, (tile, tile), 1)
    intra_mask = ((ti // chunk) == (tj // chunk)) & (tj <= ti)

    def tile_body(t, state_t):
        r0 = pl.multiple_of(t * tile, tile)
        rows = pl.ds(r0, tile)
        q = q_ref[rows, :]
        v = i_ref[rows, :]
        f = lb + (1.0 - lb) * _sigmoid(f_ref[rows, :])
        k = 1.0 - f
        hi, mid, lo = _split3(jnp.log(f))
        sums = _dot(sum_mat, hi) + _dot(sum_mat, mid) + _dot(sum_mat, lo)
        b = sums[:tile]
        b_tot = sums[tile:]
        q_dec = q * jnp.exp(b)
        k_inv = k * jnp.exp(-b)
        k_end = k * jnp.exp(b_tot - b)
        q_dec_b = q_dec.astype(BF16)
        v_b = v.astype(BF16)
        k_end_b = k_end.astype(BF16)
        att = _dot_nt(q_dec_b, k_inv.astype(BF16))
        att = jnp.where(intra_mask, att, 0.0)
        o_intra = _dot(att.astype(BF16), v_b)
        outs = []
        for ci in range(tile // chunk):
            cs = slice(ci * chunk, (ci + 1) * chunk)
            o_inter = _dot_nt(q_dec_b[cs], state_t.astype(BF16))
            u_t = _dot_tn(v_b[cs], k_end_b[cs])
            decay = jnp.exp(b_tot[ci * chunk:ci * chunk + 1, :])
            state_t = state_t * decay + u_t
            outs.append(o_intra[cs] + o_inter)
        o = jnp.concatenate(outs, axis=0)
        y = o * lax.rsqrt(jnp.mean(o * o, axis=-1, keepdims=True) + EPS) * gain
        o_ref[rows, :] = (y * _silu(z_ref[rows, :])).astype(o_ref.dtype)
        return state_t

    lax.fori_loop(0, seq // tile, tile_body, jnp.zeros((HEAD_DIM, HEAD_DIM), F32))


def hgrn2_mixer(h3, lb, gain, col_q, col_f, col_i, col_z):
    bsz, seq, _ = h3.shape

    def col(off):
        return pl.BlockSpec((None, seq, HEAD_DIM), lambda b, h: (b, 0, off + h))

    vec = pl.BlockSpec((None, 1, HEAD_DIM), lambda b, h: (h, 0, 0))
    return pl.pallas_call(
        _hgrn2_kernel,
        out_shape=jax.ShapeDtypeStruct((bsz, seq, WIDTH), BF16),
        grid=(bsz, N_HEADS),
        in_specs=[col(col_q), col(col_f), col(col_i), col(col_z), vec, vec],
        out_specs=pl.BlockSpec((None, seq, HEAD_DIM), lambda b, h: (b, 0, h)),
        compiler_params=_params("arbitrary", "arbitrary"),
        name="hgrn2",
    )(h3, h3, h3, h3, lb.reshape(N_HEADS, 1, HEAD_DIM), gain.reshape(N_HEADS, 1, HEAD_DIM))


def _moba_kernel(q_ref, k_ref, v_ref, z_ref, pos_ref, inv_ref, o_ref,
                 cos_ref, sin_lo_ref, sin_hi_ref, qr_ref, kr_ref, vb_ref):
    seq = q_ref.shape[0]
    blk = MOBA_BLOCK
    nb = seq // blk
    half = ROPE_DIM // 2
    scale = HEAD_DIM ** -0.5

    @pl.when(pl.program_id(1) == 0)
    def _():
        ang = pos_ref[...] * inv_ref[...]
        lane = lax.broadcasted_iota(jnp.int32, ang.shape, 1)
        sin = jnp.sin(ang)
        cos_ref[...] = jnp.cos(ang)
        sin_lo_ref[...] = jnp.where(lane < half, -sin, 0.0)
        sin_hi_ref[...] = jnp.where((lane >= half) & (lane < 2 * half), sin, 0.0)

    def rope(x):
        return (x * cos_ref[...]
                + pltpu.roll(x, HEAD_DIM - half, 1) * sin_lo_ref[...]
                + pltpu.roll(x, half, 1) * sin_hi_ref[...])

    kr = rope(k_ref[...])
    k_mean = jnp.mean(kr.reshape(nb, blk, HEAD_DIM), axis=1).astype(BF16)
    kr_ref[...] = kr.astype(BF16)
    qr_ref[...] = rope(q_ref[...]).astype(BF16)
    vb_ref[...] = v_ref[...].astype(BF16)

    row = lax.broadcasted_iota(jnp.int32, (blk, blk), 0)
    colm = lax.broadcasted_iota(jnp.int32, (blk, blk), 1)
    causal = colm <= row
    lane_nb = lax.broadcasted_iota(jnp.int32, (blk, nb), 1)
    neg_inf = -jnp.inf

    for qb in range(nb):
        rows = slice(qb * blk, (qb + 1) * blk)
        q = qr_ref[rows, :]
        n_keys = (qb + 1) * blk
        s = _dot_nt(q, kr_ref[:n_keys, :]) * scale
        if qb > MOBA_TOPK:
            gate = _dot_nt(q, k_mean)
            rank = jnp.zeros((blk, nb), F32)
            for j in range(qb):
                gj = gate[:, j:j + 1]
                ahead = (gj > gate) | ((gj == gate) & (lane_nb > j))
                rank = rank + jnp.where(ahead, 1.0, 0.0)
            selected = rank < float(MOBA_TOPK)
        pieces = []
        for j in range(qb):
            sj = s[:, j * blk:(j + 1) * blk]
            if qb > MOBA_TOPK:
                sj = jnp.where(selected[:, j:j + 1], sj, neg_inf)
            pieces.append(sj)
        pieces.append(jnp.where(causal, s[:, qb * blk:], neg_inf))
        s = jnp.concatenate(pieces, axis=1) if qb else pieces[0]
        m = jnp.max(s, axis=-1, keepdims=True)
        p = jnp.exp(s - m)
        l = jnp.sum(p, axis=-1, keepdims=True)
        o = _dot(p.astype(BF16), vb_ref[:n_keys, :]) / l
        o_ref[rows, :] = (o * _silu(z_ref[rows, :])).astype(o_ref.dtype)


def moba_mixer(h3, posf, inv_row, col_q, col_k, col_v, col_z):
    bsz, seq, _ = h3.shape

    def col(off):
        return pl.BlockSpec((None, seq, HEAD_DIM), lambda b, h: (b, 0, off + h))

    return pl.pallas_call(
        _moba_kernel,
        out_shape=jax.ShapeDtypeStruct((bsz, seq, WIDTH), BF16),
        grid=(bsz, N_HEADS),
        in_specs=[col(col_q), col(col_k), col(col_v), col(col_z),
                  pl.BlockSpec((None, seq, HEAD_DIM), lambda b, h: (b, 0, 0)),
                  pl.BlockSpec((1, HEAD_DIM), lambda b, h: (0, 0))],
        out_specs=pl.BlockSpec((None, seq, HEAD_DIM), lambda b, h: (b, 0, h)),
        scratch_shapes=[pltpu.VMEM((seq, HEAD_DIM), F32)] * 3 + [pltpu.VMEM((seq, HEAD_DIM), BF16)] * 3,
        compiler_params=_params("arbitrary", "arbitrary"),
        name="moba",
    )(h3, h3, h3, h3, posf, inv_row)


def _rglru_kernel(x_ref, z_ref, cw_ref, cb_ref, wa_ref, ba_ref, wx_ref, bx_ref, lam_ref,
                  o_ref, a_ref, u_ref):
    seq = x_ref.shape[0]
    sub = 8
    groups = seq // sub
    x = x_ref[...]
    t_idx = lax.broadcasted_iota(jnp.int32, x.shape, 0)

    xf = x * cw_ref[CONV_WIDTH - 1:CONV_WIDTH, :] + cb_ref[...]
    for d in range(1, CONV_WIDTH):
        shifted = jnp.where(t_idx >= d, pltpu.roll(x, d, 0), 0.0)
        xf = xf + shifted * cw_ref[CONV_WIDTH - 1 - d:CONV_WIDTH - d, :]

    xb = xf.astype(BF16)
    r = _sigmoid(_dot(xb, wa_ref[...]) + ba_ref[...])
    ig = _sigmoid(_dot(xb, wx_ref[...]) + bx_ref[...])
    nl = -lam_ref[...]
    softplus = jnp.maximum(nl, 0.0) + jnp.log1p(jnp.exp(-jnp.abs(nl)))
    log_a = (-RG_C) * r * softplus
    a = jnp.exp(log_a)
    u = jnp.sqrt(1.0 - jnp.exp(2.0 * log_a)) * ig * xf

    a3 = a.reshape(groups, sub, HEAD_DIM)
    u3 = u.reshape(groups, sub, HEAD_DIM)
    s_idx = lax.broadcasted_iota(jnp.int32, a3.shape, 1)
    for d in (1, 2, 4):
        keep = s_idx >= d
        a_prev = jnp.where(keep, pltpu.roll(a3, d, 1), 1.0)
        u_prev = jnp.where(keep, pltpu.roll(u3, d, 1), 0.0)
        u3 = u3 + a3 * u_prev
        a3 = a3 * a_prev
    a_ref[...] = a3
    u_ref[...] = u3

    def group_body(g, carry):
        hg = a_ref[g] * carry + u_ref[g]
        u_ref[g] = hg
        return jnp.broadcast_to(hg[sub - 1:sub, :], (sub, HEAD_DIM))

    lax.fori_loop(0, groups, group_body, jnp.zeros((sub, HEAD_DIM), F32), unroll=8)
    hs = u_ref[...].reshape(seq, HEAD_DIM)
    o_ref[...] = (hs * _silu(z_ref[...])).astype(o_ref.dtype)


def rglru_mixer(h3, conv_w, conv_b, wa, ba, wx, bx, lam, col_x, col_z):
    bsz, seq, _ = h3.shape
    nblk = WIDTH // HEAD_DIM

    def col(off):
        return pl.BlockSpec((None, seq, HEAD_DIM), lambda b, c: (b, 0, off + c))

    def vec(rows):
        return pl.BlockSpec((rows, HEAD_DIM), lambda b, c: (0, c))

    wspec = pl.BlockSpec((None, HEAD_DIM, HEAD_DIM), lambda b, c: (c, 0, 0))
    return pl.pallas_call(
        _rglru_kernel,
        out_shape=jax.ShapeDtypeStruct((bsz, seq, WIDTH), BF16),
        grid=(bsz, nblk),
        in_specs=[col(col_x), col(col_z), vec(CONV_WIDTH), vec(1), wspec, vec(1), wspec, vec(1), vec(1)],
        out_specs=pl.BlockSpec((None, seq, HEAD_DIM), lambda b, c: (b, 0, c)),
        scratch_shapes=[pltpu.VMEM((seq // 8, 8, HEAD_DIM), F32)] * 2,
        compiler_params=_params("arbitrary", "arbitrary"),
        name="rglru",
    )(h3, h3, conv_w, conv_b.reshape(1, WIDTH), wa.astype(BF16), ba.reshape(1, WIDTH),
      wx.astype(BF16), bx.reshape(1, WIDTH), lam.reshape(1, WIDTH))


def _retention_kernel(q_ref, k_ref, v_ref, z_ref, pos_ref, inv_ref, logg_ref, gain_ref, bias_ref,
                      o_ref, cos_ref, sin_ref):
    seq = q_ref.shape[0]
    ck = RET_CHUNK
    half = HEAD_DIM // 2

    @pl.when(pl.program_id(1) == 0)
    def _():
        ang = pos_ref[...] * inv_ref[...]
        lane = lax.broadcasted_iota(jnp.int32, ang.shape, 1)
        sin = jnp.sin(ang)
        cos_ref[...] = jnp.cos(ang)
        sin_ref[...] = jnp.where(lane < half, -sin, sin)

    log_g = logg_ref[...]
    ri = lax.broadcasted_iota(jnp.int32, (ck, ck), 0)
    ci = lax.broadcasted_iota(jnp.int32, (ck, ck), 1)
    diff = (ri - ci).astype(F32)
    dmask = jnp.where(ri >= ci, jnp.exp(jnp.maximum(diff, 0.0) * log_g[:, :1]), 0.0)
    idx = lax.broadcasted_iota(jnp.int32, (ck, HEAD_DIM), 0).astype(F32)
    q_fac = jnp.exp((idx + 1.0) * log_g)
    k_fac = jnp.exp((ck - 1.0 - idx) * log_g)
    chunk_decay = jnp.exp(float(ck) * log_g)
    gain = gain_ref[...]
    bias = bias_ref[...]

    state = jnp.zeros((HEAD_DIM, HEAD_DIM), F32)
    for c in range(seq // ck):
        rows = slice(c * ck, (c + 1) * ck)
        cos = cos_ref[rows, :]
        sin = sin_ref[rows, :]
        q = q_ref[rows, :]
        k = k_ref[rows, :]
        q = q * cos + pltpu.roll(q, half, 1) * sin
        k = (k * cos + pltpu.roll(k, half, 1) * sin) * (HEAD_DIM ** -0.5)
        v_b = v_ref[rows, :].astype(BF16)
        att = _dot_nt(q.astype(BF16), k.astype(BF16)) * dmask
        o = _dot(att.astype(BF16), v_b) + _dot((q * q_fac).astype(BF16), state.astype(BF16))
        state = state * chunk_decay + _dot_tn((k * k_fac).astype(BF16), v_b)
        mu = jnp.mean(o, axis=-1, keepdims=True)
        oc = o - mu
        var = jnp.mean(oc * oc, axis=-1, keepdims=True)
        y = oc * lax.rsqrt(var + EPS) * gain + bias
        o_ref[rows, :] = (y * _silu(z_ref[rows, :])).astype(o_ref.dtype)


def retention_mixer(h3, posf, inv_row, log_g, gain, bias, col_q, col_k, col_v, col_z):
    bsz, seq, _ = h3.shape

    def col(off):
        return pl.BlockSpec((None, seq, HEAD_DIM), lambda b, h: (b, 0, off + h))

    vec = pl.BlockSpec((None, 1, HEAD_DIM), lambda b, h: (h, 0, 0))
    return pl.pallas_call(
        _retention_kernel,
        out_shape=jax.ShapeDtypeStruct((bsz, seq, WIDTH), BF16),
        grid=(bsz, N_HEADS),
        in_specs=[col(col_q), col(col_k), col(col_v), col(col_z),
                  pl.BlockSpec((None, seq, HEAD_DIM), lambda b, h: (b, 0, 0)),
                  pl.BlockSpec((1, HEAD_DIM), lambda b, h: (0, 0)),
                  vec, vec, vec],
        out_specs=pl.BlockSpec((None, seq, HEAD_DIM), lambda b, h: (b, 0, h)),
        scratch_shapes=[pltpu.VMEM((seq, HEAD_DIM), F32)] * 2,
        compiler_params=_params("arbitrary", "arbitrary"),
        name="retention",
    )(h3, h3, h3, h3, posf, inv_row, log_g,
      gain.reshape(N_HEADS, 1, HEAD_DIM), bias.reshape(N_HEADS, 1, HEAD_DIM))


def _post_kernel(x_ref, ya_ref, yb_ref, woa_ref, wob_ref, g_ref, wq_ref, kv_ref, wo_ref, fg_ref,
                 o_ref, *, final):
    x1 = x_ref[...] + _dot(ya_ref[...], woa_ref[...]) + _dot(yb_ref[...], wob_ref[...])
    ms = jnp.mean(x1 * x1, axis=-1, keepdims=True)
    xn = (x1 * lax.rsqrt(ms + EPS) * g_ref[...]).astype(BF16)
    q = _dot(xn, wq_ref[...]).astype(BF16)
    scale = HEAD_DIM ** -0.5
    heads = []
    for hd in range(XA_HEADS):
        cs = slice(hd * HEAD_DIM, (hd + 1) * HEAD_DIM)
        k = kv_ref[:, cs]
        v = kv_ref[:, XA_WIDTH + hd * HEAD_DIM:XA_WIDTH + (hd + 1) * HEAD_DIM]
        s = _dot_nt(q[:, cs], k) * scale
        p = jnp.exp(s - jnp.max(s, axis=-1, keepdims=True))
        p = p / jnp.sum(p, axis=-1, keepdims=True)
        heads.append(_dot(p.astype(BF16), v).astype(BF16))
    o = jnp.concatenate(heads, axis=1)
    x2 = x1 + _dot(o, wo_ref[...])
    if final:
        ms2 = jnp.mean(x2 * x2, axis=-1, keepdims=True)
        x2 = x2 * lax.rsqrt(ms2 + EPS) * fg_ref[...]
    o_ref[...] = x2


def post_mixer(x3, ya, yb, w_out, xa_gain, wq, kv, wo, final_gain, final, tm=512):
    bsz, seq, d = x3.shape
    full = lambda shape: pl.BlockSpec(shape, lambda b, i: (0,) * len(shape))
    tile = lambda w: pl.BlockSpec((None, tm, w), lambda b, i: (b, i, 0))
    return pl.pallas_call(
        functools.partial(_post_kernel, final=final),
        out_shape=jax.ShapeDtypeStruct((bsz, seq, d), F32),
        grid=(bsz, seq // tm),
        in_specs=[tile(d), tile(WIDTH), tile(WIDTH),
                  full((WIDTH, d)), full((WIDTH, d)), full((1, d)), full((d, XA_WIDTH)),
                  pl.BlockSpec((None, MEM_LEN, 2 * XA_WIDTH), lambda b, i: (b, 0, 0)),
                  full((XA_WIDTH, d)), full((1, d))],
        out_specs=tile(d),
        compiler_params=_params("arbitrary", "arbitrary"),
        name="post_final" if final else "post",
    )(x3, ya, yb, w_out[:WIDTH].astype(BF16), w_out[WIDTH:].astype(BF16), xa_gain.reshape(1, d),
      wq.astype(BF16), kv, wo.astype(BF16), final_gain.reshape(1, d))


def _rope_row(rot_dim, theta):
    half = rot_dim // 2
    inv = theta ** (-jnp.arange(half, dtype=F32) / half)
    row = jnp.concatenate([inv, inv, jnp.zeros((HEAD_DIM - rot_dim,), F32)])
    return row.reshape(1, HEAD_DIM)


def kernel(x, mem, positions, hgrn_lb_logits, ev_norm, ev_w_in, ev_hgrn_gain, ev_w_out, od_norm, od_w_in, od_conv_w, od_conv_b, od_rg_wa, od_rg_ba, od_rg_wx, od_rg_bx, od_rg_lambda, od_ret_gain, od_ret_bias, od_w_out, xa_norm, xa_mem_norm, xa_wq, xa_wkv, xa_wo, final_norm):
    bsz, seq, d = x.shape
    tokens = bsz * seq
    lb_all = jnp.cumsum(jax.nn.softmax(hgrn_lb_logits.astype(F32), axis=0), axis=0)
    posf = jnp.broadcast_to(positions.astype(F32)[:, :, None], (bsz, seq, HEAD_DIM))
    log_g = jnp.log(1.0 - 2.0 ** (-5.0 - jnp.arange(N_HEADS, dtype=F32)))
    log_g = jnp.broadcast_to(log_g[:, None, None], (N_HEADS, 1, HEAD_DIM))
    mem2 = mem.reshape(bsz * MEM_LEN, d)

    def memory_kv(layer):
        kv = norm_matmul(mem2, xa_mem_norm[layer], xa_wkv[layer].astype(BF16), BF16,
                         tm=bsz * MEM_LEN // 2, tn=2 * XA_WIDTH, name="mem_kv")
        return kv.reshape(bsz, MEM_LEN, 2 * XA_WIDTH)

    h = norm_matmul(x.reshape(tokens, d), ev_norm[0], ev_w_in[0].astype(BF16), F32,
                    tm=1024, tn=1024, name="in_proj_even")
    h3 = h.reshape(bsz, seq, -1)
    nb = WIDTH // HEAD_DIM
    ya = hgrn2_mixer(h3, lb_all[0], ev_hgrn_gain[0], 0, nb, 2 * nb, 6 * nb)
    yb = moba_mixer(h3, posf, _rope_row(ROPE_DIM, ROPE_THETA), 3 * nb, 4 * nb, 5 * nb, 7 * nb)
    x1 = post_mixer(x, ya, yb, ev_w_out[0], xa_norm[0], xa_wq[0], memory_kv(0), xa_wo[0],
                    final_norm, final=False)

    h = norm_matmul(x1.reshape(tokens, d), od_norm[0], od_w_in[0].astype(BF16), F32,
                    tm=1024, tn=1024, name="in_proj_odd")
    h3 = h.reshape(bsz, seq, -1)
    yc = rglru_mixer(h3, od_conv_w[0], od_conv_b[0], od_rg_wa[0], od_rg_ba[0], od_rg_wx[0],
                     od_rg_bx[0], od_rg_lambda[0], 0, 4 * nb)
    yd = retention_mixer(h3, posf, _rope_row(HEAD_DIM, RET_THETA), log_g, od_ret_gain[0],
                         od_ret_bias[0], nb, 2 * nb, 3 * nb, 5 * nb)
    return post_mixer(x1, yc, yd, od_w_out[0], xa_norm[1], xa_wq[1], memory_kv(1), xa_wo[1],
                      final_norm, final=True)
```

```python
import functools
import math

import jax
import jax.numpy as jnp
from jax import lax
from jax.experimental import pallas as pl
from jax.experimental.pallas import tpu as pltpu

F32 = jnp.float32
BF16 = jnp.bfloat16

D_MODEL = 1024
HEAD_DIM = 128
N_HEADS = 8
WIDTH = N_HEADS * HEAD_DIM
MEM_LEN = 256
HGRN_CHUNK = 32
HGRN_TILE = 128
MOBA_BLOCK = 256
MOBA_TOPK = 3
RET_CHUNK = 256
CONV_WIDTH = 4
RG_C = 8.0
ROPE_THETA = 500000.0
ROPE_DIM = HEAD_DIM // 4
RET_THETA = 10000.0
XA_HEADS = 4
XA_WIDTH = XA_HEADS * HEAD_DIM
EPS = 1e-6
VMEM_LIMIT = 56 * 1024 * 1024
MASKED = -1e30

_NT = (((1,), (1,)), ((), ()))
_TN = (((0,), (0,)), ((), ()))


def _dot(a, b):
    return jnp.dot(a, b, preferred_element_type=F32)


def _dot_nt(a, b):
    return lax.dot_general(a, b, _NT, preferred_element_type=F32)


def _dot_tn(a, b):
    return lax.dot_general(a, b, _TN, preferred_element_type=F32)


def _sigmoid(x):
    return 1.0 / (1.0 + jnp.exp(-x))


def _silu(x):
    return x * _sigmoid(x)


def _params(*sem):
    return pltpu.CompilerParams(dimension_semantics=sem, vmem_limit_bytes=VMEM_LIMIT)


def _norm_matmul_kernel(x_ref, g_ref, w_ref, o_ref, xn_ref):
    @pl.when(pl.program_id(1) == 0)
    def _():
        x = x_ref[...]
        ms = jnp.mean(x * x, axis=-1, keepdims=True)
        xn_ref[...] = (x * lax.rsqrt(ms + EPS) * g_ref[...]).astype(BF16)

    o_ref[...] = _dot(xn_ref[...], w_ref[...]).astype(o_ref.dtype)


def norm_matmul(x, gain, w, out_dtype, tm, tn, name):
    t, d = x.shape
    n = w.shape[1]
    return pl.pallas_call(
        _norm_matmul_kernel,
        out_shape=jax.ShapeDtypeStruct((t, n), out_dtype),
        grid=(t // tm, n // tn),
        in_specs=[
            pl.BlockSpec((tm, d), lambda i, j: (i, 0)),
            pl.BlockSpec((1, d), lambda i, j: (0, 0)),
            pl.BlockSpec((d, tn), lambda i, j: (0, j)),
        ],
        out_specs=pl.BlockSpec((tm, tn), lambda i, j: (i, j)),
        scratch_shapes=[pltpu.VMEM((tm, d), BF16)],
        compiler_params=_params("arbitrary", "arbitrary"),
        name=name,
    )(x, gain.reshape(1, d), w)


def _split3(x):
    hi = x.astype(BF16)
    r1 = x - hi.astype(F32)
    mid = r1.astype(BF16)
    lo = (r1 - mid.astype(F32)).astype(BF16)
    return hi, mid, lo


def _hgrn2_kernel(q_ref, f_ref, i_ref, z_ref, lb_ref, gain_ref, o_ref, st_ref):
    seq = q_ref.shape[0]
    tile, chunk = HGRN_TILE, HGRN_CHUNK
    per_tile = tile // chunk
    nt = seq // tile
    lb = lb_ref[...]
    gain = gain_ref[...]

    def tiles(x):
        return x.reshape(nt, tile, HEAD_DIM)

    def bmm(spec, a, b):
        return jnp.einsum(spec, a, b, preferred_element_type=F32)

    r = lax.broadcasted_iota(jnp.int32, (2 * tile, tile), 0)
    c = lax.broadcasted_iota(jnp.int32, (2 * tile, tile), 1)
    rr = jnp.where(r >= tile, r - tile, r)
    same_chunk = (rr // chunk) == (c // chunk)
    sum_mat = jnp.where(same_chunk & ((r >= tile) | (c <= rr)), 1.0, 0.0).astype(BF16)
    sum_mat = jnp.broadcast_to(sum_mat[None], (nt, 2 * tile, tile))
    ti = lax.broadcasted_iota(jnp.int32, (tile, tile), 0)
    tj = lax.broadcasted_iota(jnp.int32, (tile, tile), 1)
    intra_mask = (((ti // chunk) == (tj // chunk)) & (tj <= ti))[None]
    row_chunk = (lax.broadcasted_iota(jnp.int32, (tile, HEAD_DIM), 0) // chunk)[None]

    def per_chunk_lanes(x):
        zero = jnp.zeros_like(x)
        return jnp.concatenate([jnp.where(row_chunk == ci, x, zero) for ci in range(per_tile)], axis=-1)

    f = lb + (1.0 - lb) * _sigmoid(f_ref[...].astype(F32))
    k = tiles(1.0 - f)
    hi, mid, lo = _split3(jnp.log(f))
    sums = (bmm('trc,tcd->trd', sum_mat, tiles(hi)) + bmm('trc,tcd->trd', sum_mat, tiles(mid))
            + bmm('trc,tcd->trd', sum_mat, tiles(lo)))
    b = sums[:, :tile]
    b_tot = sums[:, tile:]
    q_dec = (tiles(q_ref[...].astype(F32)) * jnp.exp(b)).astype(BF16)
    k_inv = (k * jnp.exp(-b)).astype(BF16)
    k_end = (k * jnp.exp(b_tot - b)).astype(BF16)
    v = tiles(i_ref[...].astype(F32))
    att = bmm('tid,tjd->tij', q_dec, k_inv)
    att = jnp.where(intra_mask, att, 0.0).astype(BF16)
    o_intra = bmm('tij,tjd->tid', att, v.astype(BF16))

    v_t = jnp.swapaxes(v, 1, 2).astype(BF16)
    incr = bmm('tvn,tnk->tvk', v_t, per_chunk_lanes(k_end))

    state_t = jnp.zeros((HEAD_DIM, HEAD_DIM), F32)
    for t in range(nt):
        for ci in range(per_tile):
            lanes = slice(ci * HEAD_DIM, (ci + 1) * HEAD_DIM)
            st_ref[t, :, lanes] = state_t.astype(BF16)
            decay = jnp.exp(b_tot[t, ci * chunk:ci * chunk + 1, :])
            state_t = state_t * decay + incr[t, :, lanes]

    o = o_intra + bmm('tnk,tvk->tnv', per_chunk_lanes(q_dec), st_ref[...])
    o = o.reshape(seq, HEAD_DIM)
    y = o * lax.rsqrt(jnp.mean(o * o, axis=-1, keepdims=True) + EPS) * gain
    o_ref[...] = (y * _silu(z_ref[...].astype(F32))).astype(o_ref.dtype)


def hgrn2_mixer(h3, lb, gain, col_q, col_f, col_i, col_z):
    bsz, seq, _ = h3.shape

    def col(off):
        return pl.BlockSpec((None, seq, HEAD_DIM), lambda b, h: (b, 0, off + h))

    vec = pl.BlockSpec((None, 1, HEAD_DIM), lambda b, h: (h, 0, 0))
    return pl.pallas_call(
        _hgrn2_kernel,
        out_shape=jax.ShapeDtypeStruct((bsz, seq, WIDTH), BF16),
        grid=(bsz, N_HEADS),
        in_specs=[col(col_q), col(col_f), col(col_i), col(col_z), vec, vec],
        out_specs=pl.BlockSpec((None, seq, HEAD_DIM), lambda b, h: (b, 0, h)),
        scratch_shapes=[pltpu.VMEM((seq // HGRN_TILE, HEAD_DIM, HGRN_TILE // HGRN_CHUNK * HEAD_DIM), BF16)],
        compiler_params=_params("arbitrary", "arbitrary"),
        name="hgrn2",
    )(h3, h3, h3, h3, lb.reshape(N_HEADS, 1, HEAD_DIM), gain.reshape(N_HEADS, 1, HEAD_DIM))


def _moba_kernel(q_ref, k_ref, v_ref, z_ref, pos_ref, inv_ref, o_ref,
                 cos_ref, sin_lo_ref, sin_hi_ref, qr_ref, ka_ref):
    seq = q_ref.shape[0]
    blk = MOBA_BLOCK
    nb = seq // blk
    half = ROPE_DIM // 2
    exp2_scale = HEAD_DIM ** -0.5 * math.log2(math.e)

    @pl.when(pl.program_id(1) == 0)
    def _():
        ang = pos_ref[...] * inv_ref[...]
        lane = lax.broadcasted_iota(jnp.int32, ang.shape, 1)
        sin = jnp.sin(ang)
        cos_ref[...] = jnp.cos(ang)
        sin_lo_ref[...] = jnp.where(lane < half, -sin, 0.0)
        sin_hi_ref[...] = jnp.where((lane >= half) & (lane < 2 * half), sin, 0.0)

    def rope(x):
        return (x * cos_ref[...]
                + pltpu.roll(x, HEAD_DIM - half, 1) * sin_lo_ref[...]
                + pltpu.roll(x, half, 1) * sin_hi_ref[...])

    kr = rope(k_ref[...].astype(F32))
    k_mean = jnp.mean(kr.reshape(nb, blk, HEAD_DIM), axis=1).astype(BF16)
    ka_ref[:, :HEAD_DIM] = kr.astype(BF16)
    key_blk = lax.broadcasted_iota(jnp.int32, (seq, HEAD_DIM), 0) // blk
    key_lane = lax.broadcasted_iota(jnp.int32, (seq, HEAD_DIM), 1)
    ka_ref[:, HEAD_DIM:] = jnp.where(key_blk == key_lane, MASKED, 0.0).astype(BF16)
    qr_ref[...] = rope(q_ref[...].astype(F32)).astype(BF16)
    gate_all = _dot_nt(k_mean, qr_ref[...])

    row = lax.broadcasted_iota(jnp.int32, (blk, blk), 0)
    colm = lax.broadcasted_iota(jnp.int32, (blk, blk), 1)
    causal = colm <= row
    eye = jnp.where(colm == row, 1.0, 0.0).astype(BF16)
    blk_row = lax.broadcasted_iota(jnp.int32, (nb, blk), 0)

    for qb in range(nb):
        rows = slice(qb * blk, (qb + 1) * blk)
        q = qr_ref[rows, :]
        n_keys = (qb + 1) * blk
        if qb > MOBA_TOPK:
            gate = gate_all[:, rows]
            rank = jnp.zeros((nb, blk), F32)
            for j in range(qb):
                gj = gate[j:j + 1, :]
                ahead = (gj > gate) | ((gj == gate) & (blk_row > j))
                rank = rank + jnp.where(ahead, 1.0, 0.0)
            drop = jnp.where((rank >= float(MOBA_TOPK)) & (blk_row < qb), 1.0, 0.0)
            drop = jnp.concatenate([drop, jnp.zeros((HEAD_DIM - nb, blk), F32)], axis=0).astype(BF16)
            drop_col = _dot_nt(eye, drop).astype(BF16)
            s = _dot_nt(jnp.concatenate([q, drop_col], axis=1), ka_ref[:n_keys, :])
        else:
            s = _dot_nt(q, ka_ref[:n_keys, :HEAD_DIM])
        own = jnp.where(causal, s[:, qb * blk:], MASKED)
        s = jnp.concatenate([s[:, :qb * blk], own], axis=1) if qb else own
        m = jnp.max(s, axis=-1, keepdims=True)
        p = jnp.exp2((s - m) * exp2_scale)
        l = jnp.sum(p, axis=-1, keepdims=True)
        o = _dot(p.astype(BF16), v_ref[:n_keys, :]) / l
        o_ref[rows, :] = (o * _silu(z_ref[rows, :].astype(F32))).astype(o_ref.dtype)


def moba_mixer(h3, posf, inv_row, col_q, col_k, col_v, col_z):
    bsz, seq, _ = h3.shape

    def col(off):
        return pl.BlockSpec((None, seq, HEAD_DIM), lambda b, h: (b, 0, off + h))

    return pl.pallas_call(
        _moba_kernel,
        out_shape=jax.ShapeDtypeStruct((bsz, seq, WIDTH), BF16),
        grid=(bsz, N_HEADS),
        in_specs=[col(col_q), col(col_k), col(col_v), col(col_z),
                  pl.BlockSpec((None, seq, HEAD_DIM), lambda b, h: (b, 0, 0)),
                  pl.BlockSpec((1, HEAD_DIM), lambda b, h: (0, 0))],
        out_specs=pl.BlockSpec((None, seq, HEAD_DIM), lambda b, h: (b, 0, h)),
        scratch_shapes=[pltpu.VMEM((seq, HEAD_DIM), F32)] * 3 + [
            pltpu.VMEM((seq, HEAD_DIM), BF16), pltpu.VMEM((seq, 2 * HEAD_DIM), BF16)],
        compiler_params=_params("arbitrary", "arbitrary"),
        name="moba",
    )(h3, h3, h3, h3, posf, inv_row)


def _rglru_kernel(x_ref, z_ref, cw_ref, cb_ref, wa_ref, ba_ref, wx_ref, bx_ref, lam_ref,
                  o_ref, a_ref, u_ref):
    seq = x_ref.shape[0]
    sub = 8
    groups = seq // sub
    x = x_ref[...].astype(F32)
    t_idx = lax.broadcasted_iota(jnp.int32, x.shape, 0)

    xf = x * cw_ref[CONV_WIDTH - 1:CONV_WIDTH, :] + cb_ref[...]
    for d in range(1, CONV_WIDTH):
        shifted = jnp.where(t_idx >= d, pltpu.roll(x, d, 0), 0.0)
        xf = xf + shifted * cw_ref[CONV_WIDTH - 1 - d:CONV_WIDTH - d, :]

    xb = xf.astype(BF16)
    r = _sigmoid(_dot(xb, wa_ref[...]) + ba_ref[...])
    ig = _sigmoid(_dot(xb, wx_ref[...]) + bx_ref[...])
    nl = -lam_ref[...]
    softplus = jnp.maximum(nl, 0.0) + jnp.log1p(jnp.exp(-jnp.abs(nl)))
    log_a = (-RG_C) * r * softplus
    a = jnp.exp(log_a)
    u = jnp.sqrt(1.0 - jnp.exp(2.0 * log_a)) * ig * xf

    a3 = a.reshape(groups, sub, HEAD_DIM)
    u3 = u.reshape(groups, sub, HEAD_DIM)
    s_idx = lax.broadcasted_iota(jnp.int32, a3.shape, 1)
    for d in (1, 2, 4):
        keep = s_idx >= d
        a_prev = jnp.where(keep, pltpu.roll(a3, d, 1), 1.0)
        u_prev = jnp.where(keep, pltpu.roll(u3, d, 1), 0.0)
        u3 = u3 + a3 * u_prev
        a3 = a3 * a_prev
    a_ref[...] = a3
    u_ref[...] = u3

    def group_body(g, carry):
        hg = a_ref[g] * carry + u_ref[g]
        u_ref[g] = hg
        return jnp.broadcast_to(hg[sub - 1:sub, :], (sub, HEAD_DIM))

    lax.fori_loop(0, groups, group_body, jnp.zeros((sub, HEAD_DIM), F32), unroll=8)
    hs = u_ref[...].reshape(seq, HEAD_DIM)
    o_ref[...] = (hs * _silu(z_ref[...].astype(F32))).astype(o_ref.dtype)


def rglru_mixer(h3, conv_w, conv_b, wa, ba, wx, bx, lam, col_x, col_z):
    bsz, seq, _ = h3.shape
    nblk = WIDTH // HEAD_DIM

    def col(off):
        return pl.BlockSpec((None, seq, HEAD_DIM), lambda b, c: (b, 0, off + c))

    def vec(rows):
        return pl.BlockSpec((rows, HEAD_DIM), lambda b, c: (0, c))

    wspec = pl.BlockSpec((None, HEAD_DIM, HEAD_DIM), lambda b, c: (c, 0, 0))
    return pl.pallas_call(
        _rglru_kernel,
        out_shape=jax.ShapeDtypeStruct((bsz, seq, WIDTH), BF16),
        grid=(bsz, nblk),
        in_specs=[col(col_x), col(col_z), vec(CONV_WIDTH), vec(1), wspec, vec(1), wspec, vec(1), vec(1)],
        out_specs=pl.BlockSpec((None, seq, HEAD_DIM), lambda b, c: (b, 0, c)),
        scratch_shapes=[pltpu.VMEM((seq // 8, 8, HEAD_DIM), F32)] * 2,
        compiler_params=_params("arbitrary", "arbitrary"),
        name="rglru",
    )(h3, h3, conv_w, conv_b.reshape(1, WIDTH), wa.astype(BF16), ba.reshape(1, WIDTH),
      wx.astype(BF16), bx.reshape(1, WIDTH), lam.reshape(1, WIDTH))


def _retention_kernel(q_ref, k_ref, v_ref, z_ref, pos_ref, inv_ref, logg_ref, gain_ref, bias_ref,
                      o_ref, cos_ref, sin_ref):
    seq = q_ref.shape[0]
    ck = RET_CHUNK
    half = HEAD_DIM // 2

    @pl.when(pl.program_id(1) == 0)
    def _():
        ang = pos_ref[...] * inv_ref[...]
        lane = lax.broadcasted_iota(jnp.int32, ang.shape, 1)
        sin = jnp.sin(ang)
        cos_ref[...] = jnp.cos(ang)
        sin_ref[...] = jnp.where(lane < half, -sin, sin)

    log_g = logg_ref[...]
    ri = lax.broadcasted_iota(jnp.int32, (ck, ck), 0)
    ci = lax.broadcasted_iota(jnp.int32, (ck, ck), 1)
    diff = (ri - ci).astype(F32)
    dmask = jnp.where(ri >= ci, jnp.exp(jnp.maximum(diff, 0.0) * log_g[:, :1]), 0.0)
    idx = lax.broadcasted_iota(jnp.int32, (ck, HEAD_DIM), 0).astype(F32)
    q_fac = jnp.exp((idx + 1.0) * log_g)
    k_fac = jnp.exp((ck - 1.0 - idx) * log_g)
    chunk_decay = jnp.exp(float(ck) * log_g)
    gain = gain_ref[...]
    bias = bias_ref[...]

    state = jnp.zeros((HEAD_DIM, HEAD_DIM), F32)
    for c in range(seq // ck):
        rows = slice(c * ck, (c + 1) * ck)
        cos = cos_ref[rows, :]
        sin = sin_ref[rows, :]
        q = q_ref[rows, :].astype(F32)
        k = k_ref[rows, :].astype(F32)
        q = q * cos + pltpu.roll(q, half, 1) * sin
        k = (k * cos + pltpu.roll(k, half, 1) * sin) * (HEAD_DIM ** -0.5)
        v_b = v_ref[rows, :]
        att = _dot_nt(q.astype(BF16), k.astype(BF16)) * dmask
        o = _dot(att.astype(BF16), v_b) + _dot((q * q_fac).astype(BF16), state.astype(BF16))
        state = state * chunk_decay + _dot_tn((k * k_fac).astype(BF16), v_b)
        mu = jnp.mean(o, axis=-1, keepdims=True)
        oc = o - mu
        var = jnp.mean(oc * oc, axis=-1, keepdims=True)
        y = oc * lax.rsqrt(var + EPS) * gain + bias
        o_ref[rows, :] = (y * _silu(z_ref[rows, :].astype(F32))).astype(o_ref.dtype)


def retention_mixer(h3, posf, inv_row, log_g, gain, bias, col_q, col_k, col_v, col_z):
    bsz, seq, _ = h3.shape

    def col(off):
        return pl.BlockSpec((None, seq, HEAD_DIM), lambda b, h: (b, 0, off + h))

    vec = pl.BlockSpec((None, 1, HEAD_DIM), lambda b, h: (h, 0, 0))
    return pl.pallas_call(
        _retention_kernel,
        out_shape=jax.ShapeDtypeStruct((bsz, seq, WIDTH), BF16),
        grid=(bsz, N_HEADS),
        in_specs=[col(col_q), col(col_k), col(col_v), col(col_z),
                  pl.BlockSpec((None, seq, HEAD_DIM), lambda b, h: (b, 0, 0)),
                  pl.BlockSpec((1, HEAD_DIM), lambda b, h: (0, 0)),
                  vec, vec, vec],
        out_specs=pl.BlockSpec((None, seq, HEAD_DIM), lambda b, h: (b, 0, h)),
        scratch_shapes=[pltpu.VMEM((seq, HEAD_DIM), F32)] * 2,
        compiler_params=_params("arbitrary", "arbitrary"),
        name="retention",
    )(h3, h3, h3, h3, posf, inv_row, log_g,
      gain.reshape(N_HEADS, 1, HEAD_DIM), bias.reshape(N_HEADS, 1, HEAD_DIM))


def _post_kernel(x_ref, ya_ref, yb_ref, woa_ref, wob_ref, g_ref, wq_ref, kv_ref, wo_ref, fg_ref,
                 o_ref, *, final):
    x1 = x_ref[...] + _dot(ya_ref[...], woa_ref[...]) + _dot(yb_ref[...], wob_ref[...])
    ms = jnp.mean(x1 * x1, axis=-1, keepdims=True)
    xn = (x1 * lax.rsqrt(ms + EPS) * g_ref[...]).astype(BF16)
    q = _dot(xn, wq_ref[...]).astype(BF16)
    scale = HEAD_DIM ** -0.5
    heads = []
    for hd in range(XA_HEADS):
        cs = slice(hd * HEAD_DIM, (hd + 1) * HEAD_DIM)
        k = kv_ref[:, cs]
        v = kv_ref[:, XA_WIDTH + hd * HEAD_DIM:XA_WIDTH + (hd + 1) * HEAD_DIM]
        s = _dot_nt(q[:, cs], k) * scale
        p = jnp.exp(s - jnp.max(s, axis=-1, keepdims=True))
        p = p / jnp.sum(p, axis=-1, keepdims=True)
        heads.append(_dot(p.astype(BF16), v).astype(BF16))
    o = jnp.concatenate(heads, axis=1)
    x2 = x1 + _dot(o, wo_ref[...])
    if final:
        ms2 = jnp.mean(x2 * x2, axis=-1, keepdims=True)
        x2 = x2 * lax.rsqrt(ms2 + EPS) * fg_ref[...]
    o_ref[...] = x2


def post_mixer(x3, ya, yb, w_out, xa_gain, wq, kv, wo, final_gain, final, tm=512):
    bsz, seq, d = x3.shape
    full = lambda shape: pl.BlockSpec(shape, lambda b, i: (0,) * len(shape))
    tile = lambda w: pl.BlockSpec((None, tm, w), lambda b, i: (b, i, 0))
    return pl.pallas_call(
        functools.partial(_post_kernel, final=final),
        out_shape=jax.ShapeDtypeStruct((bsz, seq, d), F32),
        grid=(bsz, seq // tm),
        in_specs=[tile(d), tile(WIDTH), tile(WIDTH),
                  full((WIDTH, d)), full((WIDTH, d)), full((1, d)), full((d, XA_WIDTH)),
                  pl.BlockSpec((None, MEM_LEN, 2 * XA_WIDTH), lambda b, i: (b, 0, 0)),
                  full((XA_WIDTH, d)), full((1, d))],
        out_specs=tile(d),
        compiler_params=_params("arbitrary", "arbitrary"),
        name="post_final" if final else "post",
    )(x3, ya, yb, w_out[:WIDTH].astype(BF16), w_out[WIDTH:].astype(BF16), xa_gain.reshape(1, d),
      wq.astype(BF16), kv, wo.astype(BF16), final_gain.reshape(1, d))


def _rope_row(rot_dim, theta):
    half = rot_dim // 2
    inv = theta ** (-jnp.arange(half, dtype=F32) / half)
    row = jnp.concatenate([inv, inv, jnp.zeros((HEAD_DIM - rot_dim,), F32)])
    return row.reshape(1, HEAD_DIM)


def kernel(x, mem, positions, hgrn_lb_logits, ev_norm, ev_w_in, ev_hgrn_gain, ev_w_out, od_norm, od_w_in, od_conv_w, od_conv_b, od_rg_wa, od_rg_ba, od_rg_wx, od_rg_bx, od_rg_lambda, od_ret_gain, od_ret_bias, od_w_out, xa_norm, xa_mem_norm, xa_wq, xa_wkv, xa_wo, final_norm):
    bsz, seq, d = x.shape
    tokens = bsz * seq
    lb_all = jnp.cumsum(jax.nn.softmax(hgrn_lb_logits.astype(F32), axis=0), axis=0)
    posf = jnp.broadcast_to(positions.astype(F32)[:, :, None], (bsz, seq, HEAD_DIM))
    log_g = jnp.log(1.0 - 2.0 ** (-5.0 - jnp.arange(N_HEADS, dtype=F32)))
    log_g = jnp.broadcast_to(log_g[:, None, None], (N_HEADS, 1, HEAD_DIM))
    mem2 = mem.reshape(bsz * MEM_LEN, d)

    def memory_kv(layer):
        kv = norm_matmul(mem2, xa_mem_norm[layer], xa_wkv[layer].astype(BF16), BF16,
                         tm=bsz * MEM_LEN // 2, tn=2 * XA_WIDTH, name="mem_kv")
        return kv.reshape(bsz, MEM_LEN, 2 * XA_WIDTH)

    h = norm_matmul(x.reshape(tokens, d), ev_norm[0], ev_w_in[0].astype(BF16), BF16,
                    tm=2048, tn=1024, name="in_proj_even")
    h3 = h.reshape(bsz, seq, -1)
    nb = WIDTH // HEAD_DIM
    ya = hgrn2_mixer(h3, lb_all[0], ev_hgrn_gain[0], 0, nb, 2 * nb, 6 * nb)
    yb = moba_mixer(h3, posf, _rope_row(ROPE_DIM, ROPE_THETA), 3 * nb, 4 * nb, 5 * nb, 7 * nb)
    x1 = post_mixer(x, ya, yb, ev_w_out[0], xa_norm[0], xa_wq[0], memory_kv(0), xa_wo[0],
                    final_norm, final=False)

    h = norm_matmul(x1.reshape(tokens, d), od_norm[0], od_w_in[0].astype(BF16), BF16,
                    tm=2048, tn=1024, name="in_proj_odd")
    h3 = h.reshape(bsz, seq, -1)
    yc = rglru_mixer(h3, od_conv_w[0], od_conv_b[0], od_rg_wa[0], od_rg_ba[0], od_rg_wx[0],
                     od_rg_bx[0], od_rg_lambda[0], 0, 4 * nb)
    yd = retention_mixer(h3, posf, _rope_row(HEAD_DIM, RET_THETA), log_g, od_ret_gain[0],
                         od_ret_bias[0], nb, 2 * nb, 3 * nb, 5 * nb)
    return post_mixer(x1, yc, yd, od_w_out[0], xa_norm[1], xa_wq[1], memory_kv(1), xa_wo[1],
                      final_norm, final=True)
```

```python
import functools
import math

import jax
import jax.numpy as jnp
from jax import lax
from jax.experimental import pallas as pl
from jax.experimental.pallas import tpu as pltpu

F32 = jnp.float32
BF16 = jnp.bfloat16

D_MODEL = 1024
HEAD_DIM = 128
N_HEADS = 8
WIDTH = N_HEADS * HEAD_DIM
MEM_LEN = 256
HGRN_CHUNK = 32
HGRN_TILE = 128
MOBA_BLOCK = 256
MOBA_TOPK = 3
RET_CHUNK = 256
CONV_WIDTH = 4
RG_C = 8.0
ROPE_THETA = 500000.0
ROPE_DIM = HEAD_DIM // 4
RET_THETA = 10000.0
XA_HEADS = 4
XA_WIDTH = XA_HEADS * HEAD_DIM
EPS = 1e-6
VMEM_LIMIT = 56 * 1024 * 1024
MASKED = -1e30

_NT = (((1,), (1,)), ((), ()))
_TN = (((0,), (0,)), ((), ()))


def _dot(a, b):
    return jnp.dot(a, b, preferred_element_type=F32)


def _dot_nt(a, b):
    return lax.dot_general(a, b, _NT, preferred_element_type=F32)


def _dot_tn(a, b):
    return lax.dot_general(a, b, _TN, preferred_element_type=F32)


def _sigmoid(x):
    return 1.0 / (1.0 + jnp.exp(-x))


def _silu(x):
    return x * _sigmoid(x)


def _params(*sem):
    return pltpu.CompilerParams(dimension_semantics=sem, vmem_limit_bytes=VMEM_LIMIT)


def _norm_matmul_kernel(x_ref, g_ref, w_ref, o_ref, xn_ref):
    @pl.when(pl.program_id(1) == 0)
    def _():
        x = x_ref[...]
        ms = jnp.mean(x * x, axis=-1, keepdims=True)
        xn_ref[...] = (x * lax.rsqrt(ms + EPS) * g_ref[...]).astype(BF16)

    o_ref[...] = _dot(xn_ref[...], w_ref[...]).astype(o_ref.dtype)


def norm_matmul(x, gain, w, out_dtype, tm, tn, name):
    t, d = x.shape
    n = w.shape[1]
    return pl.pallas_call(
        _norm_matmul_kernel,
        out_shape=jax.ShapeDtypeStruct((t, n), out_dtype),
        grid=(t // tm, n // tn),
        in_specs=[
            pl.BlockSpec((tm, d), lambda i, j: (i, 0)),
            pl.BlockSpec((1, d), lambda i, j: (0, 0)),
            pl.BlockSpec((d, tn), lambda i, j: (0, j)),
        ],
        out_specs=pl.BlockSpec((tm, tn), lambda i, j: (i, j)),
        scratch_shapes=[pltpu.VMEM((tm, d), BF16)],
        compiler_params=_params("arbitrary", "arbitrary"),
        name=name,
    )(x, gain.reshape(1, d), w)


def _split2(x):
    hi = x.astype(BF16)
    lo = (x - hi.astype(F32)).astype(BF16)
    return hi, lo


def _hgrn2_kernel(q_ref, f_ref, i_ref, z_ref, lb_ref, gain_ref, o_ref, st_ref):
    seq = q_ref.shape[0]
    tile, chunk = HGRN_TILE, HGRN_CHUNK
    per_tile = tile // chunk
    nt = seq // tile
    lb = lb_ref[...]
    gain = gain_ref[...]

    def tiles(x):
        return x.reshape(nt, tile, HEAD_DIM)

    def bmm(spec, a, b):
        return jnp.einsum(spec, a, b, preferred_element_type=F32)

    ti = lax.broadcasted_iota(jnp.int32, (tile, tile), 0)
    tj = lax.broadcasted_iota(jnp.int32, (tile, tile), 1)
    intra_mask = (((ti // chunk) == (tj // chunk)) & (tj <= ti))[None]
    sum_mat = jnp.broadcast_to(jnp.where(intra_mask, 1.0, 0.0).astype(BF16), (nt, tile, tile))
    row_chunk = (lax.broadcasted_iota(jnp.int32, (tile, HEAD_DIM), 0) // chunk)[None]

    def per_chunk_lanes(x):
        zero = jnp.zeros_like(x)
        return jnp.concatenate([jnp.where(row_chunk == ci, x, zero) for ci in range(per_tile)], axis=-1)

    f = lb + (1.0 - lb) * _sigmoid(f_ref[...].astype(F32))
    k = tiles(1.0 - f)
    hi, lo = _split2(jnp.log(f))
    b = bmm('trc,tcd->trd', sum_mat, tiles(hi)) + bmm('trc,tcd->trd', sum_mat, tiles(lo))
    b_chunks = b.reshape(seq // chunk, chunk, HEAD_DIM)
    b_tot = jnp.broadcast_to(b_chunks[:, chunk - 1:, :], b_chunks.shape).reshape(b.shape)
    q_dec = (tiles(q_ref[...].astype(F32)) * jnp.exp(b)).astype(BF16)
    k_inv = (k * jnp.exp(-b)).astype(BF16)
    k_end = (k * jnp.exp(b_tot - b)).astype(BF16)
    v = tiles(i_ref[...].astype(F32))
    att = bmm('tid,tjd->tij', q_dec, k_inv)
    att = jnp.where(intra_mask, att, 0.0).astype(BF16)
    o_intra = bmm('tij,tjd->tid', att, v.astype(BF16))

    v_t = jnp.swapaxes(v, 1, 2).astype(BF16)
    incr = bmm('tvn,tnk->tvk', v_t, per_chunk_lanes(k_end))

    state_t = jnp.zeros((HEAD_DIM, HEAD_DIM), F32)
    for t in range(nt):
        for ci in range(per_tile):
            lanes = slice(ci * HEAD_DIM, (ci + 1) * HEAD_DIM)
            st_ref[t, :, lanes] = state_t.astype(BF16)
            decay = jnp.exp(b_tot[t, ci * chunk:ci * chunk + 1, :])
            state_t = state_t * decay + incr[t, :, lanes]

    o = o_intra + bmm('tnk,tvk->tnv', per_chunk_lanes(q_dec), st_ref[...])
    o = o.reshape(seq, HEAD_DIM)
    y = o * lax.rsqrt(jnp.mean(o * o, axis=-1, keepdims=True) + EPS) * gain
    o_ref[...] = (y * _silu(z_ref[...].astype(F32))).astype(o_ref.dtype)


def hgrn2_mixer(h3, lb, gain, col_q, col_f, col_i, col_z):
    bsz, seq, _ = h3.shape

    def col(off):
        return pl.BlockSpec((None, seq, HEAD_DIM), lambda b, h: (b, 0, off + h))

    vec = pl.BlockSpec((None, 1, HEAD_DIM), lambda b, h: (h, 0, 0))
    return pl.pallas_call(
        _hgrn2_kernel,
        out_shape=jax.ShapeDtypeStruct((bsz, seq, WIDTH), BF16),
        grid=(bsz, N_HEADS),
        in_specs=[col(col_q), col(col_f), col(col_i), col(col_z), vec, vec],
        out_specs=pl.BlockSpec((None, seq, HEAD_DIM), lambda b, h: (b, 0, h)),
        scratch_shapes=[pltpu.VMEM((seq // HGRN_TILE, HEAD_DIM, HGRN_TILE // HGRN_CHUNK * HEAD_DIM), BF16)],
        compiler_params=_params("arbitrary", "arbitrary"),
        name="hgrn2",
    )(h3, h3, h3, h3, lb.reshape(N_HEADS, 1, HEAD_DIM), gain.reshape(N_HEADS, 1, HEAD_DIM))


def _moba_kernel(q_ref, k_ref, v_ref, z_ref, pos_ref, inv_ref, o_ref,
                 cos_ref, sin_lo_ref, sin_hi_ref, qr_ref, ka_ref, va_ref):
    seq = q_ref.shape[0]
    blk = MOBA_BLOCK
    nb = seq // blk
    half = ROPE_DIM // 2
    exp2_scale = HEAD_DIM ** -0.5 * math.log2(math.e)

    @pl.when(pl.program_id(1) == 0)
    def _():
        ang = pos_ref[...] * inv_ref[...]
        lane = lax.broadcasted_iota(jnp.int32, ang.shape, 1)
        sin = jnp.sin(ang)
        cos_ref[...] = jnp.cos(ang)
        sin_lo_ref[...] = jnp.where(lane < half, -sin, 0.0)
        sin_hi_ref[...] = jnp.where((lane >= half) & (lane < 2 * half), sin, 0.0)

    def rope(x):
        return (x * cos_ref[...]
                + pltpu.roll(x, HEAD_DIM - half, 1) * sin_lo_ref[...]
                + pltpu.roll(x, half, 1) * sin_hi_ref[...])

    kr = rope(k_ref[...].astype(F32))
    k_mean = jnp.mean(kr.reshape(nb, blk, HEAD_DIM), axis=1).astype(BF16)
    ka_ref[:, :HEAD_DIM] = kr.astype(BF16)
    key_blk = lax.broadcasted_iota(jnp.int32, (seq, HEAD_DIM), 0) // blk
    key_lane = lax.broadcasted_iota(jnp.int32, (seq, HEAD_DIM), 1)
    ka_ref[:, HEAD_DIM:] = jnp.where(key_blk == key_lane, MASKED, 0.0).astype(BF16)
    qr_ref[...] = (rope(q_ref[...].astype(F32)) * exp2_scale).astype(BF16)
    va_ref[:, :HEAD_DIM] = v_ref[...]
    va_ref[:, HEAD_DIM:] = jnp.ones((seq, HEAD_DIM), BF16)
    gate_all = _dot_nt(k_mean, qr_ref[...])

    row = lax.broadcasted_iota(jnp.int32, (blk, blk), 0)
    colm = lax.broadcasted_iota(jnp.int32, (blk, blk), 1)
    causal = colm <= row
    eye = jnp.where(colm == row, 1.0, 0.0).astype(BF16)
    blk_row = lax.broadcasted_iota(jnp.int32, (nb, blk), 0)

    for qb in range(nb):
        rows = slice(qb * blk, (qb + 1) * blk)
        q = qr_ref[rows, :]
        n_keys = (qb + 1) * blk
        if qb > MOBA_TOPK:
            gate = gate_all[:, rows]
            rank = jnp.zeros((nb, blk), F32)
            for j in range(qb):
                gj = gate[j:j + 1, :]
                ahead = (gj > gate) | ((gj == gate) & (blk_row > j))
                rank = rank + jnp.where(ahead, 1.0, 0.0)
            drop = jnp.where((rank >= float(MOBA_TOPK)) & (blk_row < qb), 1.0, 0.0)
            drop = jnp.concatenate([drop, jnp.zeros((HEAD_DIM - nb, blk), F32)], axis=0).astype(BF16)
            drop_col = _dot_nt(eye, drop).astype(BF16)
            s = _dot_nt(jnp.concatenate([q, drop_col], axis=1), ka_ref[:n_keys, :])
        else:
            s = _dot_nt(q, ka_ref[:n_keys, :HEAD_DIM])
        own = jnp.where(causal, s[:, qb * blk:], MASKED)
        s = jnp.concatenate([s[:, :qb * blk], own], axis=1) if qb else own
        m = jnp.max(s, axis=-1, keepdims=True)
        p = jnp.exp2(s - m).astype(BF16)
        o = _dot(p, va_ref[:n_keys, :])
        o = o[:, :HEAD_DIM] / o[:, HEAD_DIM:]
        o_ref[rows, :] = (o * _silu(z_ref[rows, :].astype(F32))).astype(o_ref.dtype)


def moba_mixer(h3, posf, inv_row, col_q, col_k, col_v, col_z):
    bsz, seq, _ = h3.shape

    def col(off):
        return pl.BlockSpec((None, seq, HEAD_DIM), lambda b, h: (b, 0, off + h))

    return pl.pallas_call(
        _moba_kernel,
        out_shape=jax.ShapeDtypeStruct((bsz, seq, WIDTH), BF16),
        grid=(bsz, N_HEADS),
        in_specs=[col(col_q), col(col_k), col(col_v), col(col_z),
                  pl.BlockSpec((None, seq, HEAD_DIM), lambda b, h: (b, 0, 0)),
                  pl.BlockSpec((1, HEAD_DIM), lambda b, h: (0, 0))],
        out_specs=pl.BlockSpec((None, seq, HEAD_DIM), lambda b, h: (b, 0, h)),
        scratch_shapes=[pltpu.VMEM((seq, HEAD_DIM), F32)] * 3 + [
            pltpu.VMEM((seq, HEAD_DIM), BF16), pltpu.VMEM((seq, 2 * HEAD_DIM), BF16),
            pltpu.VMEM((seq, 2 * HEAD_DIM), BF16)],
        compiler_params=_params("arbitrary", "arbitrary"),
        name="moba",
    )(h3, h3, h3, h3, posf, inv_row)


def _rglru_kernel(x_ref, z_ref, cw_ref, cb_ref, wa_ref, ba_ref, wx_ref, bx_ref, lam_ref,
                  o_ref, a_ref, u_ref, xpad_ref):
    seq = x_ref.shape[0]
    sub = 8
    groups = seq // sub

    xpad_ref[:sub, :] = jnp.zeros((sub, HEAD_DIM), F32)
    xpad_ref[sub:, :] = x_ref[...].astype(F32)
    xf = cb_ref[...]
    for d in range(CONV_WIDTH):
        xf = xf + xpad_ref[sub - d:sub - d + seq, :] * cw_ref[CONV_WIDTH - 1 - d:CONV_WIDTH - d, :]

    xb = xf.astype(BF16)
    r = _sigmoid(_dot(xb, wa_ref[...]) + ba_ref[...])
    ig = _sigmoid(_dot(xb, wx_ref[...]) + bx_ref[...])
    nl = -lam_ref[...]
    softplus = jnp.maximum(nl, 0.0) + jnp.log1p(jnp.exp(-jnp.abs(nl)))
    a = jnp.exp(r * ((-RG_C) * softplus))
    u = jnp.sqrt(1.0 - a * a) * ig * xf

    a3 = a.reshape(groups, sub, HEAD_DIM)
    u3 = u.reshape(groups, sub, HEAD_DIM)
    s_idx = lax.broadcasted_iota(jnp.int32, a3.shape, 1)
    for d in (1, 2, 4):
        keep = s_idx >= d
        a_prev = jnp.where(keep, pltpu.roll(a3, d, 1), 1.0)
        u_prev = jnp.where(keep, pltpu.roll(u3, d, 1), 0.0)
        u3 = u3 + a3 * u_prev
        a3 = a3 * a_prev
    a_ref[...] = a3
    u_ref[...] = u3

    def group_body(g, carry):
        hg = a_ref[g] * carry + u_ref[g]
        u_ref[g] = hg
        return jnp.broadcast_to(hg[sub - 1:sub, :], (sub, HEAD_DIM))

    lax.fori_loop(0, groups, group_body, jnp.zeros((sub, HEAD_DIM), F32), unroll=8)
    hs = u_ref[...].reshape(seq, HEAD_DIM)
    o_ref[...] = (hs * _silu(z_ref[...].astype(F32))).astype(o_ref.dtype)


def rglru_mixer(h3, conv_w, conv_b, wa, ba, wx, bx, lam, col_x, col_z):
    bsz, seq, _ = h3.shape
    nblk = WIDTH // HEAD_DIM

    def col(off):
        return pl.BlockSpec((None, seq, HEAD_DIM), lambda b, c: (b, 0, off + c))

    def vec(rows):
        return pl.BlockSpec((rows, HEAD_DIM), lambda b, c: (0, c))

    wspec = pl.BlockSpec((None, HEAD_DIM, HEAD_DIM), lambda b, c: (c, 0, 0))
    return pl.pallas_call(
        _rglru_kernel,
        out_shape=jax.ShapeDtypeStruct((bsz, seq, WIDTH), BF16),
        grid=(bsz, nblk),
        in_specs=[col(col_x), col(col_z), vec(CONV_WIDTH), vec(1), wspec, vec(1), wspec, vec(1), vec(1)],
        out_specs=pl.BlockSpec((None, seq, HEAD_DIM), lambda b, c: (b, 0, c)),
        scratch_shapes=[pltpu.VMEM((seq // 8, 8, HEAD_DIM), F32)] * 2 + [pltpu.VMEM((seq + 8, HEAD_DIM), F32)],
        compiler_params=_params("arbitrary", "arbitrary"),
        name="rglru",
    )(h3, h3, conv_w, conv_b.reshape(1, WIDTH), wa.astype(BF16), ba.reshape(1, WIDTH),
      wx.astype(BF16), bx.reshape(1, WIDTH), lam.reshape(1, WIDTH))


def _retention_kernel(q_ref, k_ref, v_ref, z_ref, pos_ref, inv_ref, logg_ref, gain_ref, bias_ref,
                      o_ref, cos_ref, sin_ref, st_ref):
    seq = q_ref.shape[0]
    ck = RET_CHUNK
    half = HEAD_DIM // 2

    nc = seq // ck

    @pl.when(pl.program_id(1) == 0)
    def _():
        ang = pos_ref[...] * inv_ref[...]
        lane = lax.broadcasted_iota(jnp.int32, ang.shape, 1)
        sin = jnp.sin(ang)
        cos_ref[...] = jnp.cos(ang)
        sin_ref[...] = jnp.where(lane < half, -sin, sin)

    def chunks(x):
        return x.reshape(nc, ck, HEAD_DIM)

    def bmm(spec, a, b):
        return jnp.einsum(spec, a, b, preferred_element_type=F32)

    log_g = logg_ref[...]
    ri = lax.broadcasted_iota(jnp.int32, (ck, ck), 0)
    ci = lax.broadcasted_iota(jnp.int32, (ck, ck), 1)
    diff = (ri - ci).astype(F32)
    dmask = jnp.where(ri >= ci, jnp.exp(jnp.maximum(diff, 0.0) * log_g[:, :1]), 0.0)
    idx = lax.broadcasted_iota(jnp.int32, (ck, HEAD_DIM), 0).astype(F32)
    q_fac = jnp.exp((idx + 1.0) * log_g)
    k_fac = jnp.exp((ck - 1.0 - idx) * log_g)
    chunk_decay = jnp.exp(float(ck) * log_g)

    cos = cos_ref[...]
    sin = sin_ref[...]
    q = q_ref[...].astype(F32)
    k = k_ref[...].astype(F32)
    q = chunks(q * cos + pltpu.roll(q, half, 1) * sin)
    k = chunks((k * cos + pltpu.roll(k, half, 1) * sin) * (HEAD_DIM ** -0.5))
    v = chunks(v_ref[...])
    att = bmm('cid,cjd->cij', q.astype(BF16), k.astype(BF16)) * dmask[None]
    o_intra = bmm('cij,cjd->cid', att.astype(BF16), v)
    k_end_t = jnp.swapaxes(k * k_fac[None], 1, 2).astype(BF16)
    incr = bmm('ckn,cnv->ckv', k_end_t, v)

    state = jnp.zeros((HEAD_DIM, HEAD_DIM), F32)
    for c in range(nc):
        st_ref[c] = state.astype(BF16)
        state = state * chunk_decay + incr[c]

    o = o_intra + bmm('cnk,ckv->cnv', (q * q_fac[None]).astype(BF16), st_ref[...])
    o = o.reshape(seq, HEAD_DIM)
    mu = jnp.mean(o, axis=-1, keepdims=True)
    oc = o - mu
    var = jnp.mean(oc * oc, axis=-1, keepdims=True)
    y = oc * lax.rsqrt(var + EPS) * gain_ref[...] + bias_ref[...]
    o_ref[...] = (y * _silu(z_ref[...].astype(F32))).astype(o_ref.dtype)


def retention_mixer(h3, posf, inv_row, log_g, gain, bias, col_q, col_k, col_v, col_z):
    bsz, seq, _ = h3.shape

    def col(off):
        return pl.BlockSpec((None, seq, HEAD_DIM), lambda b, h: (b, 0, off + h))

    vec = pl.BlockSpec((None, 1, HEAD_DIM), lambda b, h: (h, 0, 0))
    return pl.pallas_call(
        _retention_kernel,
        out_shape=jax.ShapeDtypeStruct((bsz, seq, WIDTH), BF16),
        grid=(bsz, N_HEADS),
        in_specs=[col(col_q), col(col_k), col(col_v), col(col_z),
                  pl.BlockSpec((None, seq, HEAD_DIM), lambda b, h: (b, 0, 0)),
                  pl.BlockSpec((1, HEAD_DIM), lambda b, h: (0, 0)),
                  vec, vec, vec],
        out_specs=pl.BlockSpec((None, seq, HEAD_DIM), lambda b, h: (b, 0, h)),
        scratch_shapes=[pltpu.VMEM((seq, HEAD_DIM), F32)] * 2 + [
            pltpu.VMEM((seq // RET_CHUNK, HEAD_DIM, HEAD_DIM), BF16)],
        compiler_params=_params("arbitrary", "arbitrary"),
        name="retention",
    )(h3, h3, h3, h3, posf, inv_row, log_g,
      gain.reshape(N_HEADS, 1, HEAD_DIM), bias.reshape(N_HEADS, 1, HEAD_DIM))


def _post_kernel(x_ref, ya_ref, yb_ref, woa_ref, wob_ref, g_ref, wq_ref, kv_ref, wo_ref, fg_ref,
                 o_ref, *, final):
    x1 = x_ref[...] + _dot(ya_ref[...], woa_ref[...]) + _dot(yb_ref[...], wob_ref[...])
    ms = jnp.mean(x1 * x1, axis=-1, keepdims=True)
    xn = (x1 * lax.rsqrt(ms + EPS) * g_ref[...]).astype(BF16)
    q = _dot(xn, wq_ref[...]).astype(BF16)
    scale = HEAD_DIM ** -0.5
    heads = []
    for hd in range(XA_HEADS):
        cs = slice(hd * HEAD_DIM, (hd + 1) * HEAD_DIM)
        k = kv_ref[:, cs]
        v = kv_ref[:, XA_WIDTH + hd * HEAD_DIM:XA_WIDTH + (hd + 1) * HEAD_DIM]
        s = _dot_nt(q[:, cs], k) * scale
        p = jnp.exp(s - jnp.max(s, axis=-1, keepdims=True))
        p = p / jnp.sum(p, axis=-1, keepdims=True)
        heads.append(_dot(p.astype(BF16), v).astype(BF16))
    o = jnp.concatenate(heads, axis=1)
    x2 = x1 + _dot(o, wo_ref[...])
    if final:
        ms2 = jnp.mean(x2 * x2, axis=-1, keepdims=True)
        x2 = x2 * lax.rsqrt(ms2 + EPS) * fg_ref[...]
    o_ref[...] = x2


def post_mixer(x3, ya, yb, w_out, xa_gain, wq, kv, wo, final_gain, final, tm=512):
    bsz, seq, d = x3.shape
    full = lambda shape: pl.BlockSpec(shape, lambda b, i: (0,) * len(shape))
    tile = lambda w: pl.BlockSpec((None, tm, w), lambda b, i: (b, i, 0))
    return pl.pallas_call(
        functools.partial(_post_kernel, final=final),
        out_shape=jax.ShapeDtypeStruct((bsz, seq, d), F32),
        grid=(bsz, seq // tm),
        in_specs=[tile(d), tile(WIDTH), tile(WIDTH),
                  full((WIDTH, d)), full((WIDTH, d)), full((1, d)), full((d, XA_WIDTH)),
                  pl.BlockSpec((None, MEM_LEN, 2 * XA_WIDTH), lambda b, i: (b, 0, 0)),
                  full((XA_WIDTH, d)), full((1, d))],
        out_specs=tile(d),
        compiler_params=_params("arbitrary", "arbitrary"),
        name="post_final" if final else "post",
    )(x3, ya, yb, w_out[:WIDTH].astype(BF16), w_out[WIDTH:].astype(BF16), xa_gain.reshape(1, d),
      wq.astype(BF16), kv, wo.astype(BF16), final_gain.reshape(1, d))


def _rope_row(rot_dim, theta):
    half = rot_dim // 2
    inv = theta ** (-jnp.arange(half, dtype=F32) / half)
    row = jnp.concatenate([inv, inv, jnp.zeros((HEAD_DIM - rot_dim,), F32)])
    return row.reshape(1, HEAD_DIM)


def kernel(x, mem, positions, hgrn_lb_logits, ev_norm, ev_w_in, ev_hgrn_gain, ev_w_out, od_norm, od_w_in, od_conv_w, od_conv_b, od_rg_wa, od_rg_ba, od_rg_wx, od_rg_bx, od_rg_lambda, od_ret_gain, od_ret_bias, od_w_out, xa_norm, xa_mem_norm, xa_wq, xa_wkv, xa_wo, final_norm):
    bsz, seq, d = x.shape
    tokens = bsz * seq
    lb_all = jnp.cumsum(jax.nn.softmax(hgrn_lb_logits.astype(F32), axis=0), axis=0)
    posf = jnp.broadcast_to(positions.astype(F32)[:, :, None], (bsz, seq, HEAD_DIM))
    log_g = jnp.log(1.0 - 2.0 ** (-5.0 - jnp.arange(N_HEADS, dtype=F32)))
    log_g = jnp.broadcast_to(log_g[:, None, None], (N_HEADS, 1, HEAD_DIM))
    mem2 = mem.reshape(bsz * MEM_LEN, d)

    def memory_kv(layer):
        kv = norm_matmul(mem2, xa_mem_norm[layer], xa_wkv[layer].astype(BF16), BF16,
                         tm=bsz * MEM_LEN // 2, tn=2 * XA_WIDTH, name="mem_kv")
        return kv.reshape(bsz, MEM_LEN, 2 * XA_WIDTH)

    h = norm_matmul(x.reshape(tokens, d), ev_norm[0], ev_w_in[0].astype(BF16), BF16,
                    tm=2048, tn=1024, name="in_proj_even")
    h3 = h.reshape(bsz, seq, -1)
    nb = WIDTH // HEAD_DIM
    ya = hgrn2_mixer(h3, lb_all[0], ev_hgrn_gain[0], 0, nb, 2 * nb, 6 * nb)
    yb = moba_mixer(h3, posf, _rope_row(ROPE_DIM, ROPE_THETA), 3 * nb, 4 * nb, 5 * nb, 7 * nb)
    x1 = post_mixer(x, ya, yb, ev_w_out[0], xa_norm[0], xa_wq[0], memory_kv(0), xa_wo[0],
                    final_norm, final=False)

    h = norm_matmul(x1.reshape(tokens, d), od_norm[0], od_w_in[0].astype(BF16), BF16,
                    tm=2048, tn=1024, name="in_proj_odd")
    h3 = h.reshape(bsz, seq, -1)
    yc = rglru_mixer(h3, od_conv_w[0], od_conv_b[0], od_rg_wa[0], od_rg_ba[0], od_rg_wx[0],
                     od_rg_bx[0], od_rg_lambda[0], 0, 4 * nb)
    yd = retention_mixer(h3, posf, _rope_row(HEAD_DIM, RET_THETA), log_g, od_ret_gain[0],
                         od_ret_bias[0], nb, 2 * nb, 3 * nb, 5 * nb)
    return post_mixer(x1, yc, yd, od_w_out[0], xa_norm[1], xa_wq[1], memory_kv(1), xa_wo[1],
                      final_norm, final=True)
```

```python
import functools
import math

import jax
import jax.numpy as jnp
from jax import lax
from jax.experimental import pallas as pl
from jax.experimental.pallas import tpu as pltpu

F32 = jnp.float32
BF16 = jnp.bfloat16

D_MODEL = 1024
HEAD_DIM = 128
N_HEADS = 8
WIDTH = N_HEADS * HEAD_DIM
MEM_LEN = 256
HGRN_CHUNK = 32
HGRN_TILE = 128
HGRN_PARTS = 4
MOBA_BLOCK = 256
MOBA_TOPK = 3
RET_CHUNK = 256
RET_PARTS = 2
CONV_WIDTH = 4
RG_C = 8.0
ROPE_THETA = 500000.0
ROPE_DIM = HEAD_DIM // 4
RET_THETA = 10000.0
XA_HEADS = 4
XA_WIDTH = XA_HEADS * HEAD_DIM
EPS = 1e-6
VMEM_LIMIT = 56 * 1024 * 1024
MASKED = -1e30

_NT = (((1,), (1,)), ((), ()))
_TN = (((0,), (0,)), ((), ()))


def _dot(a, b):
    return jnp.dot(a, b, preferred_element_type=F32)


def _dot_nt(a, b):
    return lax.dot_general(a, b, _NT, preferred_element_type=F32)


def _dot_tn(a, b):
    return lax.dot_general(a, b, _TN, preferred_element_type=F32)


def _sigmoid(x):
    return 1.0 / (1.0 + jnp.exp(-x))


def _silu(x):
    return x * _sigmoid(x)


def _run_staggered(parts, n_stages):
    for step in range(n_stages + len(parts) - 1):
        for p, gen in enumerate(parts):
            if 0 <= step - p < n_stages:
                next(gen)


def _params(*sem):
    return pltpu.CompilerParams(dimension_semantics=sem, vmem_limit_bytes=VMEM_LIMIT)


def _norm_matmul_kernel(x_ref, g_ref, w_ref, o_ref, xn_ref):
    @pl.when(pl.program_id(1) == 0)
    def _():
        x = x_ref[...]
        ms = jnp.mean(x * x, axis=-1, keepdims=True)
        xn_ref[...] = (x * lax.rsqrt(ms + EPS) * g_ref[...]).astype(BF16)

    o_ref[...] = _dot(xn_ref[...], w_ref[...].astype(BF16)).astype(o_ref.dtype)


def norm_matmul(x, gain, w, out_dtype, tm, tn, name):
    t, d = x.shape
    n = w.shape[1]
    return pl.pallas_call(
        _norm_matmul_kernel,
        out_shape=jax.ShapeDtypeStruct((t, n), out_dtype),
        grid=(t // tm, n // tn),
        in_specs=[
            pl.BlockSpec((tm, d), lambda i, j: (i, 0)),
            pl.BlockSpec((1, d), lambda i, j: (0, 0)),
            pl.BlockSpec((d, tn), lambda i, j: (0, j)),
        ],
        out_specs=pl.BlockSpec((tm, tn), lambda i, j: (i, j)),
        scratch_shapes=[pltpu.VMEM((tm, d), BF16)],
        compiler_params=_params("arbitrary", "arbitrary"),
        name=name,
    )(x, gain.reshape(1, d), w)


def _split2(x):
    hi = x.astype(BF16)
    lo = (x - hi.astype(F32)).astype(BF16)
    return hi, lo


def _hgrn2_kernel(q_ref, f_ref, i_ref, z_ref, lb_ref, gain_ref, o_ref, st_ref):
    seq = q_ref.shape[0]
    tile, chunk = HGRN_TILE, HGRN_CHUNK
    per_tile = tile // chunk
    part_rows = seq // HGRN_PARTS
    nt = part_rows // tile
    lb = lb_ref[...]
    gain = gain_ref[...]

    def tiles(x):
        return x.reshape(nt, tile, HEAD_DIM)

    def bmm(spec, a, b):
        return jnp.einsum(spec, a, b, preferred_element_type=F32)

    ti = lax.broadcasted_iota(jnp.int32, (tile, tile), 0)
    tj = lax.broadcasted_iota(jnp.int32, (tile, tile), 1)
    intra_mask = (((ti // chunk) == (tj // chunk)) & (tj <= ti))[None]
    sum_mat = jnp.broadcast_to(jnp.where(intra_mask, 1.0, 0.0).astype(BF16), (nt, tile, tile))
    row_chunk = (lax.broadcasted_iota(jnp.int32, (tile, HEAD_DIM), 0) // chunk)[None]

    def per_chunk_lanes(x):
        zero = jnp.zeros_like(x)
        return jnp.concatenate([jnp.where(row_chunk == ci, x, zero) for ci in range(per_tile)], axis=-1)

    carried = [jnp.zeros((HEAD_DIM, HEAD_DIM), F32)]

    def part(p):
        rows = slice(p * part_rows, (p + 1) * part_rows)
        f = lb + (1.0 - lb) * _sigmoid(f_ref[rows, :].astype(F32))
        k = tiles(1.0 - f)
        hi, lo = _split2(jnp.log(f))
        yield
        b = bmm('trc,tcd->trd', sum_mat, tiles(hi)) + bmm('trc,tcd->trd', sum_mat, tiles(lo))
        b_chunks = b.reshape(part_rows // chunk, chunk, HEAD_DIM)
        b_tot = jnp.broadcast_to(b_chunks[:, chunk - 1:, :], b_chunks.shape).reshape(b.shape)
        yield
        q_dec = (tiles(q_ref[rows, :].astype(F32)) * jnp.exp(b)).astype(BF16)
        k_inv = (k * jnp.exp(-b)).astype(BF16)
        k_end = (k * jnp.exp(b_tot - b)).astype(BF16)
        v = tiles(i_ref[rows, :].astype(F32))
        yield
        att = bmm('tid,tjd->tij', q_dec, k_inv)
        att = jnp.where(intra_mask, att, 0.0).astype(BF16)
        v_t = jnp.swapaxes(v, 1, 2).astype(BF16)
        incr = bmm('tvn,tnk->tvk', v_t, per_chunk_lanes(k_end))
        yield
        o_intra = bmm('tij,tjd->tid', att, v.astype(BF16))
        state_t = carried[0]
        for t in range(nt):
            for ci in range(per_tile):
                lanes = slice(ci * HEAD_DIM, (ci + 1) * HEAD_DIM)
                st_ref[p * nt + t, :, lanes] = state_t.astype(BF16)
                decay = jnp.exp(b_tot[t, ci * chunk:ci * chunk + 1, :])
                state_t = state_t * decay + incr[t, :, lanes]
        carried[0] = state_t
        yield
        o = o_intra + bmm('tnk,tvk->tnv', per_chunk_lanes(q_dec), st_ref[p * nt:(p + 1) * nt])
        o = o.reshape(part_rows, HEAD_DIM)
        y = o * lax.rsqrt(jnp.mean(o * o, axis=-1, keepdims=True) + EPS) * gain
        o_ref[rows, :] = (y * _silu(z_ref[rows, :].astype(F32))).astype(o_ref.dtype)
        yield

    _run_staggered([part(p) for p in range(HGRN_PARTS)], n_stages=6)


def hgrn2_mixer(h3, lb, gain, col_q, col_f, col_i, col_z):
    bsz, seq, _ = h3.shape

    def col(off):
        return pl.BlockSpec((None, seq, HEAD_DIM), lambda b, h: (b, 0, off + h))

    vec = pl.BlockSpec((None, 1, HEAD_DIM), lambda b, h: (h, 0, 0))
    return pl.pallas_call(
        _hgrn2_kernel,
        out_shape=jax.ShapeDtypeStruct((bsz, seq, WIDTH), BF16),
        grid=(bsz, N_HEADS),
        in_specs=[col(col_q), col(col_f), col(col_i), col(col_z), vec, vec],
        out_specs=pl.BlockSpec((None, seq, HEAD_DIM), lambda b, h: (b, 0, h)),
        scratch_shapes=[pltpu.VMEM((seq // HGRN_TILE, HEAD_DIM, HGRN_TILE // HGRN_CHUNK * HEAD_DIM), BF16)],
        compiler_params=_params("arbitrary", "arbitrary"),
        name="hgrn2",
    )(h3, h3, h3, h3, lb.reshape(N_HEADS, 1, HEAD_DIM), gain.reshape(N_HEADS, 1, HEAD_DIM))


def _moba_kernel(q_ref, k_ref, v_ref, z_ref, pos_ref, inv_ref, o_ref,
                 cos_ref, sin_lo_ref, sin_hi_ref, qr_ref, ka_ref, va_ref):
    seq = q_ref.shape[0]
    blk = MOBA_BLOCK
    nb = seq // blk
    half = ROPE_DIM // 2
    exp2_scale = HEAD_DIM ** -0.5 * math.log2(math.e)

    @pl.when(pl.program_id(1) == 0)
    def _():
        ang = pos_ref[...] * inv_ref[...]
        lane = lax.broadcasted_iota(jnp.int32, ang.shape, 1)
        sin = jnp.sin(ang)
        cos_ref[...] = jnp.cos(ang)
        sin_lo_ref[...] = jnp.where(lane < half, -sin, 0.0)
        sin_hi_ref[...] = jnp.where((lane >= half) & (lane < 2 * half), sin, 0.0)

    def rope(x):
        return (x * cos_ref[...]
                + pltpu.roll(x, HEAD_DIM - half, 1) * sin_lo_ref[...]
                + pltpu.roll(x, half, 1) * sin_hi_ref[...])

    kr = rope(k_ref[...].astype(F32))
    k_mean = jnp.mean(kr.reshape(nb, blk, HEAD_DIM), axis=1).astype(BF16)
    ka_ref[:, :HEAD_DIM] = kr.astype(BF16)
    key_blk = lax.broadcasted_iota(jnp.int32, (seq, HEAD_DIM), 0) // blk
    key_lane = lax.broadcasted_iota(jnp.int32, (seq, HEAD_DIM), 1)
    ka_ref[:, HEAD_DIM:] = jnp.where(key_blk == key_lane, MASKED, 0.0).astype(BF16)
    qr_ref[...] = (rope(q_ref[...].astype(F32)) * exp2_scale).astype(BF16)
    va_ref[:, :HEAD_DIM] = v_ref[...]
    va_ref[:, HEAD_DIM:] = jnp.ones((seq, HEAD_DIM), BF16)
    gate_all = _dot_nt(k_mean, qr_ref[...])

    row = lax.broadcasted_iota(jnp.int32, (blk, blk), 0)
    colm = lax.broadcasted_iota(jnp.int32, (blk, blk), 1)
    causal = colm <= row
    eye = jnp.where(colm == row, 1.0, 0.0).astype(BF16)
    blk_row = lax.broadcasted_iota(jnp.int32, (nb, blk), 0)

    def scores(qb):
        rows = slice(qb * blk, (qb + 1) * blk)
        q = qr_ref[rows, :]
        n_keys = (qb + 1) * blk
        if qb > MOBA_TOPK:
            gate = gate_all[:, rows]
            rank = jnp.zeros((nb, blk), F32)
            for j in range(qb):
                gj = gate[j:j + 1, :]
                ahead = (gj > gate) | ((gj == gate) & (blk_row > j))
                rank = rank + jnp.where(ahead, 1.0, 0.0)
            drop = jnp.where((rank >= float(MOBA_TOPK)) & (blk_row < qb), 1.0, 0.0)
            drop = jnp.concatenate([drop, jnp.zeros((HEAD_DIM - nb, blk), F32)], axis=0).astype(BF16)
            drop_col = _dot_nt(eye, drop).astype(BF16)
            s = _dot_nt(jnp.concatenate([q, drop_col], axis=1), ka_ref[:n_keys, :])
        else:
            s = _dot_nt(q, ka_ref[:n_keys, :HEAD_DIM])
        own = jnp.where(causal, s[:, qb * blk:], MASKED)
        return jnp.concatenate([s[:, :qb * blk], own], axis=1) if qb else own

    s_next = scores(0)
    for qb in range(nb):
        rows = slice(qb * blk, (qb + 1) * blk)
        s = s_next
        if qb + 1 < nb:
            s_next = scores(qb + 1)
        m = jnp.max(s, axis=-1, keepdims=True)
        p = jnp.exp2(s - m).astype(BF16)
        o = _dot(p, va_ref[:(qb + 1) * blk, :])
        o = o[:, :HEAD_DIM] / o[:, HEAD_DIM:]
        o_ref[rows, :] = (o * _silu(z_ref[rows, :].astype(F32))).astype(o_ref.dtype)


def moba_mixer(h3, posf, inv_row, col_q, col_k, col_v, col_z):
    bsz, seq, _ = h3.shape

    def col(off):
        return pl.BlockSpec((None, seq, HEAD_DIM), lambda b, h: (b, 0, off + h))

    return pl.pallas_call(
        _moba_kernel,
        out_shape=jax.ShapeDtypeStruct((bsz, seq, WIDTH), BF16),
        grid=(bsz, N_HEADS),
        in_specs=[col(col_q), col(col_k), col(col_v), col(col_z),
                  pl.BlockSpec((None, seq, HEAD_DIM), lambda b, h: (b, 0, 0)),
                  pl.BlockSpec((1, HEAD_DIM), lambda b, h: (0, 0))],
        out_specs=pl.BlockSpec((None, seq, HEAD_DIM), lambda b, h: (b, 0, h)),
        scratch_shapes=[pltpu.VMEM((seq, HEAD_DIM), F32)] * 3 + [
            pltpu.VMEM((seq, HEAD_DIM), BF16), pltpu.VMEM((seq, 2 * HEAD_DIM), BF16),
            pltpu.VMEM((seq, 2 * HEAD_DIM), BF16)],
        compiler_params=_params("arbitrary", "arbitrary"),
        name="moba",
    )(h3, h3, h3, h3, posf, inv_row)


def _rglru_kernel(x_ref, z_ref, cw_ref, cb_ref, wa_ref, ba_ref, wx_ref, bx_ref, lam_ref,
                  o_ref, a_ref, u_ref, xpad_ref):
    seq, width = x_ref.shape
    sub = 8
    groups = seq // sub

    xpad_ref[:sub, :] = jnp.zeros((sub, width), F32)
    xpad_ref[sub:, :] = x_ref[...].astype(F32)
    xf = cb_ref[...]
    for d in range(CONV_WIDTH):
        xf = xf + xpad_ref[sub - d:sub - d + seq, :] * cw_ref[CONV_WIDTH - 1 - d:CONV_WIDTH - d, :]

    xb = xf.astype(BF16)

    def gate(w_ref, b_ref):
        pre = [_dot(xb[:, j * HEAD_DIM:(j + 1) * HEAD_DIM], w_ref[j]) for j in range(width // HEAD_DIM)]
        return _sigmoid(jnp.concatenate(pre, axis=1) + b_ref[...])

    r = gate(wa_ref, ba_ref)
    ig = gate(wx_ref, bx_ref)
    nl = -lam_ref[...]
    softplus = jnp.maximum(nl, 0.0) + jnp.log1p(jnp.exp(-jnp.abs(nl)))
    a = jnp.exp(r * ((-RG_C) * softplus))
    u = jnp.sqrt(1.0 - a * a) * ig * xf

    a3 = a.reshape(groups, sub, width)
    u3 = u.reshape(groups, sub, width)
    s_idx = lax.broadcasted_iota(jnp.int32, a3.shape, 1)
    for d in (1, 2, 4):
        keep = s_idx >= d
        a_prev = jnp.where(keep, pltpu.roll(a3, d, 1), 1.0)
        u_prev = jnp.where(keep, pltpu.roll(u3, d, 1), 0.0)
        u3 = u3 + a3 * u_prev
        a3 = a3 * a_prev
    a_ref[...] = a3
    u_ref[...] = u3

    def group_body(g, carry):
        hg = a_ref[g] * carry + u_ref[g]
        u_ref[g] = hg
        return jnp.broadcast_to(hg[sub - 1:sub, :], (sub, width))

    lax.fori_loop(0, groups, group_body, jnp.zeros((sub, width), F32), unroll=8)
    hs = u_ref[...].reshape(seq, width)
    o_ref[...] = (hs * _silu(z_ref[...].astype(F32))).astype(o_ref.dtype)


def rglru_mixer(h3, conv_w, conv_b, wa, ba, wx, bx, lam, col_x, col_z, blocks_per_step=2):
    bsz, seq, _ = h3.shape
    width = blocks_per_step * HEAD_DIM

    def col(off):
        return pl.BlockSpec((None, seq, width), lambda b, c: (b, 0, off // blocks_per_step + c))

    def vec(rows):
        return pl.BlockSpec((rows, width), lambda b, c: (0, c))

    wspec = pl.BlockSpec((blocks_per_step, HEAD_DIM, HEAD_DIM), lambda b, c: (c, 0, 0))
    return pl.pallas_call(
        _rglru_kernel,
        out_shape=jax.ShapeDtypeStruct((bsz, seq, WIDTH), BF16),
        grid=(bsz, WIDTH // width),
        in_specs=[col(col_x), col(col_z), vec(CONV_WIDTH), vec(1), wspec, vec(1), wspec, vec(1), vec(1)],
        out_specs=pl.BlockSpec((None, seq, width), lambda b, c: (b, 0, c)),
        scratch_shapes=[pltpu.VMEM((seq // 8, 8, width), F32)] * 2 + [pltpu.VMEM((seq + 8, width), F32)],
        compiler_params=_params("arbitrary", "arbitrary"),
        name="rglru",
    )(h3, h3, conv_w, conv_b.reshape(1, WIDTH), wa.astype(BF16), ba.reshape(1, WIDTH),
      wx.astype(BF16), bx.reshape(1, WIDTH), lam.reshape(1, WIDTH))


def _retention_kernel(q_ref, k_ref, v_ref, z_ref, pos_ref, inv_ref, logg_ref, gain_ref, bias_ref,
                      o_ref, cos_ref, sin_ref, st_ref):
    seq = q_ref.shape[0]
    ck = RET_CHUNK
    half = HEAD_DIM // 2

    nc = seq // ck // RET_PARTS

    @pl.when(pl.program_id(1) == 0)
    def _():
        ang = pos_ref[...] * inv_ref[...]
        lane = lax.broadcasted_iota(jnp.int32, ang.shape, 1)
        sin = jnp.sin(ang)
        cos_ref[...] = jnp.cos(ang)
        sin_ref[...] = jnp.where(lane < half, -sin, sin)

    def chunks(x):
        return x.reshape(nc, ck, HEAD_DIM)

    def bmm(spec, a, b):
        return jnp.einsum(spec, a, b, preferred_element_type=F32)

    log_g = logg_ref[...]
    ri = lax.broadcasted_iota(jnp.int32, (ck, ck), 0)
    ci = lax.broadcasted_iota(jnp.int32, (ck, ck), 1)
    diff = (ri - ci).astype(F32)
    dmask = jnp.where(ri >= ci, jnp.exp(jnp.maximum(diff, 0.0) * log_g[:, :1]), 0.0)
    idx = lax.broadcasted_iota(jnp.int32, (ck, HEAD_DIM), 0).astype(F32)
    q_fac = jnp.exp((idx + 1.0) * log_g)
    k_fac = jnp.exp((ck - 1.0 - idx) * log_g)
    chunk_decay = jnp.exp(float(ck) * log_g)

    carried = [jnp.zeros((HEAD_DIM, HEAD_DIM), F32)]

    def part(p):
        rows = slice(p * nc * ck, (p + 1) * nc * ck)
        cos = cos_ref[rows, :]
        sin = sin_ref[rows, :]
        q = q_ref[rows, :].astype(F32)
        k = k_ref[rows, :].astype(F32)
        q = chunks(q * cos + pltpu.roll(q, half, 1) * sin)
        k = chunks((k * cos + pltpu.roll(k, half, 1) * sin) * (HEAD_DIM ** -0.5))
        v = chunks(v_ref[rows, :])
        yield
        att = bmm('cid,cjd->cij', q.astype(BF16), k.astype(BF16)) * dmask[None]
        k_end_t = jnp.swapaxes(k * k_fac[None], 1, 2).astype(BF16)
        incr = bmm('ckn,cnv->ckv', k_end_t, v)
        yield
        o_intra = bmm('cij,cjd->cid', att.astype(BF16), v)
        state = carried[0]
        for c in range(nc):
            st_ref[p * nc + c] = state.astype(BF16)
            state = state * chunk_decay + incr[c]
        carried[0] = state
        yield
        o = o_intra + bmm('cnk,ckv->cnv', (q * q_fac[None]).astype(BF16), st_ref[p * nc:(p + 1) * nc])
        o = o.reshape(nc * ck, HEAD_DIM)
        mu = jnp.mean(o, axis=-1, keepdims=True)
        oc = o - mu
        var = jnp.mean(oc * oc, axis=-1, keepdims=True)
        y = oc * lax.rsqrt(var + EPS) * gain_ref[...] + bias_ref[...]
        o_ref[rows, :] = (y * _silu(z_ref[rows, :].astype(F32))).astype(o_ref.dtype)
        yield

    _run_staggered([part(p) for p in range(RET_PARTS)], n_stages=4)


def retention_mixer(h3, posf, inv_row, log_g, gain, bias, col_q, col_k, col_v, col_z):
    bsz, seq, _ = h3.shape

    def col(off):
        return pl.BlockSpec((None, seq, HEAD_DIM), lambda b, h: (b, 0, off + h))

    vec = pl.BlockSpec((None, 1, HEAD_DIM), lambda b, h: (h, 0, 0))
    return pl.pallas_call(
        _retention_kernel,
        out_shape=jax.ShapeDtypeStruct((bsz, seq, WIDTH), BF16),
        grid=(bsz, N_HEADS),
        in_specs=[col(col_q), col(col_k), col(col_v), col(col_z),
                  pl.BlockSpec((None, seq, HEAD_DIM), lambda b, h: (b, 0, 0)),
                  pl.BlockSpec((1, HEAD_DIM), lambda b, h: (0, 0)),
                  vec, vec, vec],
        out_specs=pl.BlockSpec((None, seq, HEAD_DIM), lambda b, h: (b, 0, h)),
        scratch_shapes=[pltpu.VMEM((seq, HEAD_DIM), F32)] * 2 + [
            pltpu.VMEM((seq // RET_CHUNK, HEAD_DIM, HEAD_DIM), BF16)],
        compiler_params=_params("arbitrary", "arbitrary"),
        name="retention",
    )(h3, h3, h3, h3, posf, inv_row, log_g,
      gain.reshape(N_HEADS, 1, HEAD_DIM), bias.reshape(N_HEADS, 1, HEAD_DIM))


def _post_kernel(x_ref, ya_ref, yb_ref, woa_ref, wob_ref, g_ref, wq_ref, kv_ref, wo_ref, fg_ref,
                 o_ref, *, final):
    x1 = x_ref[...] + _dot(ya_ref[...], woa_ref[...]) + _dot(yb_ref[...], wob_ref[...])
    ms = jnp.mean(x1 * x1, axis=-1, keepdims=True)
    xn = (x1 * lax.rsqrt(ms + EPS) * g_ref[...]).astype(BF16)
    q = _dot(xn, wq_ref[...]).astype(BF16)
    scale = HEAD_DIM ** -0.5
    heads = []
    for hd in range(XA_HEADS):
        cs = slice(hd * HEAD_DIM, (hd + 1) * HEAD_DIM)
        k = kv_ref[:, cs]
        v = kv_ref[:, XA_WIDTH + hd * HEAD_DIM:XA_WIDTH + (hd + 1) * HEAD_DIM]
        s = _dot_nt(q[:, cs], k) * scale
        p = jnp.exp(s - jnp.max(s, axis=-1, keepdims=True))
        p = p / jnp.sum(p, axis=-1, keepdims=True)
        heads.append(_dot(p.astype(BF16), v).astype(BF16))
    o = jnp.concatenate(heads, axis=1)
    x2 = x1 + _dot(o, wo_ref[...])
    if final:
        ms2 = jnp.mean(x2 * x2, axis=-1, keepdims=True)
        x2 = x2 * lax.rsqrt(ms2 + EPS) * fg_ref[...]
    o_ref[...] = x2


def post_mixer(x3, ya, yb, w_out, xa_gain, wq, kv, wo, final_gain, final, tm=1024):
    bsz, seq, d = x3.shape
    full = lambda shape: pl.BlockSpec(shape, lambda b, i: (0,) * len(shape))
    tile = lambda w: pl.BlockSpec((None, tm, w), lambda b, i: (b, i, 0))
    return pl.pallas_call(
        functools.partial(_post_kernel, final=final),
        out_shape=jax.ShapeDtypeStruct((bsz, seq, d), F32),
        grid=(bsz, seq // tm),
        in_specs=[tile(d), tile(WIDTH), tile(WIDTH),
                  full((WIDTH, d)), full((WIDTH, d)), full((1, d)), full((d, XA_WIDTH)),
                  pl.BlockSpec((None, MEM_LEN, 2 * XA_WIDTH), lambda b, i: (b, 0, 0)),
                  full((XA_WIDTH, d)), full((1, d))],
        out_specs=tile(d),
        compiler_params=_params("arbitrary", "arbitrary"),
        name="post_final" if final else "post",
    )(x3, ya, yb, w_out[:WIDTH].astype(BF16), w_out[WIDTH:].astype(BF16), xa_gain.reshape(1, d),
      wq.astype(BF16), kv, wo.astype(BF16), final_gain.reshape(1, d))


def _rope_row(rot_dim, theta):
    half = rot_dim // 2
    inv = theta ** (-jnp.arange(half, dtype=F32) / half)
    row = jnp.concatenate([inv, inv, jnp.zeros((HEAD_DIM - rot_dim,), F32)])
    return row.reshape(1, HEAD_DIM)


def kernel(x, mem, positions, hgrn_lb_logits, ev_norm, ev_w_in, ev_hgrn_gain, ev_w_out, od_norm, od_w_in, od_conv_w, od_conv_b, od_rg_wa, od_rg_ba, od_rg_wx, od_rg_bx, od_rg_lambda, od_ret_gain, od_ret_bias, od_w_out, xa_norm, xa_mem_norm, xa_wq, xa_wkv, xa_wo, final_norm):
    bsz, seq, d = x.shape
    tokens = bsz * seq
    lb_all = jnp.cumsum(jax.nn.softmax(hgrn_lb_logits.astype(F32), axis=0), axis=0)
    posf = jnp.broadcast_to(positions.astype(F32)[:, :, None], (bsz, seq, HEAD_DIM))
    log_g = jnp.log(1.0 - 2.0 ** (-5.0 - jnp.arange(N_HEADS, dtype=F32)))
    log_g = jnp.broadcast_to(log_g[:, None, None], (N_HEADS, 1, HEAD_DIM))
    mem2 = mem.reshape(bsz * MEM_LEN, d)

    def memory_kv(layer):
        kv = norm_matmul(mem2, xa_mem_norm[layer], xa_wkv[layer], BF16,
                         tm=bsz * MEM_LEN // 2, tn=2 * XA_WIDTH, name="mem_kv")
        return kv.reshape(bsz, MEM_LEN, 2 * XA_WIDTH)

    h = norm_matmul(x.reshape(tokens, d), ev_norm[0], ev_w_in[0], BF16,
                    tm=2048, tn=1024, name="in_proj_even")
    h3 = h.reshape(bsz, seq, -1)
    nb = WIDTH // HEAD_DIM
    ya = hgrn2_mixer(h3, lb_all[0], ev_hgrn_gain[0], 0, nb, 2 * nb, 6 * nb)
    yb = moba_mixer(h3, posf, _rope_row(ROPE_DIM, ROPE_THETA), 3 * nb, 4 * nb, 5 * nb, 7 * nb)
    x1 = post_mixer(x, ya, yb, ev_w_out[0], xa_norm[0], xa_wq[0], memory_kv(0), xa_wo[0],
                    final_norm, final=False)

    h = norm_matmul(x1.reshape(tokens, d), od_norm[0], od_w_in[0], BF16,
                    tm=2048, tn=1024, name="in_proj_odd")
    h3 = h.reshape(bsz, seq, -1)
    yc = rglru_mixer(h3, od_conv_w[0], od_conv_b[0], od_rg_wa[0], od_rg_ba[0], od_rg_wx[0],
                     od_rg_bx[0], od_rg_lambda[0], 0, 4 * nb)
    yd = retention_mixer(h3, posf, _rope_row(HEAD_DIM, RET_THETA), log_g, od_ret_gain[0],
                         od_ret_bias[0], nb, 2 * nb, 3 * nb, 5 * nb)
    return post_mixer(x1, yc, yd, od_w_out[0], xa_norm[1], xa_wq[1], memory_kv(1), xa_wo[1],
                      final_norm, final=True)
```

```python
import functools
import math

import jax
import jax.numpy as jnp
from jax import lax
from jax.experimental import pallas as pl
from jax.experimental.pallas import tpu as pltpu

F32 = jnp.float32
BF16 = jnp.bfloat16

D_MODEL = 1024
HEAD_DIM = 128
N_HEADS = 8
WIDTH = N_HEADS * HEAD_DIM
MEM_LEN = 256
HGRN_CHUNK = 32
HGRN_TILE = 128
HGRN_PARTS = 4
MOBA_BLOCK = 256
MOBA_TOPK = 3
RET_CHUNK = 256
RET_PARTS = 2
CONV_WIDTH = 4
RG_C = 8.0
ROPE_THETA = 500000.0
ROPE_DIM = HEAD_DIM // 4
RET_THETA = 10000.0
XA_HEADS = 4
XA_WIDTH = XA_HEADS * HEAD_DIM
EPS = 1e-6
VMEM_LIMIT = 56 * 1024 * 1024
MASKED = -1e30

_NT = (((1,), (1,)), ((), ()))
_TN = (((0,), (0,)), ((), ()))


def _dot(a, b):
    return jnp.dot(a, b, preferred_element_type=F32)


def _dot_nt(a, b):
    return lax.dot_general(a, b, _NT, preferred_element_type=F32)


def _dot_tn(a, b):
    return lax.dot_general(a, b, _TN, preferred_element_type=F32)


def _sigmoid(x):
    return 1.0 / (1.0 + jnp.exp(-x))


def _silu(x):
    return x * _sigmoid(x)


def _run_staggered(parts, n_stages):
    for step in range(n_stages + len(parts) - 1):
        for p, gen in enumerate(parts):
            if 0 <= step - p < n_stages:
                next(gen)


def _params(*sem):
    return pltpu.CompilerParams(dimension_semantics=sem, vmem_limit_bytes=VMEM_LIMIT)


def _norm_matmul_kernel(x_ref, g_ref, w_ref, o_ref, xn_ref):
    @pl.when(pl.program_id(1) == 0)
    def _():
        x = x_ref[...]
        ms = jnp.mean(x * x, axis=-1, keepdims=True)
        xn_ref[...] = (x * lax.rsqrt(ms + EPS) * g_ref[...]).astype(BF16)

    o_ref[...] = _dot(xn_ref[...], w_ref[...].astype(BF16)).astype(o_ref.dtype)


def norm_matmul(x, gain, w, out_dtype, tm, tn, name):
    t, d = x.shape
    n = w.shape[1]
    return pl.pallas_call(
        _norm_matmul_kernel,
        out_shape=jax.ShapeDtypeStruct((t, n), out_dtype),
        grid=(t // tm, n // tn),
        in_specs=[
            pl.BlockSpec((tm, d), lambda i, j: (i, 0)),
            pl.BlockSpec((1, d), lambda i, j: (0, 0)),
            pl.BlockSpec((d, tn), lambda i, j: (0, j)),
        ],
        out_specs=pl.BlockSpec((tm, tn), lambda i, j: (i, j)),
        scratch_shapes=[pltpu.VMEM((tm, d), BF16)],
        compiler_params=_params("arbitrary", "arbitrary"),
        name=name,
    )(x, gain.reshape(1, d), w)


def _split2(x):
    hi = x.astype(BF16)
    lo = (x - hi.astype(F32)).astype(BF16)
    return hi, lo


def _hgrn2_kernel(q_ref, f_ref, i_ref, z_ref, lb_ref, gain_ref, o_ref, st_ref):
    seq = q_ref.shape[0]
    tile, chunk = HGRN_TILE, HGRN_CHUNK
    per_tile = tile // chunk
    part_rows = seq // HGRN_PARTS
    nt = part_rows // tile
    lb = lb_ref[...]
    gain = gain_ref[...]

    def tiles(x):
        return x.reshape(nt, tile, HEAD_DIM)

    def bmm(spec, a, b):
        return jnp.einsum(spec, a, b, preferred_element_type=F32)

    ti = lax.broadcasted_iota(jnp.int32, (tile, tile), 0)
    tj = lax.broadcasted_iota(jnp.int32, (tile, tile), 1)
    intra_mask = (((ti // chunk) == (tj // chunk)) & (tj <= ti))[None]
    sum_mat = jnp.broadcast_to(jnp.where(intra_mask, 1.0, 0.0).astype(BF16), (nt, tile, tile))
    row_chunk = (lax.broadcasted_iota(jnp.int32, (tile, HEAD_DIM), 0) // chunk)[None]

    def per_chunk_lanes(x):
        zero = jnp.zeros_like(x)
        return jnp.concatenate([jnp.where(row_chunk == ci, x, zero) for ci in range(per_tile)], axis=-1)

    carried = [jnp.zeros((HEAD_DIM, HEAD_DIM), F32)]

    def part(p):
        rows = slice(p * part_rows, (p + 1) * part_rows)
        f = lb + (1.0 - lb) * _sigmoid(f_ref[rows, :].astype(F32))
        k = tiles(1.0 - f)
        hi, lo = _split2(jnp.log(f))
        yield
        b = bmm('trc,tcd->trd', sum_mat, tiles(hi)) + bmm('trc,tcd->trd', sum_mat, tiles(lo))
        b_chunks = b.reshape(part_rows // chunk, chunk, HEAD_DIM)
        b_tot = jnp.broadcast_to(b_chunks[:, chunk - 1:, :], b_chunks.shape).reshape(b.shape)
        yield
        q_dec = (tiles(q_ref[rows, :].astype(F32)) * jnp.exp(b)).astype(BF16)
        k_inv = (k * jnp.exp(-b)).astype(BF16)
        k_end = (k * jnp.exp(b_tot - b)).astype(BF16)
        v = tiles(i_ref[rows, :].astype(F32))
        yield
        att = bmm('tid,tjd->tij', q_dec, k_inv)
        att = jnp.where(intra_mask, att, 0.0).astype(BF16)
        v_t = jnp.swapaxes(v, 1, 2).astype(BF16)
        incr = bmm('tvn,tnk->tvk', v_t, per_chunk_lanes(k_end))
        yield
        o_intra = bmm('tij,tjd->tid', att, v.astype(BF16))
        state_t = carried[0]
        for t in range(nt):
            for ci in range(per_tile):
                lanes = slice(ci * HEAD_DIM, (ci + 1) * HEAD_DIM)
                st_ref[p * nt + t, :, lanes] = state_t.astype(BF16)
                decay = jnp.exp(b_tot[t, ci * chunk:ci * chunk + 1, :])
                state_t = state_t * decay + incr[t, :, lanes]
        carried[0] = state_t
        yield
        o = o_intra + bmm('tnk,tvk->tnv', per_chunk_lanes(q_dec), st_ref[p * nt:(p + 1) * nt])
        o = o.reshape(part_rows, HEAD_DIM)
        y = o * lax.rsqrt(jnp.mean(o * o, axis=-1, keepdims=True) + EPS) * gain
        o_ref[rows, :] = (y * _silu(z_ref[rows, :].astype(F32))).astype(o_ref.dtype)
        yield

    _run_staggered([part(p) for p in range(HGRN_PARTS)], n_stages=6)


def hgrn2_mixer(h3, lb, gain, col_q, col_f, col_i, col_z):
    bsz, seq, _ = h3.shape

    def col(off):
        return pl.BlockSpec((None, seq, HEAD_DIM), lambda b, h: (b, 0, off + h))

    vec = pl.BlockSpec((None, 1, HEAD_DIM), lambda b, h: (h, 0, 0))
    return pl.pallas_call(
        _hgrn2_kernel,
        out_shape=jax.ShapeDtypeStruct((bsz, seq, WIDTH), BF16),
        grid=(bsz, N_HEADS),
        in_specs=[col(col_q), col(col_f), col(col_i), col(col_z), vec, vec],
        out_specs=pl.BlockSpec((None, seq, HEAD_DIM), lambda b, h: (b, 0, h)),
        scratch_shapes=[pltpu.VMEM((seq // HGRN_TILE, HEAD_DIM, HGRN_TILE // HGRN_CHUNK * HEAD_DIM), BF16)],
        compiler_params=_params("arbitrary", "arbitrary"),
        name="hgrn2",
    )(h3, h3, h3, h3, lb.reshape(N_HEADS, 1, HEAD_DIM), gain.reshape(N_HEADS, 1, HEAD_DIM))


RET_HALF = HEAD_DIM // 2
MOBA_HALF = ROPE_DIM // 2


def _rope_table_kernel(pos_ref, inv_ref, ret_cos_ref, ret_sin_ref, moba_cos_ref, moba_lo_ref, moba_hi_ref):
    ang = pos_ref[...] * inv_ref[...]
    cos = jnp.cos(ang)
    sin = jnp.sin(ang)
    lane = lax.broadcasted_iota(jnp.int32, ang.shape, 1)
    low = lane < RET_HALF
    cos_up = pltpu.roll(cos, RET_HALF, 1)
    sin_up = pltpu.roll(sin, RET_HALF, 1)
    ret_cos_ref[...] = jnp.where(low, cos, cos_up)
    ret_sin_ref[...] = jnp.where(low, -sin, sin_up)
    first = lane < MOBA_HALF
    second = (lane >= MOBA_HALF) & (lane < 2 * MOBA_HALF)
    moba_cos_ref[...] = jnp.where(first, cos_up, jnp.where(second, pltpu.roll(cos_up, MOBA_HALF, 1), 1.0))
    moba_lo_ref[...] = jnp.where(first, -sin_up, 0.0)
    moba_hi_ref[...] = jnp.where(second, pltpu.roll(sin_up, MOBA_HALF, 1), 0.0)


def rope_tables(positions):
    bsz, seq = positions.shape
    inv_ret = RET_THETA ** (-jnp.arange(RET_HALF, dtype=F32) / RET_HALF)
    inv_moba = ROPE_THETA ** (-jnp.arange(MOBA_HALF, dtype=F32) / MOBA_HALF)
    inv = jnp.concatenate([inv_ret, inv_moba, jnp.zeros((HEAD_DIM - RET_HALF - MOBA_HALF,), F32)])
    posf = jnp.broadcast_to(positions.astype(F32)[:, :, None], (bsz, seq, HEAD_DIM))
    spec = pl.BlockSpec((None, seq, HEAD_DIM), lambda b: (b, 0, 0))
    return pl.pallas_call(
        _rope_table_kernel,
        out_shape=[jax.ShapeDtypeStruct((bsz, seq, HEAD_DIM), F32)] * 5,
        grid=(bsz,),
        in_specs=[spec, pl.BlockSpec((1, HEAD_DIM), lambda b: (0, 0))],
        out_specs=[spec] * 5,
        compiler_params=_params("arbitrary"),
        name="rope_tables",
    )(posf, inv.reshape(1, HEAD_DIM))


def _moba_kernel(q_ref, k_ref, v_ref, z_ref, cos_ref, sin_lo_ref, sin_hi_ref, o_ref,
                 qr_ref, ka_ref, va_ref):
    seq = q_ref.shape[0]
    blk = MOBA_BLOCK
    nb = seq // blk
    exp2_scale = HEAD_DIM ** -0.5 * math.log2(math.e)

    def rope(x):
        return (x * cos_ref[...]
                + pltpu.roll(x, HEAD_DIM - MOBA_HALF, 1) * sin_lo_ref[...]
                + pltpu.roll(x, MOBA_HALF, 1) * sin_hi_ref[...])

    kr = rope(k_ref[...].astype(F32))
    k_mean = jnp.mean(kr.reshape(nb, blk, HEAD_DIM), axis=1).astype(BF16)
    ka_ref[:, :HEAD_DIM] = kr.astype(BF16)
    key_blk = lax.broadcasted_iota(jnp.int32, (seq, HEAD_DIM), 0) // blk
    key_lane = lax.broadcasted_iota(jnp.int32, (seq, HEAD_DIM), 1)
    ka_ref[:, HEAD_DIM:] = jnp.where(key_blk == key_lane, MASKED, 0.0).astype(BF16)
    qr_ref[...] = (rope(q_ref[...].astype(F32)) * exp2_scale).astype(BF16)
    va_ref[:, :HEAD_DIM] = v_ref[...]
    va_ref[:, HEAD_DIM:] = jnp.ones((seq, HEAD_DIM), BF16)
    gate_all = _dot_nt(k_mean, qr_ref[...])

    row = lax.broadcasted_iota(jnp.int32, (blk, blk), 0)
    colm = lax.broadcasted_iota(jnp.int32, (blk, blk), 1)
    causal = colm <= row
    eye = jnp.where(colm == row, 1.0, 0.0).astype(BF16)
    blk_row = lax.broadcasted_iota(jnp.int32, (nb, blk), 0)

    def scores(qb):
        rows = slice(qb * blk, (qb + 1) * blk)
        q = qr_ref[rows, :]
        n_keys = (qb + 1) * blk
        if qb > MOBA_TOPK:
            gate = gate_all[:, rows]
            rank = jnp.zeros((nb, blk), F32)
            for j in range(qb):
                gj = gate[j:j + 1, :]
                ahead = (gj > gate) | ((gj == gate) & (blk_row > j))
                rank = rank + jnp.where(ahead, 1.0, 0.0)
            drop = jnp.where((rank >= float(MOBA_TOPK)) & (blk_row < qb), 1.0, 0.0)
            drop = jnp.concatenate([drop, jnp.zeros((HEAD_DIM - nb, blk), F32)], axis=0).astype(BF16)
            drop_col = _dot_nt(eye, drop).astype(BF16)
            s = _dot_nt(jnp.concatenate([q, drop_col], axis=1), ka_ref[:n_keys, :])
        else:
            s = _dot_nt(q, ka_ref[:n_keys, :HEAD_DIM])
        own = jnp.where(causal, s[:, qb * blk:], MASKED)
        return jnp.concatenate([s[:, :qb * blk], own], axis=1) if qb else own

    order = [b for pair in zip(range(nb // 2), range(nb - 1, nb // 2 - 1, -1)) for b in pair]
    s_next = scores(order[0])
    for pos, qb in enumerate(order):
        rows = slice(qb * blk, (qb + 1) * blk)
        s = s_next
        if pos + 1 < nb:
            s_next = scores(order[pos + 1])
        m = jnp.max(s, axis=-1, keepdims=True)
        p = jnp.exp2(s - m).astype(BF16)
        o = _dot(p, va_ref[:(qb + 1) * blk, :])
        o = o[:, :HEAD_DIM] / o[:, HEAD_DIM:]
        o_ref[rows, :] = (o * _silu(z_ref[rows, :].astype(F32))).astype(o_ref.dtype)


def moba_mixer(h3, cos, sin_lo, sin_hi, col_q, col_k, col_v, col_z):
    bsz, seq, _ = h3.shape

    def col(off):
        return pl.BlockSpec((None, seq, HEAD_DIM), lambda b, h: (b, 0, off + h))

    table = pl.BlockSpec((None, seq, HEAD_DIM), lambda b, h: (b, 0, 0))
    return pl.pallas_call(
        _moba_kernel,
        out_shape=jax.ShapeDtypeStruct((bsz, seq, WIDTH), BF16),
        grid=(bsz, N_HEADS),
        in_specs=[col(col_q), col(col_k), col(col_v), col(col_z), table, table, table],
        out_specs=pl.BlockSpec((None, seq, HEAD_DIM), lambda b, h: (b, 0, h)),
        scratch_shapes=[pltpu.VMEM((seq, HEAD_DIM), BF16), pltpu.VMEM((seq, 2 * HEAD_DIM), BF16),
                        pltpu.VMEM((seq, 2 * HEAD_DIM), BF16)],
        compiler_params=_params("arbitrary", "arbitrary"),
        name="moba",
    )(h3, h3, h3, h3, cos, sin_lo, sin_hi)


def _rglru_kernel(x_ref, z_ref, cw_ref, cb_ref, wa_ref, ba_ref, wx_ref, bx_ref, lam_ref,
                  o_ref, a_ref, u_ref, xpad_ref):
    seq, width = x_ref.shape
    sub = 8
    groups = seq // sub

    xpad_ref[:sub, :] = jnp.zeros((sub, width), F32)
    xpad_ref[sub:, :] = x_ref[...].astype(F32)
    xf = cb_ref[...]
    for d in range(CONV_WIDTH):
        xf = xf + xpad_ref[sub - d:sub - d + seq, :] * cw_ref[CONV_WIDTH - 1 - d:CONV_WIDTH - d, :]

    xb = xf.astype(BF16)

    def gate(w_ref, b_ref):
        pre = [_dot(xb[:, j * HEAD_DIM:(j + 1) * HEAD_DIM], w_ref[j]) for j in range(width // HEAD_DIM)]
        return _sigmoid(jnp.concatenate(pre, axis=1) + b_ref[...])

    r = gate(wa_ref, ba_ref)
    ig = gate(wx_ref, bx_ref)
    nl = -lam_ref[...]
    softplus = jnp.maximum(nl, 0.0) + jnp.log1p(jnp.exp(-jnp.abs(nl)))
    a = jnp.exp(r * ((-RG_C) * softplus))
    u = jnp.sqrt(1.0 - a * a) * ig * xf

    a3 = a.reshape(groups, sub, width)
    u3 = u.reshape(groups, sub, width)
    s_idx = lax.broadcasted_iota(jnp.int32, a3.shape, 1)
    for d in (1, 2, 4):
        keep = s_idx >= d
        a_prev = jnp.where(keep, pltpu.roll(a3, d, 1), 1.0)
        u_prev = jnp.where(keep, pltpu.roll(u3, d, 1), 0.0)
        u3 = u3 + a3 * u_prev
        a3 = a3 * a_prev
    a_ref[...] = a3
    u_ref[...] = u3

    def group_body(g, carry):
        hg = a_ref[g] * carry + u_ref[g]
        u_ref[g] = hg
        return jnp.broadcast_to(hg[sub - 1:sub, :], (sub, width))

    lax.fori_loop(0, groups, group_body, jnp.zeros((sub, width), F32), unroll=8)
    hs = u_ref[...].reshape(seq, width)
    o_ref[...] = (hs * _silu(z_ref[...].astype(F32))).astype(o_ref.dtype)


def rglru_mixer(h3, conv_w, conv_b, wa, ba, wx, bx, lam, col_x, col_z, blocks_per_step=2):
    bsz, seq, _ = h3.shape
    width = blocks_per_step * HEAD_DIM

    def col(off):
        return pl.BlockSpec((None, seq, width), lambda b, c: (b, 0, off // blocks_per_step + c))

    def vec(rows):
        return pl.BlockSpec((rows, width), lambda b, c: (0, c))

    wspec = pl.BlockSpec((blocks_per_step, HEAD_DIM, HEAD_DIM), lambda b, c: (c, 0, 0))
    return pl.pallas_call(
        _rglru_kernel,
        out_shape=jax.ShapeDtypeStruct((bsz, seq, WIDTH), BF16),
        grid=(bsz, WIDTH // width),
        in_specs=[col(col_x), col(col_z), vec(CONV_WIDTH), vec(1), wspec, vec(1), wspec, vec(1), vec(1)],
        out_specs=pl.BlockSpec((None, seq, width), lambda b, c: (b, 0, c)),
        scratch_shapes=[pltpu.VMEM((seq // 8, 8, width), F32)] * 2 + [pltpu.VMEM((seq + 8, width), F32)],
        compiler_params=_params("arbitrary", "arbitrary"),
        name="rglru",
    )(h3, h3, conv_w, conv_b.reshape(1, WIDTH), wa.astype(BF16), ba.reshape(1, WIDTH),
      wx.astype(BF16), bx.reshape(1, WIDTH), lam.reshape(1, WIDTH))


def _retention_kernel(q_ref, k_ref, v_ref, z_ref, cos_ref, sin_ref, logg_ref, gain_ref, bias_ref,
                      o_ref, st_ref):
    seq = q_ref.shape[0]
    ck = RET_CHUNK
    half = RET_HALF
    nc = seq // ck // RET_PARTS

    def chunks(x):
        return x.reshape(nc, ck, HEAD_DIM)

    def bmm(spec, a, b):
        return jnp.einsum(spec, a, b, preferred_element_type=F32)

    log_g = logg_ref[...]
    ri = lax.broadcasted_iota(jnp.int32, (ck, ck), 0)
    ci = lax.broadcasted_iota(jnp.int32, (ck, ck), 1)
    diff = (ri - ci).astype(F32)
    dmask = jnp.where(ri >= ci, jnp.exp(jnp.maximum(diff, 0.0) * log_g[:, :1]), 0.0)
    idx = lax.broadcasted_iota(jnp.int32, (ck, HEAD_DIM), 0).astype(F32)
    q_fac = jnp.exp((idx + 1.0) * log_g)
    k_fac = jnp.exp((ck - 1.0 - idx) * log_g)
    chunk_decay = jnp.exp(float(ck) * log_g)

    carried = [jnp.zeros((HEAD_DIM, HEAD_DIM), F32)]

    def part(p):
        rows = slice(p * nc * ck, (p + 1) * nc * ck)
        cos = cos_ref[rows, :]
        sin = sin_ref[rows, :]
        q = q_ref[rows, :].astype(F32)
        k = k_ref[rows, :].astype(F32)
        q = chunks(q * cos + pltpu.roll(q, half, 1) * sin)
        k = chunks((k * cos + pltpu.roll(k, half, 1) * sin) * (HEAD_DIM ** -0.5))
        v = chunks(v_ref[rows, :])
        yield
        att = bmm('cid,cjd->cij', q.astype(BF16), k.astype(BF16)) * dmask[None]
        k_end_t = jnp.swapaxes(k * k_fac[None], 1, 2).astype(BF16)
        incr = bmm('ckn,cnv->ckv', k_end_t, v)
        yield
        o_intra = bmm('cij,cjd->cid', att.astype(BF16), v)
        state = carried[0]
        for c in range(nc):
            st_ref[p * nc + c] = state.astype(BF16)
            state = state * chunk_decay + incr[c]
        carried[0] = state
        yield
        o = o_intra + bmm('cnk,ckv->cnv', (q * q_fac[None]).astype(BF16), st_ref[p * nc:(p + 1) * nc])
        o = o.reshape(nc * ck, HEAD_DIM)
        mu = jnp.mean(o, axis=-1, keepdims=True)
        oc = o - mu
        var = jnp.mean(oc * oc, axis=-1, keepdims=True)
        y = oc * lax.rsqrt(var + EPS) * gain_ref[...] + bias_ref[...]
        o_ref[rows, :] = (y * _silu(z_ref[rows, :].astype(F32))).astype(o_ref.dtype)
        yield

    _run_staggered([part(p) for p in range(RET_PARTS)], n_stages=4)


def retention_mixer(h3, cos, sin, log_g, gain, bias, col_q, col_k, col_v, col_z):
    bsz, seq, _ = h3.shape

    def col(off):
        return pl.BlockSpec((None, seq, HEAD_DIM), lambda b, h: (b, 0, off + h))

    table = pl.BlockSpec((None, seq, HEAD_DIM), lambda b, h: (b, 0, 0))
    vec = pl.BlockSpec((None, 1, HEAD_DIM), lambda b, h: (h, 0, 0))
    return pl.pallas_call(
        _retention_kernel,
        out_shape=jax.ShapeDtypeStruct((bsz, seq, WIDTH), BF16),
        grid=(bsz, N_HEADS),
        in_specs=[col(col_q), col(col_k), col(col_v), col(col_z), table, table, vec, vec, vec],
        out_specs=pl.BlockSpec((None, seq, HEAD_DIM), lambda b, h: (b, 0, h)),
        scratch_shapes=[pltpu.VMEM((seq // RET_CHUNK, HEAD_DIM, HEAD_DIM), BF16)],
        compiler_params=_params("arbitrary", "arbitrary"),
        name="retention",
    )(h3, h3, h3, h3, cos, sin, log_g,
      gain.reshape(N_HEADS, 1, HEAD_DIM), bias.reshape(N_HEADS, 1, HEAD_DIM))


def _post_kernel(x_ref, ya_ref, yb_ref, woa_ref, wob_ref, g_ref, wq_ref, kv_ref, wo_ref, fg_ref,
                 o_ref, *, final):
    x1 = x_ref[...] + _dot(ya_ref[...], woa_ref[...]) + _dot(yb_ref[...], wob_ref[...])
    ms = jnp.mean(x1 * x1, axis=-1, keepdims=True)
    xn = (x1 * lax.rsqrt(ms + EPS) * g_ref[...]).astype(BF16)
    q = _dot(xn, wq_ref[...]).astype(BF16)
    scale = HEAD_DIM ** -0.5
    heads = []
    for hd in range(XA_HEADS):
        cs = slice(hd * HEAD_DIM, (hd + 1) * HEAD_DIM)
        k = kv_ref[:, cs]
        v = kv_ref[:, XA_WIDTH + hd * HEAD_DIM:XA_WIDTH + (hd + 1) * HEAD_DIM]
        s = _dot_nt(q[:, cs], k) * scale
        p = jnp.exp(s - jnp.max(s, axis=-1, keepdims=True))
        p = p / jnp.sum(p, axis=-1, keepdims=True)
        heads.append(_dot(p.astype(BF16), v).astype(BF16))
    o = jnp.concatenate(heads, axis=1)
    x2 = x1 + _dot(o, wo_ref[...])
    if final:
        ms2 = jnp.mean(x2 * x2, axis=-1, keepdims=True)
        x2 = x2 * lax.rsqrt(ms2 + EPS) * fg_ref[...]
    o_ref[...] = x2


def post_mixer(x3, ya, yb, w_out, xa_gain, wq, kv, wo, final_gain, final, tm=1024):
    bsz, seq, d = x3.shape
    full = lambda shape: pl.BlockSpec(shape, lambda b, i: (0,) * len(shape))
    tile = lambda w: pl.BlockSpec((None, tm, w), lambda b, i: (b, i, 0))
    return pl.pallas_call(
        functools.partial(_post_kernel, final=final),
        out_shape=jax.ShapeDtypeStruct((bsz, seq, d), F32),
        grid=(bsz, seq // tm),
        in_specs=[tile(d), tile(WIDTH), tile(WIDTH),
                  full((WIDTH, d)), full((WIDTH, d)), full((1, d)), full((d, XA_WIDTH)),
                  pl.BlockSpec((None, MEM_LEN, 2 * XA_WIDTH), lambda b, i: (b, 0, 0)),
                  full((XA_WIDTH, d)), full((1, d))],
        out_specs=tile(d),
        compiler_params=_params("arbitrary", "arbitrary"),
        name="post_final" if final else "post",
    )(x3, ya, yb, w_out[:WIDTH].astype(BF16), w_out[WIDTH:].astype(BF16), xa_gain.reshape(1, d),
      wq.astype(BF16), kv, wo.astype(BF16), final_gain.reshape(1, d))


def kernel(x, mem, positions, hgrn_lb_logits, ev_norm, ev_w_in, ev_hgrn_gain, ev_w_out, od_norm, od_w_in, od_conv_w, od_conv_b, od_rg_wa, od_rg_ba, od_rg_wx, od_rg_bx, od_rg_lambda, od_ret_gain, od_ret_bias, od_w_out, xa_norm, xa_mem_norm, xa_wq, xa_wkv, xa_wo, final_norm):
    bsz, seq, d = x.shape
    tokens = bsz * seq
    lb_all = jnp.cumsum(jax.nn.softmax(hgrn_lb_logits.astype(F32), axis=0), axis=0)
    ret_cos, ret_sin, moba_cos, moba_lo, moba_hi = rope_tables(positions)
    log_g =jnp.log(1.0 - 2.0 ** (-5.0 - jnp.arange(N_HEADS, dtype=F32)))
    log_g = jnp.broadcast_to(log_g[:, None, None], (N_HEADS, 1, HEAD_DIM))
    mem2 = mem.reshape(bsz * MEM_LEN, d)

    def memory_kv(layer):
        kv = norm_matmul(mem2, xa_mem_norm[layer], xa_wkv[layer], BF16,
                         tm=bsz * MEM_LEN // 2, tn=2 * XA_WIDTH, name="mem_kv")
        return kv.reshape(bsz, MEM_LEN, 2 * XA_WIDTH)

    h = norm_matmul(x.reshape(tokens, d), ev_norm[0], ev_w_in[0], BF16,
                    tm=2048, tn=1024, name="in_proj_even")
    h3 = h.reshape(bsz, seq, -1)
    nb = WIDTH // HEAD_DIM
    ya = hgrn2_mixer(h3, lb_all[0], ev_hgrn_gain[0], 0, nb, 2 * nb, 6 * nb)
    yb = moba_mixer(h3, moba_cos, moba_lo, moba_hi, 3 * nb, 4 * nb, 5 * nb, 7 * nb)
    x1 = post_mixer(x, ya, yb, ev_w_out[0], xa_norm[0], xa_wq[0], memory_kv(0), xa_wo[0],
                    final_norm, final=False)

    h = norm_matmul(x1.reshape(tokens, d), od_norm[0], od_w_in[0], BF16,
                    tm=2048, tn=1024, name="in_proj_odd")
    h3 = h.reshape(bsz, seq, -1)
    yc = rglru_mixer(h3, od_conv_w[0], od_conv_b[0], od_rg_wa[0], od_rg_ba[0], od_rg_wx[0],
                     od_rg_bx[0], od_rg_lambda[0], 0, 4 * nb)
    yd = retention_mixer(h3, ret_cos, ret_sin, log_g, od_ret_gain[0], od_ret_bias[0],
                         nb, 2 * nb, 3 * nb, 5 * nb)
    return post_mixer(x1, yc, yd, od_w_out[0], xa_norm[1], xa_wq[1], memory_kv(1), xa_wo[1],
                      final_norm, final=True)
```

```python
import functools
import math

import jax
import jax.numpy as jnp
from jax import lax
from jax.experimental import pallas as pl
from jax.experimental.pallas import tpu as pltpu

F32 = jnp.float32
BF16 = jnp.bfloat16

D_MODEL = 1024
HEAD_DIM = 128
N_HEADS = 8
WIDTH = N_HEADS * HEAD_DIM
MEM_LEN = 256
HGRN_CHUNK = 32
HGRN_TILE = 128
HGRN_PARTS = 16
MOBA_BLOCK = 256
MOBA_TOPK = 3
RET_CHUNK = 256
SIDE_SLICES = 4
RET_PARTS = 4
CONV_WIDTH = 4
RG_C = 8.0
RG_LANES = 2 * HEAD_DIM
RG_PARTS = 4
ROPE_THETA = 500000.0
ROPE_DIM = HEAD_DIM // 4
RET_THETA = 10000.0
XA_HEADS = 4
XA_WIDTH = XA_HEADS * HEAD_DIM
EPS = 1e-6
VMEM_LIMIT = 56 * 1024 * 1024
MASKED = -1e30

_NT = (((1,), (1,)), ((), ()))
_TN = (((0,), (0,)), ((), ()))


def _dot(a, b):
    return jnp.dot(a, b, preferred_element_type=F32)


def _dot_nt(a, b):
    return lax.dot_general(a, b, _NT, preferred_element_type=F32)


def _dot_tn(a, b):
    return lax.dot_general(a, b, _TN, preferred_element_type=F32)


def _sigmoid(x):
    return 1.0 / (1.0 + jnp.exp(-x))


def _silu(x):
    return x * _sigmoid(x)


def _run_staggered(parts, n_stages, between=None):
    n_steps = n_stages + len(parts) - 1
    for step in range(n_steps):
        if between is not None:
            between(step, n_steps)
        for p, gen in enumerate(parts):
            if 0 <= step - p < n_stages:
                next(gen)


def _spread(side):
    def between(step, n_steps):
        for j in range(SIDE_SLICES):
            if j * n_steps // SIDE_SLICES == step:
                side(j, SIDE_SLICES)
    return between


def _params(*sem):
    return pltpu.CompilerParams(dimension_semantics=sem, vmem_limit_bytes=VMEM_LIMIT)


def _norm_matmul_kernel(x_ref, g_ref, w_ref, o_ref, xn_ref):
    @pl.when(pl.program_id(1) == 0)
    def _():
        x = x_ref[...]
        ms = jnp.mean(x * x, axis=-1, keepdims=True)
        xn_ref[...] = (x * lax.rsqrt(ms + EPS) * g_ref[...]).astype(BF16)

    o_ref[...] = _dot(xn_ref[...], w_ref[...].astype(BF16)).astype(o_ref.dtype)


def norm_matmul(x, gain, w, out_dtype, tm, tn, name):
    t, d = x.shape
    n = w.shape[1]
    return pl.pallas_call(
        _norm_matmul_kernel,
        out_shape=jax.ShapeDtypeStruct((t, n), out_dtype),
        grid=(t // tm, n // tn),
        in_specs=[
            pl.BlockSpec((tm, d), lambda i, j: (i, 0)),
            pl.BlockSpec((1, d), lambda i, j: (0, 0)),
            pl.BlockSpec((d, tn), lambda i, j: (0, j)),
        ],
        out_specs=pl.BlockSpec((tm, tn), lambda i, j: (i, j)),
        scratch_shapes=[pltpu.VMEM((tm, d), BF16)],
        compiler_params=_params("arbitrary", "arbitrary"),
        name=name,
    )(x, gain.reshape(1, d), w)


def _fused_kernel(xn_ref, w_next_ref, w_first_ref, *refs, body, n_extra):
    extra = refs[:n_extra]
    o_ref, p_even, p_odd = refs[n_extra:n_extra + 3]
    scratch = refs[n_extra + 3:]
    seq = xn_ref.shape[0]
    step = pl.program_id(0)

    @pl.when(step == 0)
    def _():
        p_even[...] = _dot(xn_ref[...], w_first_ref[...]).astype(p_even.dtype)

    def run(p_cur, p_next):
        def side(j, n):
            rows = slice(j * seq // n, (j + 1) * seq // n)
            p_next[rows, :] = _dot(xn_ref[rows, :], w_next_ref[...]).astype(p_next.dtype)

        body(p_cur, *extra, o_ref, *scratch, side=side)

    @pl.when(step % 2 == 0)
    def _():
        run(p_even, p_odd)

    @pl.when(step % 2 == 1)
    def _():
        run(p_odd, p_even)


def fused_mixer(body, xn, w_groups, extra, extra_specs, scratch, out_width, name):
    bsz, seq, d = xn.shape
    n_groups, _, cols = w_groups.shape
    steps = bsz * n_groups

    def nxt(i):
        return jnp.minimum(i + 1, steps - 1)

    return pl.pallas_call(
        functools.partial(_fused_kernel, body=body, n_extra=len(extra)),
        out_shape=jax.ShapeDtypeStruct((bsz, seq, n_groups * out_width), BF16),
        grid=(steps,),
        in_specs=[
            pl.BlockSpec((None, seq, d), lambda i: (nxt(i) // n_groups, 0, 0)),
            pl.BlockSpec((None, d, cols), lambda i: (nxt(i) % n_groups, 0, 0)),
            pl.BlockSpec((None, d, cols), lambda i: (0, 0, 0)),
        ] + list(extra_specs),
        out_specs=pl.BlockSpec((None, seq, out_width), lambda i: (i // n_groups, 0, i % n_groups)),
        scratch_shapes=[pltpu.VMEM((seq, cols), BF16)] * 2 + list(scratch),
        compiler_params=_params("arbitrary"),
        name=name,
    )(xn, w_groups, w_groups, *extra)


def group_columns(w, starts, width, n_groups):
    d_in = w.shape[0]
    parts = [w[:, s:s + n_groups * width].reshape(d_in, n_groups, 1, width) for s in starts]
    stacked = jnp.concatenate(parts, axis=2)
    return stacked.transpose(1, 0, 2, 3).reshape(n_groups, d_in, len(starts) * width).astype(BF16)


def _split2(x):
    hi = x.astype(BF16)
    lo = (x - hi.astype(F32)).astype(BF16)
    return hi, lo


def _hgrn2_kernel(q_ref, f_ref, i_ref, z_ref, lb_ref, gain_ref, o_ref, st_ref):
    seq = q_ref.shape[0]
    tile, chunk = HGRN_TILE, HGRN_CHUNK
    per_tile = tile // chunk
    part_rows = seq // HGRN_PARTS
    nt = part_rows // tile
    lb = lb_ref[...]
    gain = gain_ref[...]

    def tiles(x):
        return x.reshape(nt, tile, HEAD_DIM)

    def bmm(spec, a, b):
        return jnp.einsum(spec, a, b, preferred_element_type=F32)

    ti = lax.broadcasted_iota(jnp.int32, (tile, tile), 0)
    tj = lax.broadcasted_iota(jnp.int32, (tile, tile), 1)
    intra_mask = (((ti // chunk) == (tj // chunk)) & (tj <= ti))[None]
    sum_mat = jnp.broadcast_to(jnp.where(intra_mask, 1.0, 0.0).astype(BF16), (nt, tile, tile))
    row_chunk = (lax.broadcasted_iota(jnp.int32, (tile, HEAD_DIM), 0) // chunk)[None]

    def per_chunk_lanes(x):
        zero = jnp.zeros_like(x)
        return jnp.concatenate([jnp.where(row_chunk == ci, x, zero) for ci in range(per_tile)], axis=-1)

    carried = [jnp.zeros((HEAD_DIM, HEAD_DIM), F32)]

    def part(p):
        rows = slice(p * part_rows, (p + 1) * part_rows)
        f = lb + (1.0 - lb) * _sigmoid(f_ref[rows, :].astype(F32))
        k = tiles(1.0 - f)
        hi, lo = _split2(jnp.log(f))
        yield
        b = bmm('trc,tcd->trd', sum_mat, tiles(hi)) + bmm('trc,tcd->trd', sum_mat, tiles(lo))
        b_chunks = b.reshape(part_rows // chunk, chunk, HEAD_DIM)
        b_tot = jnp.broadcast_to(b_chunks[:, chunk - 1:, :], b_chunks.shape).reshape(b.shape)
        yield
        q_dec = (tiles(q_ref[rows, :].astype(F32)) * jnp.exp(b)).astype(BF16)
        k_inv = (k * jnp.exp(-b)).astype(BF16)
        k_end = (k * jnp.exp(b_tot - b)).astype(BF16)
        v = tiles(i_ref[rows, :].astype(F32))
        yield
        att = bmm('tid,tjd->tij', q_dec, k_inv)
        att = jnp.where(intra_mask, att, 0.0).astype(BF16)
        v_t = jnp.swapaxes(v, 1, 2).astype(BF16)
        incr = bmm('tvn,tnk->tvk', v_t, per_chunk_lanes(k_end))
        yield
        o_intra = bmm('tij,tjd->tid', att, v.astype(BF16))
        state_t = carried[0]
        for t in range(nt):
            for ci in range(per_tile):
                lanes = slice(ci * HEAD_DIM, (ci + 1) * HEAD_DIM)
                st_ref[p * nt + t, :, lanes] = state_t.astype(BF16)
                decay = jnp.exp(b_tot[t, ci * chunk:ci * chunk + 1, :])
                state_t = state_t * decay + incr[t, :, lanes]
        carried[0] = state_t
        yield
        o = o_intra + bmm('tnk,tvk->tnv', per_chunk_lanes(q_dec), st_ref[p * nt:(p + 1) * nt])
        o = o.reshape(part_rows, HEAD_DIM)
        y = o * lax.rsqrt(jnp.mean(o * o, axis=-1, keepdims=True) + EPS) * gain
        o_ref[rows, :] = (y * _silu(z_ref[rows, :].astype(F32))).astype(o_ref.dtype)
        yield

    _run_staggered([part(p) for p in range(HGRN_PARTS)], n_stages=6)


def hgrn2_mixer(h3, lb, gain, col_q, col_f, col_i, col_z):
    bsz, seq, _ = h3.shape

    def col(off):
        return pl.BlockSpec((None, seq, HEAD_DIM), lambda b, h: (b, 0, off + h))

    vec = pl.BlockSpec((None, 1, HEAD_DIM), lambda b, h: (h, 0, 0))
    return pl.pallas_call(
        _hgrn2_kernel,
        out_shape=jax.ShapeDtypeStruct((bsz, seq, WIDTH), BF16),
        grid=(bsz, N_HEADS),
        in_specs=[col(col_q), col(col_f), col(col_i), col(col_z), vec, vec],
        out_specs=pl.BlockSpec((None, seq, HEAD_DIM), lambda b, h: (b, 0, h)),
        scratch_shapes=[pltpu.VMEM((seq // HGRN_TILE, HEAD_DIM, HGRN_TILE // HGRN_CHUNK * HEAD_DIM), BF16)],
        compiler_params=_params("arbitrary", "arbitrary"),
        name="hgrn2",
    )(h3, h3, h3, h3, lb.reshape(N_HEADS, 1, HEAD_DIM), gain.reshape(N_HEADS, 1, HEAD_DIM))


RET_HALF = HEAD_DIM // 2
MOBA_HALF = ROPE_DIM // 2


def _rope_table_kernel(pos_ref, inv_ref, ret_cos_ref, ret_sin_ref, moba_cos_ref, moba_lo_ref, moba_hi_ref):
    ang = pos_ref[...] * inv_ref[...]
    cos = jnp.cos(ang)
    sin = jnp.sin(ang)
    lane = lax.broadcasted_iota(jnp.int32, ang.shape, 1)
    low = lane < RET_HALF
    cos_up = pltpu.roll(cos, RET_HALF, 1)
    sin_up = pltpu.roll(sin, RET_HALF, 1)
    ret_cos_ref[...] = jnp.where(low, cos, cos_up)
    ret_sin_ref[...] = jnp.where(low, -sin, sin_up)
    first = lane < MOBA_HALF
    second = (lane >= MOBA_HALF) & (lane < 2 * MOBA_HALF)
    moba_cos_ref[...] = jnp.where(first, cos_up, jnp.where(second, pltpu.roll(cos_up, MOBA_HALF, 1), 1.0))
    moba_lo_ref[...] = jnp.where(first, -sin_up, 0.0)
    moba_hi_ref[...] = jnp.where(second, pltpu.roll(sin_up, MOBA_HALF, 1), 0.0)


def rope_tables(positions):
    bsz, seq = positions.shape
    inv_ret = RET_THETA ** (-jnp.arange(RET_HALF, dtype=F32) / RET_HALF)
    inv_moba = ROPE_THETA ** (-jnp.arange(MOBA_HALF, dtype=F32) / MOBA_HALF)
    inv = jnp.concatenate([inv_ret, inv_moba, jnp.zeros((HEAD_DIM - RET_HALF - MOBA_HALF,), F32)])
    posf = jnp.broadcast_to(positions.astype(F32)[:, :, None], (bsz, seq, HEAD_DIM))
    spec = pl.BlockSpec((None, seq, HEAD_DIM), lambda b: (b, 0, 0))
    return pl.pallas_call(
        _rope_table_kernel,
        out_shape=[jax.ShapeDtypeStruct((bsz, seq, HEAD_DIM), F32)] * 5,
        grid=(bsz,),
        in_specs=[spec, pl.BlockSpec((1, HEAD_DIM), lambda b: (0, 0))],
        out_specs=[spec] * 5,
        compiler_params=_params("arbitrary"),
        name="rope_tables",
    )(posf, inv.reshape(1, HEAD_DIM))


def _moba_kernel(q_ref, k_ref, v_ref, z_ref, cos_ref, sin_lo_ref, sin_hi_ref, o_ref,
                 qr_ref, ka_ref, va_ref):
    seq = q_ref.shape[0]
    blk = MOBA_BLOCK
    nb = seq // blk
    exp2_scale = HEAD_DIM ** -0.5 * math.log2(math.e)

    def rope(x):
        return (x * cos_ref[...]
                + pltpu.roll(x, HEAD_DIM - MOBA_HALF, 1) * sin_lo_ref[...]
                + pltpu.roll(x, MOBA_HALF, 1) * sin_hi_ref[...])

    kr = rope(k_ref[...].astype(F32))
    k_mean = jnp.mean(kr.reshape(nb, blk, HEAD_DIM), axis=1).astype(BF16)
    ka_ref[:, :HEAD_DIM] = kr.astype(BF16)
    key_blk = lax.broadcasted_iota(jnp.int32, (seq, HEAD_DIM), 0) // blk
    key_lane = lax.broadcasted_iota(jnp.int32, (seq, HEAD_DIM), 1)
    ka_ref[:, HEAD_DIM:] = jnp.where(key_blk == key_lane, MASKED, 0.0).astype(BF16)
    qr_ref[...] = (rope(q_ref[...].astype(F32)) * exp2_scale).astype(BF16)
    va_ref[:, :HEAD_DIM] = v_ref[...]
    va_ref[:, HEAD_DIM:] = jnp.ones((seq, HEAD_DIM), BF16)
    gate_all = _dot_nt(k_mean, qr_ref[...])

    row = lax.broadcasted_iota(jnp.int32, (blk, blk), 0)
    colm = lax.broadcasted_iota(jnp.int32, (blk, blk), 1)
    causal = colm <= row
    eye = jnp.where(colm == row, 1.0, 0.0).astype(BF16)
    blk_row = lax.broadcasted_iota(jnp.int32, (nb, blk), 0)

    def scores(qb):
        rows = slice(qb * blk, (qb + 1) * blk)
        q = qr_ref[rows, :]
        n_keys = (qb + 1) * blk
        if qb > MOBA_TOPK:
            gate = gate_all[:, rows]
            rank = jnp.zeros((nb, blk), F32)
            for j in range(qb):
                gj = gate[j:j + 1, :]
                ahead = (gj > gate) | ((gj == gate) & (blk_row > j))
                rank = rank + jnp.where(ahead, 1.0, 0.0)
            drop = jnp.where((rank >= float(MOBA_TOPK)) & (blk_row < qb), 1.0, 0.0)
            drop = jnp.concatenate([drop, jnp.zeros((HEAD_DIM - nb, blk), F32)], axis=0).astype(BF16)
            drop_col = _dot_nt(eye, drop).astype(BF16)
            s = _dot_nt(jnp.concatenate([q, drop_col], axis=1), ka_ref[:n_keys, :])
        else:
            s = _dot_nt(q, ka_ref[:n_keys, :HEAD_DIM])
        own = jnp.where(causal, s[:, qb * blk:], MASKED)
        return jnp.concatenate([s[:, :qb * blk], own], axis=1) if qb else own

    order = [b for pair in zip(range(nb // 2), range(nb - 1, nb // 2 - 1, -1)) for b in pair]
    s_next = scores(order[0])
    for pos, qb in enumerate(order):
        rows = slice(qb * blk, (qb + 1) * blk)
        s = s_next
        if pos + 1 < nb:
            s_next = scores(order[pos + 1])
        m = jnp.max(s, axis=-1, keepdims=True)
        p = jnp.exp2(s - m).astype(BF16)
        o = _dot(p, va_ref[:(qb + 1) * blk, :])
        o = o[:, :HEAD_DIM] / o[:, HEAD_DIM:]
        o_ref[rows, :] = (o * _silu(z_ref[rows, :].astype(F32))).astype(o_ref.dtype)


def moba_mixer(h3, cos, sin_lo, sin_hi, col_q, col_k, col_v, col_z):
    bsz, seq, _ = h3.shape

    def col(off):
        return pl.BlockSpec((None, seq, HEAD_DIM), lambda b, h: (b, 0, off + h))

    table = pl.BlockSpec((None, seq, HEAD_DIM), lambda b, h: (b, 0, 0))
    return pl.pallas_call(
        _moba_kernel,
        out_shape=jax.ShapeDtypeStruct((bsz, seq, WIDTH), BF16),
        grid=(bsz, N_HEADS),
        in_specs=[col(col_q), col(col_k), col(col_v), col(col_z), table, table, table],
        out_specs=pl.BlockSpec((None, seq, HEAD_DIM), lambda b, h: (b, 0, h)),
        scratch_shapes=[pltpu.VMEM((seq, HEAD_DIM), BF16), pltpu.VMEM((seq, 2 * HEAD_DIM), BF16),
                        pltpu.VMEM((seq, 2 * HEAD_DIM), BF16)],
        compiler_params=_params("arbitrary", "arbitrary"),
        name="moba",
    )(h3, h3, h3, h3, cos, sin_lo, sin_hi)


def _rglru_kernel(x_ref, z_ref, cw_ref, cb_ref, wa_ref, ba_ref, wx_ref, bx_ref, lam_ref,
                  o_ref, a_ref, u_ref, xpad_ref):
    seq, width = x_ref.shape
    sub = 8
    part_rows = seq // RG_PARTS
    part_groups = part_rows // sub

    xpad_ref[:sub, :] = jnp.zeros((sub, width), F32)
    xpad_ref[sub:, :] = x_ref[...].astype(F32)
    nl = -lam_ref[...]
    softplus = jnp.maximum(nl, 0.0) + jnp.log1p(jnp.exp(-jnp.abs(nl)))
    log_a_scale = (-RG_C) * softplus
    s_idx = lax.broadcasted_iota(jnp.int32, (part_groups, sub, width), 1)

    def part(p):
        r0 = p * part_rows
        xf = cb_ref[...]
        for d in range(CONV_WIDTH):
            xf = xf + (xpad_ref[sub - d + r0:sub - d + r0 + part_rows, :]
                       * cw_ref[CONV_WIDTH - 1 - d:CONV_WIDTH - d, :])
        xb = xf.astype(BF16)
        yield

        def gate(w_ref, b_ref):
            pre = [_dot(xb[:, j * HEAD_DIM:(j + 1) * HEAD_DIM], w_ref[j]) for j in range(width // HEAD_DIM)]
            return _sigmoid(jnp.concatenate(pre, axis=1) + b_ref[...])

        r = gate(wa_ref, ba_ref)
        ig = gate(wx_ref, bx_ref)
        yield
        a = jnp.exp(r * log_a_scale)
        u = jnp.sqrt(1.0 - a * a) * ig * xf
        yield
        a3 = a.reshape(part_groups, sub, width)
        u3 = u.reshape(part_groups, sub, width)
        for d in (1, 2, 4):
            keep = s_idx >= d
            a_prev = jnp.where(keep, pltpu.roll(a3, d, 1), 1.0)
            u_prev = jnp.where(keep, pltpu.roll(u3, d, 1), 0.0)
            u3 = u3 + a3 * u_prev
            a3 = a3 * a_prev
        a_ref[p * part_groups:(p + 1) * part_groups] = a3
        u_ref[p * part_groups:(p + 1) * part_groups] = u3
        yield

    _run_staggered([part(p) for p in range(RG_PARTS)], n_stages=4)

    def group_body(g, carry):
        hg = a_ref[g] * carry + u_ref[g]
        u_ref[g] = hg
        return jnp.broadcast_to(hg[sub - 1:sub, :], (sub, width))

    lax.fori_loop(0, seq // sub, group_body, jnp.zeros((sub, width), F32), unroll=8)
    hs = u_ref[...].reshape(seq, width)
    o_ref[...] = (hs * _silu(z_ref[...].astype(F32))).astype(o_ref.dtype)


def rglru_mixer(h3, conv_w, conv_b, wa, ba, wx, bx, lam, col_x, col_z):
    bsz, seq, _ = h3.shape
    width = RG_LANES
    per_step = width // HEAD_DIM

    def col(off):
        return pl.BlockSpec((None, seq, width), lambda b, c: (b, 0, off // per_step + c))

    def vec(rows):
        return pl.BlockSpec((rows, width), lambda b, c: (0, c))

    wspec = pl.BlockSpec((per_step, HEAD_DIM, HEAD_DIM), lambda b, c: (c, 0, 0))
    return pl.pallas_call(
        _rglru_kernel,
        out_shape=jax.ShapeDtypeStruct((bsz, seq, WIDTH), BF16),
        grid=(bsz, WIDTH // width),
        in_specs=[col(col_x), col(col_z), vec(CONV_WIDTH), vec(1), wspec, vec(1), wspec, vec(1), vec(1)],
        out_specs=pl.BlockSpec((None, seq, width), lambda b, c: (b, 0, c)),
        scratch_shapes=[pltpu.VMEM((seq // 8, 8, width), F32)] * 2 + [pltpu.VMEM((seq + 8, width), F32)],
        compiler_params=_params("arbitrary", "arbitrary"),
        name="rglru",
    )(h3, h3, conv_w, conv_b.reshape(1, WIDTH), wa.astype(BF16), ba.reshape(1, WIDTH),
      wx.astype(BF16), bx.reshape(1, WIDTH), lam.reshape(1, WIDTH))


def _retention_body(p_ref, cos_ref, sin_ref, logg_ref, gain_ref, bias_ref, o_ref, st_ref, *, side):
    seq = p_ref.shape[0]
    q_ref, k_ref, v_ref, z_ref = (p_ref.at[:, j * HEAD_DIM:(j + 1) * HEAD_DIM] for j in range(4))
    ck = RET_CHUNK
    half = RET_HALF
    nc = seq // ck // RET_PARTS

    def chunks(x):
        return x.reshape(nc, ck, HEAD_DIM)

    def bmm(spec, a, b):
        return jnp.einsum(spec, a, b, preferred_element_type=F32)

    log_g = logg_ref[...]
    ri = lax.broadcasted_iota(jnp.int32, (ck, ck), 0)
    ci = lax.broadcasted_iota(jnp.int32, (ck, ck), 1)
    diff = (ri - ci).astype(F32)
    dmask = jnp.where(ri >= ci, jnp.exp(jnp.maximum(diff, 0.0) * log_g[:, :1]), 0.0)
    idx = lax.broadcasted_iota(jnp.int32, (ck, HEAD_DIM), 0).astype(F32)
    q_fac = jnp.exp((idx + 1.0) * log_g)
    k_fac = jnp.exp((ck - 1.0 - idx) * log_g)
    chunk_decay = jnp.exp(float(ck) * log_g)

    carried = [jnp.zeros((HEAD_DIM, HEAD_DIM), F32)]

    def part(p):
        rows = slice(p * nc * ck, (p + 1) * nc * ck)
        cos = cos_ref[rows, :]
        sin = sin_ref[rows, :]
        q = q_ref[rows, :].astype(F32)
        k = k_ref[rows, :].astype(F32)
        q = chunks(q * cos + pltpu.roll(q, half, 1) * sin)
        k = chunks((k * cos + pltpu.roll(k, half, 1) * sin) * (HEAD_DIM ** -0.5))
        v = chunks(v_ref[rows, :])
        yield
        att = bmm('cid,cjd->cij', q.astype(BF16), k.astype(BF16)) * dmask[None]
        k_end_t = jnp.swapaxes(k * k_fac[None], 1, 2).astype(BF16)
        incr = bmm('ckn,cnv->ckv', k_end_t, v)
        yield
        o_intra = bmm('cij,cjd->cid', att.astype(BF16), v)
        state = carried[0]
        for c in range(nc):
            st_ref[p * nc + c] = state.astype(BF16)
            state = state * chunk_decay + incr[c]
        carried[0] = state
        yield
        o = o_intra + bmm('cnk,ckv->cnv', (q * q_fac[None]).astype(BF16), st_ref[p * nc:(p + 1) * nc])
        o = o.reshape(nc * ck, HEAD_DIM)
        mu = jnp.mean(o, axis=-1, keepdims=True)
        oc = o - mu
        var = jnp.mean(oc * oc, axis=-1, keepdims=True)
        y = oc * lax.rsqrt(var + EPS) * gain_ref[...] + bias_ref[...]
        o_ref[rows, :] = (y * _silu(z_ref[rows, :].astype(F32))).astype(o_ref.dtype)
        yield

    _run_staggered([part(p) for p in range(RET_PARTS)], n_stages=4, between=_spread(side))


def retention_mixer(xn, w_heads, cos, sin, log_g, gain, bias):
    seq = xn.shape[1]
    table = pl.BlockSpec((None, seq, HEAD_DIM), lambda i: (i // N_HEADS, 0, 0))
    vec = pl.BlockSpec((None, 1, HEAD_DIM), lambda i: (i % N_HEADS, 0, 0))
    return fused_mixer(
        _retention_body, xn, w_heads,
        extra=[cos, sin, log_g, gain.reshape(N_HEADS, 1, HEAD_DIM), bias.reshape(N_HEADS, 1, HEAD_DIM)],
        extra_specs=[table, table, vec, vec, vec],
        scratch=[pltpu.VMEM((seq // RET_CHUNK, HEAD_DIM, HEAD_DIM), BF16)],
        out_width=HEAD_DIM, name="retention")


def _post_kernel(x_ref, ya_ref, yb_ref, woa_ref, wob_ref, g_ref, wq_ref, kv_ref, wo_ref, ng_ref,
                 *o_refs):
    x1 = x_ref[...] + _dot(ya_ref[...], woa_ref[...]) + _dot(yb_ref[...], wob_ref[...])
    ms = jnp.mean(x1 * x1, axis=-1, keepdims=True)
    xn = (x1 * lax.rsqrt(ms + EPS) * g_ref[...]).astype(BF16)
    q = _dot(xn, wq_ref[...]).astype(BF16)
    scale = HEAD_DIM ** -0.5
    heads = []
    for hd in range(XA_HEADS):
        cs = slice(hd * HEAD_DIM, (hd + 1) * HEAD_DIM)
        k = kv_ref[:, cs]
        v = kv_ref[:, XA_WIDTH + hd * HEAD_DIM:XA_WIDTH + (hd + 1) * HEAD_DIM]
        s = _dot_nt(q[:, cs], k) * scale
        p = jnp.exp(s - jnp.max(s, axis=-1, keepdims=True))
        p = p / jnp.sum(p, axis=-1, keepdims=True)
        heads.append(_dot(p.astype(BF16), v).astype(BF16))
    o = jnp.concatenate(heads, axis=1)
    x2 = x1 + _dot(o, wo_ref[...])
    ms2 = jnp.mean(x2 * x2, axis=-1, keepdims=True)
    normed = x2 * lax.rsqrt(ms2 + EPS) * ng_ref[...]
    if len(o_refs) == 2:
        o_refs[0][...] = x2
    o_refs[-1][...] = normed.astype(o_refs[-1].dtype)


def post_mixer(x3, ya, yb, w_out, xa_gain, wq, kv, wo, next_gain, final, tm=1024):
    bsz, seq, d = x3.shape
    full = lambda shape: pl.BlockSpec(shape, lambda b, i: (0,) * len(shape))
    tile = lambda w: pl.BlockSpec((None, tm, w), lambda b, i: (b, i, 0))
    stream = jax.ShapeDtypeStruct((bsz, seq, d), F32)
    return pl.pallas_call(
        _post_kernel,
        out_shape=stream if final else [stream, jax.ShapeDtypeStruct((bsz, seq, d), BF16)],
        grid=(bsz, seq // tm),
        in_specs=[tile(d), tile(WIDTH), tile(WIDTH),
                  full((WIDTH, d)), full((WIDTH, d)), full((1, d)), full((d, XA_WIDTH)),
                  pl.BlockSpec((None, MEM_LEN, 2 * XA_WIDTH), lambda b, i: (b, 0, 0)),
                  full((XA_WIDTH, d)), full((1, d))],
        out_specs=tile(d) if final else [tile(d), tile(d)],
        compiler_params=_params("arbitrary", "arbitrary"),
        name="post_final" if final else "post",
    )(x3, ya, yb, w_out[:WIDTH].astype(BF16), w_out[WIDTH:].astype(BF16), xa_gain.reshape(1, d),
      wq.astype(BF16), kv, wo.astype(BF16), next_gain.reshape(1, d))


def kernel(x, mem, positions, hgrn_lb_logits, ev_norm, ev_w_in, ev_hgrn_gain, ev_w_out, od_norm, od_w_in, od_conv_w, od_conv_b, od_rg_wa, od_rg_ba, od_rg_wx, od_rg_bx, od_rg_lambda, od_ret_gain, od_ret_bias, od_w_out, xa_norm, xa_mem_norm, xa_wq, xa_wkv, xa_wo, final_norm):
    bsz, seq, d = x.shape
    tokens = bsz * seq
    lb_all = jnp.cumsum(jax.nn.softmax(hgrn_lb_logits.astype(F32), axis=0), axis=0)
    ret_cos, ret_sin, moba_cos, moba_lo, moba_hi = rope_tables(positions)
    log_g =jnp.log(1.0 - 2.0 ** (-5.0 - jnp.arange(N_HEADS, dtype=F32)))
    log_g = jnp.broadcast_to(log_g[:, None, None], (N_HEADS, 1, HEAD_DIM))
    mem2 = mem.reshape(bsz * MEM_LEN, d)

    def memory_kv(layer):
        kv = norm_matmul(mem2, xa_mem_norm[layer], xa_wkv[layer], BF16,
                         tm=bsz * MEM_LEN // 2, tn=2 * XA_WIDTH, name="mem_kv")
        return kv.reshape(bsz, MEM_LEN, 2 * XA_WIDTH)

    h = norm_matmul(x.reshape(tokens, d), ev_norm[0], ev_w_in.reshape(ev_w_in.shape[1:]), BF16,
                    tm=2048, tn=1024, name="in_proj_even")
    h3 = h.reshape(bsz, seq, -1)
    nb = WIDTH // HEAD_DIM
    ya = hgrn2_mixer(h3, lb_all[0], ev_hgrn_gain[0], 0, nb, 2 * nb, 6 * nb)
    yb = moba_mixer(h3, moba_cos, moba_lo, moba_hi, 3 * nb, 4 * nb, 5 * nb, 7 * nb)
    x1, xn1 = post_mixer(x, ya, yb, ev_w_out[0], xa_norm[0], xa_wq[0], memory_kv(0), xa_wo[0],
                         od_norm[0], final=False)

    w_in = od_w_in[0]
    w_rg = jnp.concatenate([w_in[:, :WIDTH], w_in[:, 4 * WIDTH:5 * WIDTH]], axis=1)
    h = norm_matmul(x1.reshape(tokens, d), od_norm[0], w_rg, BF16, tm=2048, tn=1024, name="in_proj_odd")
    yc = rglru_mixer(h.reshape(bsz, seq, -1), od_conv_w[0], od_conv_b[0], od_rg_wa[0], od_rg_ba[0],
                     od_rg_wx[0], od_rg_bx[0], od_rg_lambda[0], 0, nb)
    w_ret = group_columns(w_in, [WIDTH, 2 * WIDTH, 3 * WIDTH, 5 * WIDTH], HEAD_DIM, N_HEADS)
    yd = retention_mixer(xn1, w_ret, ret_cos, ret_sin, log_g, od_ret_gain[0], od_ret_bias[0])
    return post_mixer(x1, yc, yd, od_w_out[0], xa_norm[1], xa_wq[1], memory_kv(1), xa_wo[1],
                      final_norm, final=True)
```

```python
import functools
import math

import jax
import jax.numpy as jnp
from jax import lax
from jax.experimental import pallas as pl
from jax.experimental.pallas import tpu as pltpu

F32 = jnp.float32
BF16 = jnp.bfloat16

D_MODEL = 1024
HEAD_DIM = 128
N_HEADS = 8
WIDTH = N_HEADS * HEAD_DIM
MEM_LEN = 256
HGRN_CHUNK = 32
HGRN_TILE = 128
HGRN_PARTS = 16
MOBA_BLOCK = 256
MOBA_TOPK = 3
RET_CHUNK = 256
RET_PARTS = 2
CONV_WIDTH = 4
RG_C = 8.0
ROPE_THETA = 500000.0
ROPE_DIM = HEAD_DIM // 4
RET_THETA = 10000.0
XA_HEADS = 4
XA_WIDTH = XA_HEADS * HEAD_DIM
EPS = 1e-6
VMEM_LIMIT = 56 * 1024 * 1024
MASKED = -1e30

_NT = (((1,), (1,)), ((), ()))


def _dot(a, b):
    return jnp.dot(a, b, preferred_element_type=F32)


def _dot_nt(a, b):
    return lax.dot_general(a, b, _NT, preferred_element_type=F32)


def _sigmoid(x):
    return 1.0 / (1.0 + jnp.exp(-x))


def _silu(x):
    return x * _sigmoid(x)


def _run_staggered(parts, n_stages):
    for step in range(n_stages + len(parts) - 1):
        for p, gen in enumerate(parts):
            if 0 <= step - p < n_stages:
                next(gen)


def _params(*sem):
    return pltpu.CompilerParams(dimension_semantics=sem, vmem_limit_bytes=VMEM_LIMIT)


def _norm_matmul_kernel(x_ref, g_ref, w_ref, o_ref, xn_ref):
    @pl.when(pl.program_id(1) == 0)
    def _():
        x = x_ref[...]
        ms = jnp.mean(x * x, axis=-1, keepdims=True)
        xn_ref[...] = (x * lax.rsqrt(ms + EPS) * g_ref[...]).astype(BF16)

    o_ref[...] = _dot(xn_ref[...], w_ref[...].astype(BF16)).astype(o_ref.dtype)


def norm_matmul(x, gain, w, out_dtype, tm, tn, name):
    t, d = x.shape
    n = w.shape[1]
    return pl.pallas_call(
        _norm_matmul_kernel,
        out_shape=jax.ShapeDtypeStruct((t, n), out_dtype),
        grid=(t // tm, n // tn),
        in_specs=[
            pl.BlockSpec((tm, d), lambda i, j: (i, 0)),
            pl.BlockSpec((1, d), lambda i, j: (0, 0)),
            pl.BlockSpec((d, tn), lambda i, j: (0, j)),
        ],
        out_specs=pl.BlockSpec((tm, tn), lambda i, j: (i, j)),
        scratch_shapes=[pltpu.VMEM((tm, d), BF16)],
        compiler_params=_params("arbitrary", "arbitrary"),
        name=name,
    )(x, gain.reshape(1, d), w)


def _split2(x):
    hi = x.astype(BF16)
    lo = (x - hi.astype(F32)).astype(BF16)
    return hi, lo


def _hgrn2_kernel(q_ref, f_ref, i_ref, z_ref, lb_ref, gain_ref, o_ref, st_ref):
    seq = q_ref.shape[0]
    tile, chunk = HGRN_TILE, HGRN_CHUNK
    per_tile = tile // chunk
    part_rows = seq // HGRN_PARTS
    nt = part_rows // tile
    lb = lb_ref[...]
    gain = gain_ref[...]

    def tiles(x):
        return x.reshape(nt, tile, HEAD_DIM)

    def bmm(spec, a, b):
        return jnp.einsum(spec, a, b, preferred_element_type=F32)

    ti = lax.broadcasted_iota(jnp.int32, (tile, tile), 0)
    tj = lax.broadcasted_iota(jnp.int32, (tile, tile), 1)
    intra_mask = (((ti // chunk) == (tj // chunk)) & (tj <= ti))[None]
    sum_mat = jnp.broadcast_to(jnp.where(intra_mask, 1.0, 0.0).astype(BF16), (nt, tile, tile))
    row_chunk = (lax.broadcasted_iota(jnp.int32, (tile, HEAD_DIM), 0) // chunk)[None]

    def per_chunk_lanes(x):
        zero = jnp.zeros_like(x)
        return jnp.concatenate([jnp.where(row_chunk == ci, x, zero) for ci in range(per_tile)], axis=-1)

    carried = [jnp.zeros((HEAD_DIM, HEAD_DIM), F32)]

    def part(p):
        rows = slice(p * part_rows, (p + 1) * part_rows)
        f = lb + (1.0 - lb) * _sigmoid(f_ref[rows, :].astype(F32))
        k = tiles(1.0 - f)
        hi, lo = _split2(jnp.log(f))
        yield
        b = bmm('trc,tcd->trd', sum_mat, tiles(hi)) + bmm('trc,tcd->trd', sum_mat, tiles(lo))
        b_chunks = b.reshape(part_rows // chunk, chunk, HEAD_DIM)
        b_tot = jnp.broadcast_to(b_chunks[:, chunk - 1:, :], b_chunks.shape).reshape(b.shape)
        yield
        q_dec = (tiles(q_ref[rows, :].astype(F32)) * jnp.exp(b)).astype(BF16)
        k_inv = (k * jnp.exp(-b)).astype(BF16)
        k_end = (k * jnp.exp(b_tot - b)).astype(BF16)
        v = tiles(i_ref[rows, :].astype(F32))
        yield
        att = bmm('tid,tjd->tij', q_dec, k_inv)
        att = jnp.where(intra_mask, att, 0.0).astype(BF16)
        v_t = jnp.swapaxes(v, 1, 2).astype(BF16)
        incr = bmm('tvn,tnk->tvk', v_t, per_chunk_lanes(k_end))
        yield
        o_intra = bmm('tij,tjd->tid', att, v.astype(BF16))
        state_t = carried[0]
        for t in range(nt):
            for ci in range(per_tile):
                lanes = slice(ci * HEAD_DIM, (ci + 1) * HEAD_DIM)
                st_ref[p * nt + t, :, lanes] = state_t.astype(BF16)
                decay = jnp.exp(b_tot[t, ci * chunk:ci * chunk + 1, :])
                state_t = state_t * decay + incr[t, :, lanes]
        carried[0] = state_t
        yield
        o = o_intra + bmm('tnk,tvk->tnv', per_chunk_lanes(q_dec), st_ref[p * nt:(p + 1) * nt])
        o = o.reshape(part_rows, HEAD_DIM)
        y = o * lax.rsqrt(jnp.mean(o * o, axis=-1, keepdims=True) + EPS) * gain
        o_ref[rows, :] = (y * _silu(z_ref[rows, :].astype(F32))).astype(o_ref.dtype)
        yield

    _run_staggered([part(p) for p in range(HGRN_PARTS)], n_stages=6)


def hgrn2_mixer(h3, lb, gain, col_q, col_f, col_i, col_z):
    bsz, seq, _ = h3.shape

    def col(off):
        return pl.BlockSpec((None, seq, HEAD_DIM), lambda b, h: (b, 0, off + h))

    vec = pl.BlockSpec((None, 1, HEAD_DIM), lambda b, h: (h, 0, 0))
    return pl.pallas_call(
        _hgrn2_kernel,
        out_shape=jax.ShapeDtypeStruct((bsz, seq, WIDTH), BF16),
        grid=(bsz, N_HEADS),
        in_specs=[col(col_q), col(col_f), col(col_i), col(col_z), vec, vec],
        out_specs=pl.BlockSpec((None, seq, HEAD_DIM), lambda b, h: (b, 0, h)),
        scratch_shapes=[pltpu.VMEM((seq // HGRN_TILE, HEAD_DIM, HGRN_TILE // HGRN_CHUNK * HEAD_DIM), BF16)],
        compiler_params=_params("arbitrary", "arbitrary"),
        name="hgrn2",
    )(h3, h3, h3, h3, lb.reshape(N_HEADS, 1, HEAD_DIM), gain.reshape(N_HEADS, 1, HEAD_DIM))


RET_HALF = HEAD_DIM // 2
MOBA_HALF = ROPE_DIM // 2


def _rope_table_kernel(pos_ref, inv_ref, ret_cos_ref, ret_sin_ref, moba_cos_ref, moba_lo_ref, moba_hi_ref):
    ang = pos_ref[...] * inv_ref[...]
    cos = jnp.cos(ang)
    sin = jnp.sin(ang)
    lane = lax.broadcasted_iota(jnp.int32, ang.shape, 1)
    low = lane < RET_HALF
    cos_up = pltpu.roll(cos, RET_HALF, 1)
    sin_up = pltpu.roll(sin, RET_HALF, 1)
    ret_cos_ref[...] = jnp.where(low, cos, cos_up)
    ret_sin_ref[...] = jnp.where(low, -sin, sin_up)
    first = lane < MOBA_HALF
    second = (lane >= MOBA_HALF) & (lane < 2 * MOBA_HALF)
    moba_cos_ref[...] = jnp.where(first, cos_up, jnp.where(second, pltpu.roll(cos_up, MOBA_HALF, 1), 1.0))
    moba_lo_ref[...] = jnp.where(first, -sin_up, 0.0)
    moba_hi_ref[...] = jnp.where(second, pltpu.roll(sin_up, MOBA_HALF, 1), 0.0)


def rope_tables(positions):
    bsz, seq = positions.shape
    inv_ret = RET_THETA ** (-jnp.arange(RET_HALF, dtype=F32) / RET_HALF)
    inv_moba = ROPE_THETA ** (-jnp.arange(MOBA_HALF, dtype=F32) / MOBA_HALF)
    inv = jnp.concatenate([inv_ret, inv_moba, jnp.zeros((HEAD_DIM - RET_HALF - MOBA_HALF,), F32)])
    posf = jnp.broadcast_to(positions.astype(F32)[:, :, None], (bsz, seq, HEAD_DIM))
    spec = pl.BlockSpec((None, seq, HEAD_DIM), lambda b: (b, 0, 0))
    return pl.pallas_call(
        _rope_table_kernel,
        out_shape=[jax.ShapeDtypeStruct((bsz, seq, HEAD_DIM), F32)] * 5,
        grid=(bsz,),
        in_specs=[spec, pl.BlockSpec((1, HEAD_DIM), lambda b: (0, 0))],
        out_specs=[spec] * 5,
        compiler_params=_params("arbitrary"),
        name="rope_tables",
    )(posf, inv.reshape(1, HEAD_DIM))


def _moba_kernel(q_ref, k_ref, v_ref, z_ref, cos_ref, sin_lo_ref, sin_hi_ref, o_ref,
                 qr_ref, ka_ref, va_ref):
    seq = q_ref.shape[0]
    blk = MOBA_BLOCK
    nb = seq // blk
    exp2_scale = HEAD_DIM ** -0.5 * math.log2(math.e)

    def rope(x):
        return (x * cos_ref[...]
                + pltpu.roll(x, HEAD_DIM - MOBA_HALF, 1) * sin_lo_ref[...]
                + pltpu.roll(x, MOBA_HALF, 1) * sin_hi_ref[...])

    kr = rope(k_ref[...].astype(F32))
    k_mean = jnp.mean(kr.reshape(nb, blk, HEAD_DIM), axis=1).astype(BF16)
    ka_ref[:, :HEAD_DIM] = kr.astype(BF16)
    key_blk = lax.broadcasted_iota(jnp.int32, (seq, HEAD_DIM), 0) // blk
    key_lane = lax.broadcasted_iota(jnp.int32, (seq, HEAD_DIM), 1)
    ka_ref[:, HEAD_DIM:] = jnp.where(key_blk == key_lane, MASKED, 0.0).astype(BF16)
    qr_ref[...] = (rope(q_ref[...].astype(F32)) * exp2_scale).astype(BF16)
    va_ref[:, :HEAD_DIM] = v_ref[...]
    va_ref[:, HEAD_DIM:] = jnp.ones((seq, HEAD_DIM), BF16)
    gate_all = _dot_nt(k_mean, qr_ref[...])

    row = lax.broadcasted_iota(jnp.int32, (blk, blk), 0)
    colm = lax.broadcasted_iota(jnp.int32, (blk, blk), 1)
    causal = colm <= row
    eye = jnp.where(colm == row, 1.0, 0.0).astype(BF16)
    blk_row = lax.broadcasted_iota(jnp.int32, (nb, blk), 0)

    def scores(qb):
        rows = slice(qb * blk, (qb + 1) * blk)
        q = qr_ref[rows, :]
        n_keys = (qb + 1) * blk
        if qb > MOBA_TOPK:
            gate = gate_all[:, rows]
            rank = jnp.zeros((nb, blk), F32)
            for j in range(qb):
                gj = gate[j:j + 1, :]
                ahead = (gj > gate) | ((gj == gate) & (blk_row > j))
                rank = rank + jnp.where(ahead, 1.0, 0.0)
            drop = jnp.where((rank >= float(MOBA_TOPK)) & (blk_row < qb), 1.0, 0.0)
            drop = jnp.concatenate([drop, jnp.zeros((HEAD_DIM - nb, blk), F32)], axis=0).astype(BF16)
            drop_col = _dot_nt(eye, drop).astype(BF16)
            s = _dot_nt(jnp.concatenate([q, drop_col], axis=1), ka_ref[:n_keys, :])
        else:
            s = _dot_nt(q, ka_ref[:n_keys, :HEAD_DIM])
        own = jnp.where(causal, s[:, qb * blk:], MASKED)
        return jnp.concatenate([s[:, :qb * blk], own], axis=1) if qb else own

    order = [b for pair in zip(range(nb // 2), range(nb - 1, nb // 2 - 1, -1)) for b in pair]
    s_next = scores(order[0])
    for pos, qb in enumerate(order):
        rows = slice(qb * blk, (qb + 1) * blk)
        s = s_next
        if pos + 1 < nb:
            s_next = scores(order[pos + 1])
        m = jnp.max(s, axis=-1, keepdims=True)
        p = jnp.exp2(s - m).astype(BF16)
        o = _dot(p, va_ref[:(qb + 1) * blk, :])
        o = o[:, :HEAD_DIM] / o[:, HEAD_DIM:]
        o_ref[rows, :] = (o * _silu(z_ref[rows, :].astype(F32))).astype(o_ref.dtype)


def moba_mixer(h3, cos, sin_lo, sin_hi, col_q, col_k, col_v, col_z):
    bsz, seq, _ = h3.shape

    def col(off):
        return pl.BlockSpec((None, seq, HEAD_DIM), lambda b, h: (b, 0, off + h))

    table = pl.BlockSpec((None, seq, HEAD_DIM), lambda b, h: (b, 0, 0))
    return pl.pallas_call(
        _moba_kernel,
        out_shape=jax.ShapeDtypeStruct((bsz, seq, WIDTH), BF16),
        grid=(bsz, N_HEADS),
        in_specs=[col(col_q), col(col_k), col(col_v), col(col_z), table, table, table],
        out_specs=pl.BlockSpec((None, seq, HEAD_DIM), lambda b, h: (b, 0, h)),
        scratch_shapes=[pltpu.VMEM((seq, HEAD_DIM), BF16), pltpu.VMEM((seq, 2 * HEAD_DIM), BF16),
                        pltpu.VMEM((seq, 2 * HEAD_DIM), BF16)],
        compiler_params=_params("arbitrary", "arbitrary"),
        name="moba",
    )(h3, h3, h3, h3, cos, sin_lo, sin_hi)


def _rglru_kernel(x_ref, z_ref, cw_ref, cb_ref, wa_ref, ba_ref, wx_ref, bx_ref, lam_ref,
                  o_ref, a_ref, u_ref, xpad_ref):
    seq, width = x_ref.shape
    sub = 8
    groups = seq // sub

    xpad_ref[:sub, :] = jnp.zeros((sub, width), F32)
    xpad_ref[sub:, :] = x_ref[...].astype(F32)
    xf = cb_ref[...]
    for d in range(CONV_WIDTH):
        xf = xf + xpad_ref[sub - d:sub - d + seq, :] * cw_ref[CONV_WIDTH - 1 - d:CONV_WIDTH - d, :]

    xb = xf.astype(BF16)

    def gate(w_ref, b_ref):
        pre = [_dot(xb[:, j * HEAD_DIM:(j + 1) * HEAD_DIM], w_ref[j]) for j in range(width // HEAD_DIM)]
        return _sigmoid(jnp.concatenate(pre, axis=1) + b_ref[...])

    r = gate(wa_ref, ba_ref)
    ig = gate(wx_ref, bx_ref)
    nl = -lam_ref[...]
    softplus = jnp.maximum(nl, 0.0) + jnp.log1p(jnp.exp(-jnp.abs(nl)))
    a = jnp.exp(r * ((-RG_C) * softplus))
    u = jnp.sqrt(1.0 - a * a) * ig * xf

    a3 = a.reshape(groups, sub, width)
    u3 = u.reshape(groups, sub, width)
    s_idx = lax.broadcasted_iota(jnp.int32, a3.shape, 1)
    for d in (1, 2, 4):
        keep = s_idx >= d
        a_prev = jnp.where(keep, pltpu.roll(a3, d, 1), 1.0)
        u_prev = jnp.where(keep, pltpu.roll(u3, d, 1), 0.0)
        u3 = u3 + a3 * u_prev
        a3 = a3 * a_prev
    a_ref[...] = a3
    u_ref[...] = u3

    def group_body(g, carry):
        hg = a_ref[g] * carry + u_ref[g]
        u_ref[g] = hg
        return jnp.broadcast_to(hg[sub - 1:sub, :], (sub, width))

    lax.fori_loop(0, groups, group_body, jnp.zeros((sub, width), F32), unroll=8)
    hs = u_ref[...].reshape(seq, width)
    o_ref[...] = (hs * _silu(z_ref[...].astype(F32))).astype(o_ref.dtype)


def rglru_mixer(h3, conv_w, conv_b, wa, ba, wx, bx, lam, col_x, col_z, blocks_per_step=2):
    bsz, seq, _ = h3.shape
    width = blocks_per_step * HEAD_DIM

    def col(off):
        return pl.BlockSpec((None, seq, width), lambda b, c: (b, 0, off // blocks_per_step + c))

    def vec(rows):
        return pl.BlockSpec((rows, width), lambda b, c: (0, c))

    wspec = pl.BlockSpec((blocks_per_step, HEAD_DIM, HEAD_DIM), lambda b, c: (c, 0, 0))
    return pl.pallas_call(
        _rglru_kernel,
        out_shape=jax.ShapeDtypeStruct((bsz, seq, WIDTH), BF16),
        grid=(bsz, WIDTH // width),
        in_specs=[col(col_x), col(col_z), vec(CONV_WIDTH), vec(1), wspec, vec(1), wspec, vec(1), vec(1)],
        out_specs=pl.BlockSpec((None, seq, width), lambda b, c: (b, 0, c)),
        scratch_shapes=[pltpu.VMEM((seq // 8, 8, width), F32)] * 2 + [pltpu.VMEM((seq + 8, width), F32)],
        compiler_params=_params("arbitrary", "arbitrary"),
        name="rglru",
    )(h3, h3, conv_w, conv_b.reshape(1, WIDTH), wa.astype(BF16), ba.reshape(1, WIDTH),
      wx.astype(BF16), bx.reshape(1, WIDTH), lam.reshape(1, WIDTH))


def _retention_kernel(q_ref, k_ref, v_ref, z_ref, cos_ref, sin_ref, logg_ref, gain_ref, bias_ref,
                      o_ref, st_ref):
    seq = q_ref.shape[0]
    ck = RET_CHUNK
    half = RET_HALF
    nc = seq // ck // RET_PARTS

    def chunks(x):
        return x.reshape(nc, ck, HEAD_DIM)

    def bmm(spec, a, b):
        return jnp.einsum(spec, a, b, preferred_element_type=F32)

    log_g = logg_ref[...]
    ri = lax.broadcasted_iota(jnp.int32, (ck, ck), 0)
    ci = lax.broadcasted_iota(jnp.int32, (ck, ck), 1)
    diff = (ri - ci).astype(F32)
    dmask = jnp.where(ri >= ci, jnp.exp(jnp.maximum(diff, 0.0) * log_g[:, :1]), 0.0)
    idx = lax.broadcasted_iota(jnp.int32, (ck, HEAD_DIM), 0).astype(F32)
    q_fac = jnp.exp((idx + 1.0) * log_g)
    k_fac = jnp.exp((ck - 1.0 - idx) * log_g)
    chunk_decay = jnp.exp(float(ck) * log_g)

    carried = [jnp.zeros((HEAD_DIM, HEAD_DIM), F32)]

    def part(p):
        rows = slice(p * nc * ck, (p + 1) * nc * ck)
        cos = cos_ref[rows, :]
        sin = sin_ref[rows, :]
        q = q_ref[rows, :].astype(F32)
        k = k_ref[rows, :].astype(F32)
        q = chunks(q * cos + pltpu.roll(q, half, 1) * sin)
        k = chunks((k * cos + pltpu.roll(k, half, 1) * sin) * (HEAD_DIM ** -0.5))
        v = chunks(v_ref[rows, :])
        yield
        att = bmm('cid,cjd->cij', q.astype(BF16), k.astype(BF16)) * dmask[None]
        k_end_t = jnp.swapaxes(k * k_fac[None], 1, 2).astype(BF16)
        incr = bmm('ckn,cnv->ckv', k_end_t, v)
        yield
        o_intra = bmm('cij,cjd->cid', att.astype(BF16), v)
        state = carried[0]
        for c in range(nc):
            st_ref[p * nc + c] = state.astype(BF16)
            state = state * chunk_decay + incr[c]
        carried[0] = state
        yield
        o = o_intra + bmm('cnk,ckv->cnv', (q * q_fac[None]).astype(BF16), st_ref[p * nc:(p + 1) * nc])
        o = o.reshape(nc * ck, HEAD_DIM)
        mu = jnp.mean(o, axis=-1, keepdims=True)
        oc = o - mu
        var = jnp.mean(oc * oc, axis=-1, keepdims=True)
        y = oc * lax.rsqrt(var + EPS) * gain_ref[...] + bias_ref[...]
        o_ref[rows, :] = (y * _silu(z_ref[rows, :].astype(F32))).astype(o_ref.dtype)
        yield

    _run_staggered([part(p) for p in range(RET_PARTS)], n_stages=4)


def retention_mixer(h3, cos, sin, log_g, gain, bias, col_q, col_k, col_v, col_z):
    bsz, seq, _ = h3.shape

    def col(off):
        return pl.BlockSpec((None, seq, HEAD_DIM), lambda b, h: (b, 0, off + h))

    table = pl.BlockSpec((None, seq, HEAD_DIM), lambda b, h: (b, 0, 0))
    vec = pl.BlockSpec((None, 1, HEAD_DIM), lambda b, h: (h, 0, 0))
    return pl.pallas_call(
        _retention_kernel,
        out_shape=jax.ShapeDtypeStruct((bsz, seq, WIDTH), BF16),
        grid=(bsz, N_HEADS),
        in_specs=[col(col_q), col(col_k), col(col_v), col(col_z), table, table, vec, vec, vec],
        out_specs=pl.BlockSpec((None, seq, HEAD_DIM), lambda b, h: (b, 0, h)),
        scratch_shapes=[pltpu.VMEM((seq // RET_CHUNK, HEAD_DIM, HEAD_DIM), BF16)],
        compiler_params=_params("arbitrary", "arbitrary"),
        name="retention",
    )(h3, h3, h3, h3, cos, sin, log_g,
      gain.reshape(N_HEADS, 1, HEAD_DIM), bias.reshape(N_HEADS, 1, HEAD_DIM))


def _post_kernel(x_ref, ya_ref, yb_ref, woa_ref, wob_ref, g_ref, wq_ref, kv_ref, wo_ref, fg_ref,
                 o_ref, *, final):
    x1 = x_ref[...] + _dot(ya_ref[...], woa_ref[...]) + _dot(yb_ref[...], wob_ref[...])
    ms = jnp.mean(x1 * x1, axis=-1, keepdims=True)
    xn = (x1 * lax.rsqrt(ms + EPS) * g_ref[...]).astype(BF16)
    q = _dot(xn, wq_ref[...]).astype(BF16)
    scale = HEAD_DIM ** -0.5
    heads = []
    for hd in range(XA_HEADS):
        cs = slice(hd * HEAD_DIM, (hd + 1) * HEAD_DIM)
        k = kv_ref[:, cs]
        v = kv_ref[:, XA_WIDTH + hd * HEAD_DIM:XA_WIDTH + (hd + 1) * HEAD_DIM]
        s = _dot_nt(q[:, cs], k) * scale
        p = jnp.exp(s - jnp.max(s, axis=-1, keepdims=True))
        p = p / jnp.sum(p, axis=-1, keepdims=True)
        heads.append(_dot(p.astype(BF16), v).astype(BF16))
    o = jnp.concatenate(heads, axis=1)
    x2 = x1 + _dot(o, wo_ref[...])
    if final:
        ms2 = jnp.mean(x2 * x2, axis=-1, keepdims=True)
        x2 = x2 * lax.rsqrt(ms2 + EPS) * fg_ref[...]
    o_ref[...] = x2


def post_mixer(x3, ya, yb, w_out, xa_gain, wq, kv, wo, final_gain, final, tm=1024):
    bsz, seq, d = x3.shape
    full = lambda shape: pl.BlockSpec(shape, lambda b, i: (0,) * len(shape))
    tile = lambda w: pl.BlockSpec((None, tm, w), lambda b, i: (b, i, 0))
    return pl.pallas_call(
        functools.partial(_post_kernel, final=final),
        out_shape=jax.ShapeDtypeStruct((bsz, seq, d), F32),
        grid=(bsz, seq // tm),
        in_specs=[tile(d), tile(WIDTH), tile(WIDTH),
                  full((WIDTH, d)), full((WIDTH, d)), full((1, d)), full((d, XA_WIDTH)),
                  pl.BlockSpec((None, MEM_LEN, 2 * XA_WIDTH), lambda b, i: (b, 0, 0)),
                  full((XA_WIDTH, d)), full((1, d))],
        out_specs=tile(d),
        compiler_params=_params("arbitrary", "arbitrary"),
        name="post_final" if final else "post",
    )(x3, ya, yb, w_out[:WIDTH].astype(BF16), w_out[WIDTH:].astype(BF16), xa_gain.reshape(1, d),
      wq.astype(BF16), kv, wo.astype(BF16), final_gain.reshape(1, d))


def kernel(x, mem, positions, hgrn_lb_logits, ev_norm, ev_w_in, ev_hgrn_gain, ev_w_out, od_norm, od_w_in, od_conv_w, od_conv_b, od_rg_wa, od_rg_ba, od_rg_wx, od_rg_bx, od_rg_lambda, od_ret_gain, od_ret_bias, od_w_out, xa_norm, xa_mem_norm, xa_wq, xa_wkv, xa_wo, final_norm):
    bsz, seq, d = x.shape
    tokens = bsz * seq
    lb_all = jnp.cumsum(jax.nn.softmax(hgrn_lb_logits.astype(F32), axis=0), axis=0)
    ret_cos, ret_sin, moba_cos, moba_lo, moba_hi = rope_tables(positions)
    log_g = jnp.log(1.0 - 2.0 ** (-5.0 - jnp.arange(N_HEADS, dtype=F32)))
    log_g = jnp.broadcast_to(log_g[:, None, None], (N_HEADS, 1, HEAD_DIM))
    mem2 = mem.reshape(bsz * MEM_LEN, d)

    def memory_kv(layer):
        kv = norm_matmul(mem2, xa_mem_norm[layer], xa_wkv[layer], BF16,
                         tm=bsz * MEM_LEN // 2, tn=2 * XA_WIDTH, name="mem_kv")
        return kv.reshape(bsz, MEM_LEN, 2 * XA_WIDTH)

    h = norm_matmul(x.reshape(tokens, d), ev_norm[0], ev_w_in.reshape(ev_w_in.shape[1:]), BF16,
                    tm=2048, tn=1024, name="in_proj_even")
    h3 = h.reshape(bsz, seq, -1)
    nb = WIDTH // HEAD_DIM
    ya = hgrn2_mixer(h3, lb_all[0], ev_hgrn_gain[0], 0, nb, 2 * nb, 6 * nb)
    yb = moba_mixer(h3, moba_cos, moba_lo, moba_hi, 3 * nb, 4 * nb, 5 * nb, 7 * nb)
    x1 = post_mixer(x, ya, yb, ev_w_out[0], xa_norm[0], xa_wq[0], memory_kv(0), xa_wo[0],
                    final_norm, final=False)

    h = norm_matmul(x1.reshape(tokens, d), od_norm[0], od_w_in.reshape(od_w_in.shape[1:]), BF16,
                    tm=2048, tn=1024, name="in_proj_odd")
    h3 = h.reshape(bsz, seq, -1)
    yc = rglru_mixer(h3, od_conv_w[0], od_conv_b[0], od_rg_wa[0], od_rg_ba[0], od_rg_wx[0],
                     od_rg_bx[0], od_rg_lambda[0], 0, 4 * nb)
    yd = retention_mixer(h3, ret_cos, ret_sin, log_g, od_ret_gain[0], od_ret_bias[0],
                         nb, 2 * nb, 3 * nb, 5 * nb)
    return post_mixer(x1, yc, yd, od_w_out[0], xa_norm[1], xa_wq[1], memory_kv(1), xa_wo[1],
                      final_norm, final=True)
```

```python
import functools
import math

import jax
import jax.numpy as jnp
from jax import lax
from jax.experimental import pallas as pl
from jax.experimental.pallas import tpu as pltpu

F32 = jnp.float32
BF16 = jnp.bfloat16

D_MODEL = 1024
HEAD_DIM = 128
N_HEADS = 8
WIDTH = N_HEADS * HEAD_DIM
MEM_LEN = 256
HGRN_CHUNK = 32
HGRN_TILE = 128
HGRN_PARTS = 16
MOBA_BLOCK = 256
MOBA_TOPK = 3
RET_CHUNK = 256
RET_PARTS = 2
CONV_WIDTH = 4
RG_C = 8.0
ROPE_THETA = 500000.0
ROPE_DIM = HEAD_DIM // 4
RET_THETA = 10000.0
POST_PARTS = 4
XA_HEADS = 4
XA_WIDTH = XA_HEADS * HEAD_DIM
EPS = 1e-6
VMEM_LIMIT = 56 * 1024 * 1024
MASKED = -1e30

_NT = (((1,), (1,)), ((), ()))


def _dot(a, b):
    return jnp.dot(a, b, preferred_element_type=F32)


def _dot_nt(a, b):
    return lax.dot_general(a, b, _NT, preferred_element_type=F32)


def _sigmoid(x):
    return 1.0 / (1.0 + jnp.exp(-x))


def _silu(x):
    return x * _sigmoid(x)


def _run_staggered(parts, n_stages):
    for step in range(n_stages + len(parts) - 1):
        for p, gen in enumerate(parts):
            if 0 <= step - p < n_stages:
                next(gen)


def _params(*sem):
    return pltpu.CompilerParams(dimension_semantics=sem, vmem_limit_bytes=VMEM_LIMIT)


def _norm_matmul_kernel(x_ref, g_ref, w_ref, o_ref, xn_ref):
    @pl.when(pl.program_id(1) == 0)
    def _():
        x = x_ref[...]
        ms = jnp.mean(x * x, axis=-1, keepdims=True)
        xn_ref[...] = (x * lax.rsqrt(ms + EPS) * g_ref[...]).astype(BF16)

    o_ref[...] = _dot(xn_ref[...], w_ref[...].astype(BF16)).astype(o_ref.dtype)


def norm_matmul(x, gain, w, out_dtype, tm, tn, name):
    t, d = x.shape
    n = w.shape[1]
    return pl.pallas_call(
        _norm_matmul_kernel,
        out_shape=jax.ShapeDtypeStruct((t, n), out_dtype),
        grid=(t // tm, n // tn),
        in_specs=[
            pl.BlockSpec((tm, d), lambda i, j: (i, 0)),
            pl.BlockSpec((1, d), lambda i, j: (0, 0)),
            pl.BlockSpec((d, tn), lambda i, j: (0, j)),
        ],
        out_specs=pl.BlockSpec((tm, tn), lambda i, j: (i, j)),
        scratch_shapes=[pltpu.VMEM((tm, d), BF16)],
        compiler_params=_params("arbitrary", "arbitrary"),
        name=name,
    )(x, gain.reshape(1, d), w)


def _split2(x):
    hi = x.astype(BF16)
    lo = (x - hi.astype(F32)).astype(BF16)
    return hi, lo


def _hgrn2_kernel(q_ref, f_ref, i_ref, z_ref, lb_ref, gain_ref, o_ref, st_ref):
    seq = q_ref.shape[0]
    tile, chunk = HGRN_TILE, HGRN_CHUNK
    per_tile = tile // chunk
    part_rows = seq // HGRN_PARTS
    nt = part_rows // tile
    lb = lb_ref[...]
    gain = gain_ref[...]

    def tiles(x):
        return x.reshape(nt, tile, HEAD_DIM)

    def bmm(spec, a, b):
        return jnp.einsum(spec, a, b, preferred_element_type=F32)

    ti = lax.broadcasted_iota(jnp.int32, (tile, tile), 0)
    tj = lax.broadcasted_iota(jnp.int32, (tile, tile), 1)
    intra_mask = (((ti // chunk) == (tj // chunk)) & (tj <= ti))[None]
    sum_mat = jnp.broadcast_to(jnp.where(intra_mask, 1.0, 0.0).astype(BF16), (nt, tile, tile))
    row_chunk = (lax.broadcasted_iota(jnp.int32, (tile, HEAD_DIM), 0) // chunk)[None]

    def per_chunk_lanes(x):
        zero = jnp.zeros_like(x)
        return jnp.concatenate([jnp.where(row_chunk == ci, x, zero) for ci in range(per_tile)], axis=-1)

    carried = [jnp.zeros((HEAD_DIM, HEAD_DIM), F32)]

    def part(p):
        rows = slice(p * part_rows, (p + 1) * part_rows)
        f = lb + (1.0 - lb) * _sigmoid(f_ref[rows, :].astype(F32))
        k = tiles(1.0 - f)
        hi, lo = _split2(jnp.log(f))
        yield
        b = bmm('trc,tcd->trd', sum_mat, tiles(hi)) + bmm('trc,tcd->trd', sum_mat, tiles(lo))
        b_chunks = b.reshape(part_rows // chunk, chunk, HEAD_DIM)
        b_tot = jnp.broadcast_to(b_chunks[:, chunk - 1:, :], b_chunks.shape).reshape(b.shape)
        yield
        q_dec = (tiles(q_ref[rows, :].astype(F32)) * jnp.exp(b)).astype(BF16)
        k_inv = (k * jnp.exp(-b)).astype(BF16)
        k_end = (k * jnp.exp(b_tot - b)).astype(BF16)
        v = tiles(i_ref[rows, :].astype(F32))
        yield
        att = bmm('tid,tjd->tij', q_dec, k_inv)
        att = jnp.where(intra_mask, att, 0.0).astype(BF16)
        v_t = jnp.swapaxes(v, 1, 2).astype(BF16)
        incr = bmm('tvn,tnk->tvk', v_t, per_chunk_lanes(k_end))
        yield
        o_intra = bmm('tij,tjd->tid', att, v.astype(BF16))
        state_t = carried[0]
        for t in range(nt):
            for ci in range(per_tile):
                lanes = slice(ci * HEAD_DIM, (ci + 1) * HEAD_DIM)
                st_ref[p * nt + t, :, lanes] = state_t.astype(BF16)
                decay = jnp.exp(b_tot[t, ci * chunk:ci * chunk + 1, :])
                state_t = state_t * decay + incr[t, :, lanes]
        carried[0] = state_t
        yield
        o = o_intra + bmm('tnk,tvk->tnv', per_chunk_lanes(q_dec), st_ref[p * nt:(p + 1) * nt])
        o = o.reshape(part_rows, HEAD_DIM)
        y = o * lax.rsqrt(jnp.mean(o * o, axis=-1, keepdims=True) + EPS) * gain
        o_ref[rows, :] = (y * _silu(z_ref[rows, :].astype(F32))).astype(o_ref.dtype)
        yield

    _run_staggered([part(p) for p in range(HGRN_PARTS)], n_stages=6)


def hgrn2_mixer(h3, lb, gain, col_q, col_f, col_i, col_z):
    bsz, seq, _ = h3.shape

    def col(off):
        return pl.BlockSpec((None, seq, HEAD_DIM), lambda b, h: (b, 0, off + h))

    vec = pl.BlockSpec((None, 1, HEAD_DIM), lambda b, h: (h, 0, 0))
    return pl.pallas_call(
        _hgrn2_kernel,
        out_shape=jax.ShapeDtypeStruct((bsz, seq, WIDTH), BF16),
        grid=(bsz, N_HEADS),
        in_specs=[col(col_q), col(col_f), col(col_i), col(col_z), vec, vec],
        out_specs=pl.BlockSpec((None, seq, HEAD_DIM), lambda b, h: (b, 0, h)),
        scratch_shapes=[pltpu.VMEM((seq // HGRN_TILE, HEAD_DIM, HGRN_TILE // HGRN_CHUNK * HEAD_DIM), BF16)],
        compiler_params=_params("arbitrary", "arbitrary"),
        name="hgrn2",
    )(h3, h3, h3, h3, lb.reshape(N_HEADS, 1, HEAD_DIM), gain.reshape(N_HEADS, 1, HEAD_DIM))


RET_HALF = HEAD_DIM // 2
MOBA_HALF = ROPE_DIM // 2


def _rope_table_kernel(pos_ref, inv_ref, ret_cos_ref, ret_sin_ref, moba_cos_ref, moba_lo_ref, moba_hi_ref):
    ang = pos_ref[...] * inv_ref[...]
    cos = jnp.cos(ang)
    sin = jnp.sin(ang)
    lane = lax.broadcasted_iota(jnp.int32, ang.shape, 1)
    low = lane < RET_HALF
    cos_up = pltpu.roll(cos, RET_HALF, 1)
    sin_up = pltpu.roll(sin, RET_HALF, 1)
    ret_cos_ref[...] = jnp.where(low, cos, cos_up)
    ret_sin_ref[...] = jnp.where(low, -sin, sin_up)
    first = lane < MOBA_HALF
    second = (lane >= MOBA_HALF) & (lane < 2 * MOBA_HALF)
    moba_cos_ref[...] = jnp.where(first, cos_up, jnp.where(second, pltpu.roll(cos_up, MOBA_HALF, 1), 1.0))
    moba_lo_ref[...] = jnp.where(first, -sin_up, 0.0)
    moba_hi_ref[...] = jnp.where(second, pltpu.roll(sin_up, MOBA_HALF, 1), 0.0)


def rope_tables(positions):
    bsz, seq = positions.shape
    inv_ret = RET_THETA ** (-jnp.arange(RET_HALF, dtype=F32) / RET_HALF)
    inv_moba = ROPE_THETA ** (-jnp.arange(MOBA_HALF, dtype=F32) / MOBA_HALF)
    inv = jnp.concatenate([inv_ret, inv_moba, jnp.zeros((HEAD_DIM - RET_HALF - MOBA_HALF,), F32)])
    posf = jnp.broadcast_to(positions.astype(F32)[:, :, None], (bsz, seq, HEAD_DIM))
    spec = pl.BlockSpec((None, seq, HEAD_DIM), lambda b: (b, 0, 0))
    return pl.pallas_call(
        _rope_table_kernel,
        out_shape=[jax.ShapeDtypeStruct((bsz, seq, HEAD_DIM), F32)] * 5,
        grid=(bsz,),
        in_specs=[spec, pl.BlockSpec((1, HEAD_DIM), lambda b: (0, 0))],
        out_specs=[spec] * 5,
        compiler_params=_params("arbitrary"),
        name="rope_tables",
    )(posf, inv.reshape(1, HEAD_DIM))


def _moba_kernel(q_ref, k_ref, v_ref, z_ref, cos_ref, sin_lo_ref, sin_hi_ref, o_ref,
                 qr_ref, ka_ref, va_ref):
    seq = q_ref.shape[0]
    blk = MOBA_BLOCK
    nb = seq // blk
    exp2_scale = HEAD_DIM ** -0.5 * math.log2(math.e)

    def rope(x):
        return (x * cos_ref[...]
                + pltpu.roll(x, HEAD_DIM - MOBA_HALF, 1) * sin_lo_ref[...]
                + pltpu.roll(x, MOBA_HALF, 1) * sin_hi_ref[...])

    kr = rope(k_ref[...].astype(F32))
    k_mean = jnp.mean(kr.reshape(nb, blk, HEAD_DIM), axis=1).astype(BF16)
    ka_ref[:, :HEAD_DIM] = kr.astype(BF16)
    key_blk = lax.broadcasted_iota(jnp.int32, (seq, HEAD_DIM), 0) // blk
    key_lane = lax.broadcasted_iota(jnp.int32, (seq, HEAD_DIM), 1)
    ka_ref[:, HEAD_DIM:] = jnp.where(key_blk == key_lane, MASKED, 0.0).astype(BF16)
    qr_ref[...] = (rope(q_ref[...].astype(F32)) * exp2_scale).astype(BF16)
    va_ref[:, :HEAD_DIM] = v_ref[...]
    va_ref[:, HEAD_DIM:] = jnp.ones((seq, HEAD_DIM), BF16)
    gate_all = _dot_nt(k_mean, qr_ref[...])

    row = lax.broadcasted_iota(jnp.int32, (blk, blk), 0)
    colm = lax.broadcasted_iota(jnp.int32, (blk, blk), 1)
    causal = colm <= row
    eye = jnp.where(colm == row, 1.0, 0.0).astype(BF16)
    blk_row = lax.broadcasted_iota(jnp.int32, (nb, blk), 0)

    def scores(qb):
        rows = slice(qb * blk, (qb + 1) * blk)
        q = qr_ref[rows, :]
        n_keys = (qb + 1) * blk
        if qb > MOBA_TOPK:
            gate = gate_all[:, rows]
            rank = jnp.zeros((nb, blk), F32)
            for j in range(qb):
                gj = gate[j:j + 1, :]
                ahead = (gj > gate) | ((gj == gate) & (blk_row > j))
                rank = rank + jnp.where(ahead, 1.0, 0.0)
            drop = jnp.where((rank >= float(MOBA_TOPK)) & (blk_row < qb), 1.0, 0.0)
            drop = jnp.concatenate([drop, jnp.zeros((HEAD_DIM - nb, blk), F32)], axis=0).astype(BF16)
            drop_col = _dot_nt(eye, drop).astype(BF16)
            s = _dot_nt(jnp.concatenate([q, drop_col], axis=1), ka_ref[:n_keys, :])
        else:
            s = _dot_nt(q, ka_ref[:n_keys, :HEAD_DIM])
        own = jnp.where(causal, s[:, qb * blk:], MASKED)
        return jnp.concatenate([s[:, :qb * blk], own], axis=1) if qb else own

    order = [b for pair in zip(range(nb // 2), range(nb - 1, nb // 2 - 1, -1)) for b in pair]
    s_next = scores(order[0])
    for pos, qb in enumerate(order):
        rows = slice(qb * blk, (qb + 1) * blk)
        s = s_next
        if pos + 1 < nb:
            s_next = scores(order[pos + 1])
        m = jnp.max(s, axis=-1, keepdims=True)
        p = jnp.exp2(s - m).astype(BF16)
        o = _dot(p, va_ref[:(qb + 1) * blk, :])
        o = o[:, :HEAD_DIM] / o[:, HEAD_DIM:]
        o_ref[rows, :] = (o * _silu(z_ref[rows, :].astype(F32))).astype(o_ref.dtype)


def moba_mixer(h3, cos, sin_lo, sin_hi, col_q, col_k, col_v, col_z):
    bsz, seq, _ = h3.shape

    def col(off):
        return pl.BlockSpec((None, seq, HEAD_DIM), lambda b, h: (b, 0, off + h))

    table = pl.BlockSpec((None, seq, HEAD_DIM), lambda b, h: (b, 0, 0))
    return pl.pallas_call(
        _moba_kernel,
        out_shape=jax.ShapeDtypeStruct((bsz, seq, WIDTH), BF16),
        grid=(bsz, N_HEADS),
        in_specs=[col(col_q), col(col_k), col(col_v), col(col_z), table, table, table],
        out_specs=pl.BlockSpec((None, seq, HEAD_DIM), lambda b, h: (b, 0, h)),
        scratch_shapes=[pltpu.VMEM((seq, HEAD_DIM), BF16), pltpu.VMEM((seq, 2 * HEAD_DIM), BF16),
                        pltpu.VMEM((seq, 2 * HEAD_DIM), BF16)],
        compiler_params=_params("arbitrary", "arbitrary"),
        name="moba",
    )(h3, h3, h3, h3, cos, sin_lo, sin_hi)


def _rglru_kernel(x_ref, z_ref, cw_ref, cb_ref, wa_ref, ba_ref, wx_ref, bx_ref, lam_ref,
                  o_ref, a_ref, u_ref, xpad_ref):
    seq, width = x_ref.shape
    sub = 8
    groups = seq // sub

    xpad_ref[:sub, :] = jnp.zeros((sub, width), F32)
    xpad_ref[sub:, :] = x_ref[...].astype(F32)
    xf = cb_ref[...]
    for d in range(CONV_WIDTH):
        xf = xf + xpad_ref[sub - d:sub - d + seq, :] * cw_ref[CONV_WIDTH - 1 - d:CONV_WIDTH - d, :]

    xb = xf.astype(BF16)

    def gate(w_ref, b_ref):
        pre = [_dot(xb[:, j * HEAD_DIM:(j + 1) * HEAD_DIM], w_ref[j]) for j in range(width // HEAD_DIM)]
        return _sigmoid(jnp.concatenate(pre, axis=1) + b_ref[...])

    r = gate(wa_ref, ba_ref)
    ig = gate(wx_ref, bx_ref)
    nl = -lam_ref[...]
    softplus = jnp.maximum(nl, 0.0) + jnp.log1p(jnp.exp(-jnp.abs(nl)))
    a = jnp.exp(r * ((-RG_C) * softplus))
    u = jnp.sqrt(1.0 - a * a) * ig * xf

    a3 = a.reshape(groups, sub, width)
    u3 = u.reshape(groups, sub, width)
    s_idx = lax.broadcasted_iota(jnp.int32, a3.shape, 1)
    for d in (1, 2, 4):
        keep = s_idx >= d
        a_prev = jnp.where(keep, pltpu.roll(a3, d, 1), 1.0)
        u_prev = jnp.where(keep, pltpu.roll(u3, d, 1), 0.0)
        u3 = u3 + a3 * u_prev
        a3 = a3 * a_prev
    a_ref[...] = a3
    u_ref[...] = u3

    def group_body(g, carry):
        hg = a_ref[g] * carry + u_ref[g]
        u_ref[g] = hg
        return jnp.broadcast_to(hg[sub - 1:sub, :], (sub, width))

    lax.fori_loop(0, groups, group_body, jnp.zeros((sub, width), F32), unroll=8)
    hs = u_ref[...].reshape(seq, width)
    o_ref[...] = (hs * _silu(z_ref[...].astype(F32))).astype(o_ref.dtype)


def rglru_mixer(h3, conv_w, conv_b, wa, ba, wx, bx, lam, col_x, col_z, blocks_per_step=2):
    bsz, seq, _ = h3.shape
    width = blocks_per_step * HEAD_DIM

    def col(off):
        return pl.BlockSpec((None, seq, width), lambda b, c: (b, 0, off // blocks_per_step + c))

    def vec(rows):
        return pl.BlockSpec((rows, width), lambda b, c: (0, c))

    wspec = pl.BlockSpec((blocks_per_step, HEAD_DIM, HEAD_DIM), lambda b, c: (c, 0, 0))
    return pl.pallas_call(
        _rglru_kernel,
        out_shape=jax.ShapeDtypeStruct((bsz, seq, WIDTH), BF16),
        grid=(bsz, WIDTH // width),
        in_specs=[col(col_x), col(col_z), vec(CONV_WIDTH), vec(1), wspec, vec(1), wspec, vec(1), vec(1)],
        out_specs=pl.BlockSpec((None, seq, width), lambda b, c: (b, 0, c)),
        scratch_shapes=[pltpu.VMEM((seq // 8, 8, width), F32)] * 2 + [pltpu.VMEM((seq + 8, width), F32)],
        compiler_params=_params("arbitrary", "arbitrary"),
        name="rglru",
    )(h3, h3, conv_w, conv_b.reshape(1, WIDTH), wa.astype(BF16), ba.reshape(1, WIDTH),
      wx.astype(BF16), bx.reshape(1, WIDTH), lam.reshape(1, WIDTH))


def _retention_kernel(q_ref, k_ref, v_ref, z_ref, cos_ref, sin_ref, logg_ref, gain_ref, bias_ref,
                      o_ref, st_ref):
    seq = q_ref.shape[0]
    ck = RET_CHUNK
    half = RET_HALF
    nc = seq // ck // RET_PARTS

    def chunks(x):
        return x.reshape(nc, ck, HEAD_DIM)

    def bmm(spec, a, b):
        return jnp.einsum(spec, a, b, preferred_element_type=F32)

    log_g = logg_ref[...]
    ri = lax.broadcasted_iota(jnp.int32, (ck, ck), 0)
    ci = lax.broadcasted_iota(jnp.int32, (ck, ck), 1)
    diff = (ri - ci).astype(F32)
    dmask = jnp.where(ri >= ci, jnp.exp(jnp.maximum(diff, 0.0) * log_g[:, :1]), 0.0)
    idx = lax.broadcasted_iota(jnp.int32, (ck, HEAD_DIM), 0).astype(F32)
    q_fac = jnp.exp((idx + 1.0) * log_g)
    k_fac = jnp.exp((ck - 1.0 - idx) * log_g)
    chunk_decay = jnp.exp(float(ck) * log_g)

    carried = [jnp.zeros((HEAD_DIM, HEAD_DIM), F32)]

    def part(p):
        rows = slice(p * nc * ck, (p + 1) * nc * ck)
        cos = cos_ref[rows, :]
        sin = sin_ref[rows, :]
        q = q_ref[rows, :].astype(F32)
        k = k_ref[rows, :].astype(F32)
        q = chunks(q * cos + pltpu.roll(q, half, 1) * sin)
        k = chunks((k * cos + pltpu.roll(k, half, 1) * sin) * (HEAD_DIM ** -0.5))
        v = chunks(v_ref[rows, :])
        yield
        att = bmm('cid,cjd->cij', q.astype(BF16), k.astype(BF16)) * dmask[None]
        k_end_t = jnp.swapaxes(k * k_fac[None], 1, 2).astype(BF16)
        incr = bmm('ckn,cnv->ckv', k_end_t, v)
        yield
        o_intra = bmm('cij,cjd->cid', att.astype(BF16), v)
        state = carried[0]
        for c in range(nc):
            st_ref[p * nc + c] = state.astype(BF16)
            state = state * chunk_decay + incr[c]
        carried[0] = state
        yield
        o = o_intra + bmm('cnk,ckv->cnv', (q * q_fac[None]).astype(BF16), st_ref[p * nc:(p + 1) * nc])
        o = o.reshape(nc * ck, HEAD_DIM)
        mu = jnp.mean(o, axis=-1, keepdims=True)
        oc = o - mu
        var = jnp.mean(oc * oc, axis=-1, keepdims=True)
        y = oc * lax.rsqrt(var + EPS) * gain_ref[...] + bias_ref[...]
        o_ref[rows, :] = (y * _silu(z_ref[rows, :].astype(F32))).astype(o_ref.dtype)
        yield

    _run_staggered([part(p) for p in range(RET_PARTS)], n_stages=4)


def retention_mixer(h3, cos, sin, log_g, gain, bias, col_q, col_k, col_v, col_z):
    bsz, seq, _ = h3.shape

    def col(off):
        return pl.BlockSpec((None, seq, HEAD_DIM), lambda b, h: (b, 0, off + h))

    table = pl.BlockSpec((None, seq, HEAD_DIM), lambda b, h: (b, 0, 0))
    vec = pl.BlockSpec((None, 1, HEAD_DIM), lambda b, h: (h, 0, 0))
    return pl.pallas_call(
        _retention_kernel,
        out_shape=jax.ShapeDtypeStruct((bsz, seq, WIDTH), BF16),
        grid=(bsz, N_HEADS),
        in_specs=[col(col_q), col(col_k), col(col_v), col(col_z), table, table, vec, vec, vec],
        out_specs=pl.BlockSpec((None, seq, HEAD_DIM), lambda b, h: (b, 0, h)),
        scratch_shapes=[pltpu.VMEM((seq // RET_CHUNK, HEAD_DIM, HEAD_DIM), BF16)],
        compiler_params=_params("arbitrary", "arbitrary"),
        name="retention",
    )(h3, h3, h3, h3, cos, sin, log_g,
      gain.reshape(N_HEADS, 1, HEAD_DIM), bias.reshape(N_HEADS, 1, HEAD_DIM))


def _post_kernel(x_ref, ya_ref, yb_ref, woa_ref, wob_ref, g_ref, wq_ref, kv_ref, wo_ref, fg_ref,
                 o_ref, *, final):
    part_rows = x_ref.shape[0] // POST_PARTS
    scale = HEAD_DIM ** -0.5

    def part(p):
        rows = slice(p * part_rows, (p + 1) * part_rows)
        x1 = x_ref[rows, :] + _dot(ya_ref[rows, :], woa_ref[...]) + _dot(yb_ref[rows, :], wob_ref[...])
        yield
        ms = jnp.mean(x1 * x1, axis=-1, keepdims=True)
        xn = (x1 * lax.rsqrt(ms + EPS) * g_ref[...]).astype(BF16)
        q = _dot(xn, wq_ref[...]).astype(BF16)
        yield
        head_cols = [slice(hd * HEAD_DIM, (hd + 1) * HEAD_DIM) for hd in range(XA_HEADS)]
        scores = [_dot_nt(q[:, cs], kv_ref[:, cs]) * scale for cs in head_cols]
        yield
        probs = []
        for s in scores:
            pr = jnp.exp(s - jnp.max(s, axis=-1, keepdims=True))
            probs.append((pr / jnp.sum(pr, axis=-1, keepdims=True)).astype(BF16))
        yield
        o = jnp.concatenate([_dot(pr, kv_ref[:, XA_WIDTH + cs.start:XA_WIDTH + cs.stop]).astype(BF16)
                             for pr, cs in zip(probs, head_cols)], axis=1)
        yield
        x2 = x1 + _dot(o, wo_ref[...])
        if final:
            ms2 = jnp.mean(x2 * x2, axis=-1, keepdims=True)
            x2 = x2 * lax.rsqrt(ms2 + EPS) * fg_ref[...]
        o_ref[rows, :] = x2
        yield

    _run_staggered([part(p) for p in range(POST_PARTS)], n_stages=6)


def post_mixer(x3, ya, yb, w_out, xa_gain, wq, kv, wo, final_gain, final, tm=1024):
    bsz, seq, d = x3.shape
    full = lambda shape: pl.BlockSpec(shape, lambda b, i: (0,) * len(shape))
    tile = lambda w: pl.BlockSpec((None, tm, w), lambda b, i: (b, i, 0))
    return pl.pallas_call(
        functools.partial(_post_kernel, final=final),
        out_shape=jax.ShapeDtypeStruct((bsz, seq, d), F32),
        grid=(bsz, seq // tm),
        in_specs=[tile(d), tile(WIDTH), tile(WIDTH),
                  full((WIDTH, d)), full((WIDTH, d)), full((1, d)), full((d, XA_WIDTH)),
                  pl.BlockSpec((None, MEM_LEN, 2 * XA_WIDTH), lambda b, i: (b, 0, 0)),
                  full((XA_WIDTH, d)), full((1, d))],
        out_specs=tile(d),
        compiler_params=_params("arbitrary", "arbitrary"),
        name="post_final" if final else "post",
    )(x3, ya, yb, w_out[:WIDTH].astype(BF16), w_out[WIDTH:].astype(BF16), xa_gain.reshape(1, d),
      wq.astype(BF16), kv, wo.astype(BF16), final_gain.reshape(1, d))


def kernel(x, mem, positions, hgrn_lb_logits, ev_norm, ev_w_in, ev_hgrn_gain, ev_w_out, od_norm, od_w_in, od_conv_w, od_conv_b, od_rg_wa, od_rg_ba, od_rg_wx, od_rg_bx, od_rg_lambda, od_ret_gain, od_ret_bias, od_w_out, xa_norm, xa_mem_norm, xa_wq, xa_wkv, xa_wo, final_norm):
    bsz, seq, d = x.shape
    tokens = bsz * seq
    lb_all = jnp.cumsum(jax.nn.softmax(hgrn_lb_logits.astype(F32), axis=0), axis=0)
    ret_cos, ret_sin, moba_cos, moba_lo, moba_hi = rope_tables(positions)
    log_g = jnp.log(1.0 - 2.0 ** (-5.0 - jnp.arange(N_HEADS, dtype=F32)))
    log_g = jnp.broadcast_to(log_g[:, None, None], (N_HEADS, 1, HEAD_DIM))
    mem2 = mem.reshape(bsz * MEM_LEN, d)

    def memory_kv(layer):
        kv = norm_matmul(mem2, xa_mem_norm[layer], xa_wkv[layer], BF16,
                         tm=bsz * MEM_LEN // 2, tn=2 * XA_WIDTH, name="mem_kv")
        return kv.reshape(bsz, MEM_LEN, 2 * XA_WIDTH)

    h = norm_matmul(x.reshape(tokens, d), ev_norm[0], ev_w_in.reshape(ev_w_in.shape[1:]), BF16,
                    tm=2048, tn=1024, name="in_proj_even")
    h3 = h.reshape(bsz, seq, -1)
    nb = WIDTH // HEAD_DIM
    ya = hgrn2_mixer(h3, lb_all[0], ev_hgrn_gain[0], 0, nb, 2 * nb, 6 * nb)
    yb = moba_mixer(h3, moba_cos, moba_lo, moba_hi, 3 * nb, 4 * nb, 5 * nb, 7 * nb)
    x1 = post_mixer(x, ya, yb, ev_w_out[0], xa_norm[0], xa_wq[0], memory_kv(0), xa_wo[0],
                    final_norm, final=False)

    h = norm_matmul(x1.reshape(tokens, d), od_norm[0], od_w_in.reshape(od_w_in.shape[1:]), BF16,
                    tm=2048, tn=1024, name="in_proj_odd")
    h3 = h.reshape(bsz, seq, -1)
    yc = rglru_mixer(h3, od_conv_w[0], od_conv_b[0], od_rg_wa[0], od_rg_ba[0], od_rg_wx[0],
                     od_rg_bx[0], od_rg_lambda[0], 0, 4 * nb)
    yd = retention_mixer(h3, ret_cos, ret_sin, log_g, od_ret_gain[0], od_ret_bias[0],
                         nb, 2 * nb, 3 * nb, 5 * nb)
    return post_mixer(x1, yc, yd, od_w_out[0], xa_norm[1], xa_wq[1], memory_kv(1), xa_wo[1],
                      final_norm, final=True)
```

```python
import functools
import math

import jax
import jax.numpy as jnp
from jax import lax
from jax.experimental import pallas as pl
from jax.experimental.pallas import tpu as pltpu

F32 = jnp.float32
BF16 = jnp.bfloat16

D_MODEL = 1024
HEAD_DIM = 128
N_HEADS = 8
WIDTH = N_HEADS * HEAD_DIM
MEM_LEN = 256
HGRN_CHUNK = 32
HGRN_TILE = 128
HGRN_PARTS = 16
MOBA_BLOCK = 256
MOBA_TOPK = 3
RET_CHUNK = 256
RET_PARTS = 2
CONV_WIDTH = 4
RG_C = 8.0
ROPE_THETA = 500000.0
ROPE_DIM = HEAD_DIM // 4
RET_THETA = 10000.0
POST_PARTS = 4
XA_HEADS = 4
XA_WIDTH = XA_HEADS * HEAD_DIM
EPS = 1e-6
VMEM_LIMIT = 56 * 1024 * 1024
MASKED = -1e30

_NT = (((1,), (1,)), ((), ()))


def _dot(a, b):
    return jnp.dot(a, b, preferred_element_type=F32)


def _dot_nt(a, b):
    return lax.dot_general(a, b, _NT, preferred_element_type=F32)


LOG2E = math.log2(math.e)


def _sigmoid(x):
    return 1.0 / (1.0 + jnp.exp2(x * (-LOG2E)))


def _silu(x):
    return x * _sigmoid(x)


def _run_staggered(parts, n_stages):
    for step in range(n_stages + len(parts) - 1):
        for p, gen in enumerate(parts):
            if 0 <= step - p < n_stages:
                next(gen)


def _params(*sem):
    return pltpu.CompilerParams(dimension_semantics=sem, vmem_limit_bytes=VMEM_LIMIT)


def _norm_matmul_kernel(x_ref, g_ref, w_ref, o_ref, xn_ref):
    @pl.when(pl.program_id(1) == 0)
    def _():
        x = x_ref[...]
        ms = jnp.mean(x * x, axis=-1, keepdims=True)
        xn_ref[...] = (x * lax.rsqrt(ms + EPS) * g_ref[...]).astype(BF16)

    o_ref[...] = _dot(xn_ref[...], w_ref[...].astype(BF16)).astype(o_ref.dtype)


def norm_matmul(x, gain, w, out_dtype, tm, tn, name):
    t, d = x.shape
    n = w.shape[1]
    return pl.pallas_call(
        _norm_matmul_kernel,
        out_shape=jax.ShapeDtypeStruct((t, n), out_dtype),
        grid=(t // tm, n // tn),
        in_specs=[
            pl.BlockSpec((tm, d), lambda i, j: (i, 0)),
            pl.BlockSpec((1, d), lambda i, j: (0, 0)),
            pl.BlockSpec((d, tn), lambda i, j: (0, j)),
        ],
        out_specs=pl.BlockSpec((tm, tn), lambda i, j: (i, j)),
        scratch_shapes=[pltpu.VMEM((tm, d), BF16)],
        compiler_params=_params("arbitrary", "arbitrary"),
        name=name,
    )(x, gain.reshape(1, d), w)


def _split2(x):
    hi = x.astype(BF16)
    lo = (x - hi.astype(F32)).astype(BF16)
    return hi, lo


def _hgrn2_kernel(q_ref, f_ref, i_ref, z_ref, lb_ref, gain_ref, o_ref, st_ref):
    seq = q_ref.shape[0]
    tile, chunk = HGRN_TILE, HGRN_CHUNK
    per_tile = tile // chunk
    part_rows = seq // HGRN_PARTS
    nt = part_rows // tile
    lb = lb_ref[...]
    gain = gain_ref[...]

    def tiles(x):
        return x.reshape(nt, tile, HEAD_DIM)

    def bmm(spec, a, b):
        return jnp.einsum(spec, a, b, preferred_element_type=F32)

    ti = lax.broadcasted_iota(jnp.int32, (tile, tile), 0)
    tj = lax.broadcasted_iota(jnp.int32, (tile, tile), 1)
    intra_mask = (((ti // chunk) == (tj // chunk)) & (tj <= ti))[None]
    sum_mat = jnp.broadcast_to(jnp.where(intra_mask, 1.0, 0.0).astype(BF16), (nt, tile, tile))
    row_chunk = (lax.broadcasted_iota(jnp.int32, (tile, HEAD_DIM), 0) // chunk)[None]

    def per_chunk_lanes(x):
        zero = jnp.zeros_like(x)
        return jnp.concatenate([jnp.where(row_chunk == ci, x, zero) for ci in range(per_tile)], axis=-1)

    carried = [jnp.zeros((HEAD_DIM, HEAD_DIM), F32)]

    def part(p):
        rows = slice(p * part_rows, (p + 1) * part_rows)
        f = lb + (1.0 - lb) * _sigmoid(f_ref[rows, :].astype(F32))
        k = tiles(1.0 - f)
        hi, lo = _split2(jnp.log2(f))
        yield
        b = bmm('trc,tcd->trd', sum_mat, tiles(hi)) + bmm('trc,tcd->trd', sum_mat, tiles(lo))
        b_chunks = b.reshape(part_rows // chunk, chunk, HEAD_DIM)
        b_tot = jnp.broadcast_to(b_chunks[:, chunk - 1:, :], b_chunks.shape).reshape(b.shape)
        yield
        q_dec = (tiles(q_ref[rows, :].astype(F32)) * jnp.exp2(b)).astype(BF16)
        k_inv = (k * jnp.exp2(-b)).astype(BF16)
        k_end = (k * jnp.exp2(b_tot - b)).astype(BF16)
        v = tiles(i_ref[rows, :].astype(F32))
        yield
        att = bmm('tid,tjd->tij', q_dec, k_inv)
        att = jnp.where(intra_mask, att, 0.0).astype(BF16)
        v_t = jnp.swapaxes(v, 1, 2).astype(BF16)
        incr = bmm('tvn,tnk->tvk', v_t, per_chunk_lanes(k_end))
        yield
        o_intra = bmm('tij,tjd->tid', att, v.astype(BF16))
        state_t = carried[0]
        for t in range(nt):
            for ci in range(per_tile):
                lanes = slice(ci * HEAD_DIM, (ci + 1) * HEAD_DIM)
                st_ref[p * nt + t, :, lanes] = state_t.astype(BF16)
                decay = jnp.exp2(b_tot[t, ci * chunk:ci * chunk + 1, :])
                state_t = state_t * decay + incr[t, :, lanes]
        carried[0] = state_t
        yield
        o = o_intra + bmm('tnk,tvk->tnv', per_chunk_lanes(q_dec), st_ref[p * nt:(p + 1) * nt])
        o = o.reshape(part_rows, HEAD_DIM)
        y = o * lax.rsqrt(jnp.mean(o * o, axis=-1, keepdims=True) + EPS) * gain
        o_ref[rows, :] = (y * _silu(z_ref[rows, :].astype(F32))).astype(o_ref.dtype)
        yield

    _run_staggered([part(p) for p in range(HGRN_PARTS)], n_stages=6)


def hgrn2_mixer(h3, lb, gain, col_q, col_f, col_i, col_z):
    bsz, seq, _ = h3.shape

    def col(off):
        return pl.BlockSpec((None, seq, HEAD_DIM), lambda b, h: (b, 0, off + h))

    vec = pl.BlockSpec((None, 1, HEAD_DIM), lambda b, h: (h, 0, 0))
    return pl.pallas_call(
        _hgrn2_kernel,
        out_shape=jax.ShapeDtypeStruct((bsz, seq, WIDTH), BF16),
        grid=(bsz, N_HEADS),
        in_specs=[col(col_q), col(col_f), col(col_i), col(col_z), vec, vec],
        out_specs=pl.BlockSpec((None, seq, HEAD_DIM), lambda b, h: (b, 0, h)),
        scratch_shapes=[pltpu.VMEM((seq // HGRN_TILE, HEAD_DIM, HGRN_TILE // HGRN_CHUNK * HEAD_DIM), BF16)],
        compiler_params=_params("arbitrary", "arbitrary"),
        name="hgrn2",
    )(h3, h3, h3, h3, lb.reshape(N_HEADS, 1, HEAD_DIM), gain.reshape(N_HEADS, 1, HEAD_DIM))


RET_HALF = HEAD_DIM // 2
MOBA_HALF = ROPE_DIM // 2


def _rope_table_kernel(pos_ref, inv_ref, ret_cos_ref, ret_sin_ref, moba_cos_ref, moba_lo_ref, moba_hi_ref):
    ang = pos_ref[...] * inv_ref[...]
    cos = jnp.cos(ang)
    sin = jnp.sin(ang)
    lane = lax.broadcasted_iota(jnp.int32, ang.shape, 1)
    low = lane < RET_HALF
    cos_up = pltpu.roll(cos, RET_HALF, 1)
    sin_up = pltpu.roll(sin, RET_HALF, 1)
    ret_cos_ref[...] = jnp.where(low, cos, cos_up)
    ret_sin_ref[...] = jnp.where(low, -sin, sin_up)
    first = lane < MOBA_HALF
    second = (lane >= MOBA_HALF) & (lane < 2 * MOBA_HALF)
    moba_cos_ref[...] = jnp.where(first, cos_up, jnp.where(second, pltpu.roll(cos_up, MOBA_HALF, 1), 1.0))
    moba_lo_ref[...] = jnp.where(first, -sin_up, 0.0)
    moba_hi_ref[...] = jnp.where(second, pltpu.roll(sin_up, MOBA_HALF, 1), 0.0)


def rope_tables(positions):
    bsz, seq = positions.shape
    inv_ret = RET_THETA ** (-jnp.arange(RET_HALF, dtype=F32) / RET_HALF)
    inv_moba = ROPE_THETA ** (-jnp.arange(MOBA_HALF, dtype=F32) / MOBA_HALF)
    inv = jnp.concatenate([inv_ret, inv_moba, jnp.zeros((HEAD_DIM - RET_HALF - MOBA_HALF,), F32)])
    posf = jnp.broadcast_to(positions.astype(F32)[:, :, None], (bsz, seq, HEAD_DIM))
    spec = pl.BlockSpec((None, seq, HEAD_DIM), lambda b: (b, 0, 0))
    return pl.pallas_call(
        _rope_table_kernel,
        out_shape=[jax.ShapeDtypeStruct((bsz, seq, HEAD_DIM), F32)] * 5,
        grid=(bsz,),
        in_specs=[spec, pl.BlockSpec((1, HEAD_DIM), lambda b: (0, 0))],
        out_specs=[spec] * 5,
        compiler_params=_params("arbitrary"),
        name="rope_tables",
    )(posf, inv.reshape(1, HEAD_DIM))


def _moba_kernel(q_ref, k_ref, v_ref, z_ref, cos_ref, sin_lo_ref, sin_hi_ref, o_ref,
                 qr_ref, ka_ref, va_ref, kmean_ref):
    seq = q_ref.shape[0]
    blk = MOBA_BLOCK
    nb = seq // blk
    exp2_scale = HEAD_DIM ** -0.5 * LOG2E

    row = lax.broadcasted_iota(jnp.int32, (blk, blk), 0)
    colm = lax.broadcasted_iota(jnp.int32, (blk, blk), 1)
    causal = colm <= row
    eye = jnp.where(colm == row, 1.0, 0.0).astype(BF16)
    blk_row = lax.broadcasted_iota(jnp.int32, (nb, blk), 0)
    lane = lax.broadcasted_iota(jnp.int32, (blk, HEAD_DIM), 1)
    kmean_ref[...] = jnp.zeros(kmean_ref.shape, kmean_ref.dtype)

    def prepare(qb):
        rows = slice(qb * blk, (qb + 1) * blk)

        def rope(x):
            return (x * cos_ref[rows, :]
                    + pltpu.roll(x, HEAD_DIM - MOBA_HALF, 1) * sin_lo_ref[rows, :]
                    + pltpu.roll(x, MOBA_HALF, 1) * sin_hi_ref[rows, :])

        kr = rope(k_ref[rows, :].astype(F32))
        kmean_ref[qb:qb + 1, :] = jnp.mean(kr, axis=0, keepdims=True)
        ka_ref[rows, :HEAD_DIM] = kr.astype(BF16)
        ka_ref[rows, HEAD_DIM:] = jnp.where(lane == qb, MASKED, 0.0).astype(BF16)
        qr_ref[rows, :] = (rope(q_ref[rows, :].astype(F32)) * exp2_scale).astype(BF16)
        va_ref[rows, :HEAD_DIM] = v_ref[rows, :]
        va_ref[rows, HEAD_DIM:] = jnp.ones((blk, HEAD_DIM), BF16)

    def scores(qb):
        rows = slice(qb * blk, (qb + 1) * blk)
        q = qr_ref[rows, :]
        n_keys = (qb + 1) * blk
        if qb > MOBA_TOPK:
            gate = _dot_nt(kmean_ref[...].astype(BF16), q)
            rank = jnp.zeros((nb, blk), F32)
            for j in range(qb):
                gj = gate[j:j + 1, :]
                ahead = (gj > gate) | ((gj == gate) & (blk_row > j))
                rank = rank + jnp.where(ahead, 1.0, 0.0)
            drop = jnp.where((rank >= float(MOBA_TOPK)) & (blk_row < qb), 1.0, 0.0)
            drop = jnp.concatenate([drop, jnp.zeros((HEAD_DIM - nb, blk), F32)], axis=0).astype(BF16)
            drop_col = _dot_nt(eye, drop).astype(BF16)
            s = _dot_nt(jnp.concatenate([q, drop_col], axis=1), ka_ref[:n_keys, :])
        else:
            s = _dot_nt(q, ka_ref[:n_keys, :HEAD_DIM])
        own = jnp.where(causal, s[:, qb * blk:], MASKED)
        return jnp.concatenate([s[:, :qb * blk], own], axis=1) if qb else own

    prepare(0)
    s_next = scores(0)
    for qb in range(nb):
        rows = slice(qb * blk, (qb + 1) * blk)
        s = s_next
        if qb + 1 < nb:
            prepare(qb + 1)
            s_next = scores(qb + 1)
        m = jnp.max(s, axis=-1, keepdims=True)
        p = jnp.exp2(s - m).astype(BF16)
        o = _dot(p, va_ref[:(qb + 1) * blk, :])
        o = o[:, :HEAD_DIM] / o[:, HEAD_DIM:]
        o_ref[rows, :] = (o * _silu(z_ref[rows, :].astype(F32))).astype(o_ref.dtype)


def moba_mixer(h3, cos, sin_lo, sin_hi, col_q, col_k, col_v, col_z):
    bsz, seq, _ = h3.shape

    def col(off):
        return pl.BlockSpec((None, seq, HEAD_DIM), lambda b, h: (b, 0, off + h))

    table = pl.BlockSpec((None, seq, HEAD_DIM), lambda b, h: (b, 0, 0))
    return pl.pallas_call(
        _moba_kernel,
        out_shape=jax.ShapeDtypeStruct((bsz, seq, WIDTH), BF16),
        grid=(bsz, N_HEADS),
        in_specs=[col(col_q), col(col_k), col(col_v), col(col_z), table, table, table],
        out_specs=pl.BlockSpec((None, seq, HEAD_DIM), lambda b, h: (b, 0, h)),
        scratch_shapes=[pltpu.VMEM((seq, HEAD_DIM), BF16), pltpu.VMEM((seq, 2 * HEAD_DIM), BF16),
                        pltpu.VMEM((seq, 2 * HEAD_DIM), BF16), pltpu.VMEM((seq // MOBA_BLOCK, HEAD_DIM), F32)],
        compiler_params=_params("arbitrary", "arbitrary"),
        name="moba",
    )(h3, h3, h3, h3, cos, sin_lo, sin_hi)


def _rglru_kernel(x_ref, z_ref, cw_ref, cb_ref, wa_ref, ba_ref, wx_ref, bx_ref, lam_ref,
                  o_ref, a_ref, u_ref, xpad_ref):
    seq, width = x_ref.shape
    sub = 8
    groups = seq // sub

    xpad_ref[:sub, :] = jnp.zeros((sub, width), F32)
    xpad_ref[sub:, :] = x_ref[...].astype(F32)
    xf = cb_ref[...]
    for d in range(CONV_WIDTH):
        xf = xf + xpad_ref[sub - d:sub - d + seq, :] * cw_ref[CONV_WIDTH - 1 - d:CONV_WIDTH - d, :]

    xb = xf.astype(BF16)

    def gate(w_ref, b_ref):
        pre = [_dot(xb[:, j * HEAD_DIM:(j + 1) * HEAD_DIM], w_ref[j]) for j in range(width // HEAD_DIM)]
        return _sigmoid(jnp.concatenate(pre, axis=1) + b_ref[...])

    r = gate(wa_ref, ba_ref)
    ig = gate(wx_ref, bx_ref)
    nl = -lam_ref[...]
    softplus = jnp.maximum(nl, 0.0) + jnp.log1p(jnp.exp(-jnp.abs(nl)))
    a = jnp.exp2(r * ((-RG_C * LOG2E) * softplus))
    u = jnp.sqrt(1.0 - a * a) * ig * xf

    a3 = a.reshape(groups, sub, width)
    u3 = u.reshape(groups, sub, width)
    s_idx = lax.broadcasted_iota(jnp.int32, a3.shape, 1)
    for d in (1, 2, 4):
        keep = s_idx >= d
        a_prev = jnp.where(keep, pltpu.roll(a3, d, 1), 1.0)
        u_prev = jnp.where(keep, pltpu.roll(u3, d, 1), 0.0)
        u3 = u3 + a3 * u_prev
        a3 = a3 * a_prev
    a_ref[...] = a3
    u_ref[...] = u3

    def group_body(g, carry):
        hg = a_ref[g] * carry + u_ref[g]
        u_ref[g] = hg
        return jnp.broadcast_to(hg[sub - 1:sub, :], (sub, width))

    lax.fori_loop(0, groups, group_body, jnp.zeros((sub, width), F32), unroll=8)
    hs = u_ref[...].reshape(seq, width)
    o_ref[...] = (hs * _silu(z_ref[...].astype(F32))).astype(o_ref.dtype)


def rglru_mixer(h3, conv_w, conv_b, wa, ba, wx, bx, lam, col_x, col_z, blocks_per_step=2):
    bsz, seq, _ = h3.shape
    width = blocks_per_step * HEAD_DIM

    def col(off):
        return pl.BlockSpec((None, seq, width), lambda b, c: (b, 0, off // blocks_per_step + c))

    def vec(rows):
        return pl.BlockSpec((rows, width), lambda b, c: (0, c))

    wspec = pl.BlockSpec((blocks_per_step, HEAD_DIM, HEAD_DIM), lambda b, c: (c, 0, 0))
    return pl.pallas_call(
        _rglru_kernel,
        out_shape=jax.ShapeDtypeStruct((bsz, seq, WIDTH), BF16),
        grid=(bsz, WIDTH // width),
        in_specs=[col(col_x), col(col_z), vec(CONV_WIDTH), vec(1), wspec, vec(1), wspec, vec(1), vec(1)],
        out_specs=pl.BlockSpec((None, seq, width), lambda b, c: (b, 0, c)),
        scratch_shapes=[pltpu.VMEM((seq // 8, 8, width), F32)] * 2 + [pltpu.VMEM((seq + 8, width), F32)],
        compiler_params=_params("arbitrary", "arbitrary"),
        name="rglru",
    )(h3, h3, conv_w, conv_b.reshape(1, WIDTH), wa.astype(BF16), ba.reshape(1, WIDTH),
      wx.astype(BF16), bx.reshape(1, WIDTH), lam.reshape(1, WIDTH))


def _retention_kernel(q_ref, k_ref, v_ref, z_ref, cos_ref, sin_ref, logg_ref, gain_ref, bias_ref,
                      o_ref, st_ref):
    seq = q_ref.shape[0]
    ck = RET_CHUNK
    half = RET_HALF
    nc = seq // ck // RET_PARTS

    def chunks(x):
        return x.reshape(nc, ck, HEAD_DIM)

    def bmm(spec, a, b):
        return jnp.einsum(spec, a, b, preferred_element_type=F32)

    log_g = logg_ref[...]
    ri = lax.broadcasted_iota(jnp.int32, (ck, ck), 0)
    ci = lax.broadcasted_iota(jnp.int32, (ck, ck), 1)
    diff = (ri - ci).astype(F32)
    dmask = jnp.where(ri >= ci, jnp.exp(jnp.maximum(diff, 0.0) * log_g[:, :1]), 0.0)
    idx = lax.broadcasted_iota(jnp.int32, (ck, HEAD_DIM), 0).astype(F32)
    q_fac = jnp.exp((idx + 1.0) * log_g)
    k_fac = jnp.exp((ck - 1.0 - idx) * log_g)
    chunk_decay = jnp.exp(float(ck) * log_g)

    carried = [jnp.zeros((HEAD_DIM, HEAD_DIM), F32)]

    def part(p):
        rows = slice(p * nc * ck, (p + 1) * nc * ck)
        cos = cos_ref[rows, :]
        sin = sin_ref[rows, :]
        q = q_ref[rows, :].astype(F32)
        k = k_ref[rows, :].astype(F32)
        q = chunks(q * cos + pltpu.roll(q, half, 1) * sin)
        k = chunks((k * cos + pltpu.roll(k, half, 1) * sin) * (HEAD_DIM ** -0.5))
        v = chunks(v_ref[rows, :])
        yield
        att = bmm('cid,cjd->cij', q.astype(BF16), k.astype(BF16)) * dmask[None]
        k_end_t = jnp.swapaxes(k * k_fac[None], 1, 2).astype(BF16)
        incr = bmm('ckn,cnv->ckv', k_end_t, v)
        yield
        o_intra = bmm('cij,cjd->cid', att.astype(BF16), v)
        state = carried[0]
        for c in range(nc):
            st_ref[p * nc + c] = state.astype(BF16)
            state = state * chunk_decay + incr[c]
        carried[0] = state
        yield
        o = o_intra + bmm('cnk,ckv->cnv', (q * q_fac[None]).astype(BF16), st_ref[p * nc:(p + 1) * nc])
        o = o.reshape(nc * ck, HEAD_DIM)
        mu = jnp.mean(o, axis=-1, keepdims=True)
        oc = o - mu
        var = jnp.mean(oc * oc, axis=-1, keepdims=True)
        y = oc * lax.rsqrt(var + EPS) * gain_ref[...] + bias_ref[...]
        o_ref[rows, :] = (y * _silu(z_ref[rows, :].astype(F32))).astype(o_ref.dtype)
        yield

    _run_staggered([part(p) for p in range(RET_PARTS)], n_stages=4)


def retention_mixer(h3, cos, sin, log_g, gain, bias, col_q, col_k, col_v, col_z):
    bsz, seq, _ = h3.shape

    def col(off):
        return pl.BlockSpec((None, seq, HEAD_DIM), lambda b, h: (b, 0, off + h))

    table = pl.BlockSpec((None, seq, HEAD_DIM), lambda b, h: (b, 0, 0))
    vec = pl.BlockSpec((None, 1, HEAD_DIM), lambda b, h: (h, 0, 0))
    return pl.pallas_call(
        _retention_kernel,
        out_shape=jax.ShapeDtypeStruct((bsz, seq, WIDTH), BF16),
        grid=(bsz, N_HEADS),
        in_specs=[col(col_q), col(col_k), col(col_v), col(col_z), table, table, vec, vec, vec],
        out_specs=pl.BlockSpec((None, seq, HEAD_DIM), lambda b, h: (b, 0, h)),
        scratch_shapes=[pltpu.VMEM((seq // RET_CHUNK, HEAD_DIM, HEAD_DIM), BF16)],
        compiler_params=_params("arbitrary", "arbitrary"),
        name="retention",
    )(h3, h3, h3, h3, cos, sin, log_g,
      gain.reshape(N_HEADS, 1, HEAD_DIM), bias.reshape(N_HEADS, 1, HEAD_DIM))


def _post_kernel(x_ref, ya_ref, yb_ref, woa_ref, wob_ref, g_ref, wq_ref, kv_ref, wo_ref, fg_ref,
                 o_ref, *, final):
    part_rows = x_ref.shape[0] // POST_PARTS
    scale = HEAD_DIM ** -0.5

    def part(p):
        rows = slice(p * part_rows, (p + 1) * part_rows)
        x1 = x_ref[rows, :] + _dot(ya_ref[rows, :], woa_ref[...]) + _dot(yb_ref[rows, :], wob_ref[...])
        yield
        ms = jnp.mean(x1 * x1, axis=-1, keepdims=True)
        xn = (x1 * lax.rsqrt(ms + EPS) * g_ref[...]).astype(BF16)
        q = _dot(xn, wq_ref[...]).astype(BF16)
        yield
        head_cols = [slice(hd * HEAD_DIM, (hd + 1) * HEAD_DIM) for hd in range(XA_HEADS)]
        scores = [_dot_nt(q[:, cs], kv_ref[:, cs]) * scale for cs in head_cols]
        yield
        probs = []
        for s in scores:
            pr = jnp.exp(s - jnp.max(s, axis=-1, keepdims=True))
            probs.append((pr / jnp.sum(pr, axis=-1, keepdims=True)).astype(BF16))
        yield
        o = jnp.concatenate([_dot(pr, kv_ref[:, XA_WIDTH + cs.start:XA_WIDTH + cs.stop]).astype(BF16)
                             for pr, cs in zip(probs, head_cols)], axis=1)
        yield
        x2 = x1 + _dot(o, wo_ref[...])
        if final:
            ms2 = jnp.mean(x2 * x2, axis=-1, keepdims=True)
            x2 = x2 * lax.rsqrt(ms2 + EPS) * fg_ref[...]
        o_ref[rows, :] = x2
        yield

    _run_staggered([part(p) for p in range(POST_PARTS)], n_stages=6)


def post_mixer(x3, ya, yb, w_out, xa_gain, wq, kv, wo, final_gain, final, tm=1024):
    bsz, seq, d = x3.shape
    full = lambda shape: pl.BlockSpec(shape, lambda b, i: (0,) * len(shape))
    tile = lambda w: pl.BlockSpec((None, tm, w), lambda b, i: (b, i, 0))
    return pl.pallas_call(
        functools.partial(_post_kernel, final=final),
        out_shape=jax.ShapeDtypeStruct((bsz, seq, d), F32),
        grid=(bsz, seq // tm),
        in_specs=[tile(d), tile(WIDTH), tile(WIDTH),
                  full((WIDTH, d)), full((WIDTH, d)), full((1, d)), full((d, XA_WIDTH)),
                  pl.BlockSpec((None, MEM_LEN, 2 * XA_WIDTH), lambda b, i: (b, 0, 0)),
                  full((XA_WIDTH, d)), full((1, d))],
        out_specs=tile(d),
        compiler_params=_params("arbitrary", "arbitrary"),
        name="post_final" if final else "post",
    )(x3, ya, yb, w_out[:WIDTH].astype(BF16), w_out[WIDTH:].astype(BF16), xa_gain.reshape(1, d),
      wq.astype(BF16), kv, wo.astype(BF16), final_gain.reshape(1, d))


def kernel(x, mem, positions, hgrn_lb_logits, ev_norm, ev_w_in, ev_hgrn_gain, ev_w_out, od_norm, od_w_in, od_conv_w, od_conv_b, od_rg_wa, od_rg_ba, od_rg_wx, od_rg_bx, od_rg_lambda, od_ret_gain, od_ret_bias, od_w_out, xa_norm, xa_mem_norm, xa_wq, xa_wkv, xa_wo, final_norm):
    bsz, seq, d = x.shape
    tokens = bsz * seq
    lb_all = jnp.cumsum(jax.nn.softmax(hgrn_lb_logits.astype(F32), axis=0), axis=0)
    ret_cos, ret_sin, moba_cos, moba_lo, moba_hi = rope_tables(positions)
    log_g = jnp.log(1.0 - 2.0 ** (-5.0 - jnp.arange(N_HEADS, dtype=F32)))
    log_g = jnp.broadcast_to(log_g[:, None, None], (N_HEADS, 1, HEAD_DIM))
    mem2 = mem.reshape(bsz * MEM_LEN, d)

    def memory_kv(layer):
        kv = norm_matmul(mem2, xa_mem_norm[layer], xa_wkv[layer], BF16,
                         tm=bsz * MEM_LEN // 2, tn=2 * XA_WIDTH, name="mem_kv")
        return kv.reshape(bsz, MEM_LEN, 2 * XA_WIDTH)

    h = norm_matmul(x.reshape(tokens, d), ev_norm[0], ev_w_in.reshape(ev_w_in.shape[1:]), BF16,
                    tm=2048, tn=1024, name="in_proj_even")
    h3 = h.reshape(bsz, seq, -1)
    nb = WIDTH // HEAD_DIM
    ya = hgrn2_mixer(h3, lb_all[0], ev_hgrn_gain[0], 0, nb, 2 * nb, 6 * nb)
    yb = moba_mixer(h3, moba_cos, moba_lo, moba_hi, 3 * nb, 4 * nb, 5 * nb, 7 * nb)
    x1 = post_mixer(x, ya, yb, ev_w_out[0], xa_norm[0], xa_wq[0], memory_kv(0), xa_wo[0],
                    final_norm, final=False)

    h = norm_matmul(x1.reshape(tokens, d), od_norm[0], od_w_in.reshape(od_w_in.shape[1:]), BF16,
                    tm=2048, tn=1024, name="in_proj_odd")
    h3 = h.reshape(bsz, seq, -1)
    yc = rglru_mixer(h3, od_conv_w[0], od_conv_b[0], od_rg_wa[0], od_rg_ba[0], od_rg_wx[0],
                     od_rg_bx[0], od_rg_lambda[0], 0, 4 * nb)
    yd = retention_mixer(h3, ret_cos, ret_sin, log_g, od_ret_gain[0], od_ret_bias[0],
                         nb, 2 * nb, 3 * nb, 5 * nb)
    return post_mixer(x1, yc, yd, od_w_out[0], xa_norm[1], xa_wq[1], memory_kv(1), xa_wo[1],
                      final_norm, final=True)
```

```python
import functools
import math

import jax
import jax.numpy as jnp
from jax import lax
from jax.experimental import pallas as pl
from jax.experimental.pallas import tpu as pltpu

F32 = jnp.float32
BF16 = jnp.bfloat16

D_MODEL = 1024
HEAD_DIM = 128
N_HEADS = 8
WIDTH = N_HEADS * HEAD_DIM
MEM_LEN = 256
HGRN_CHUNK = 32
HGRN_TILE = 128
HGRN_PARTS = 16
MOBA_BLOCK = 256
MOBA_TOPK = 3
RET_CHUNK = 256
RET_PARTS = 2
CONV_WIDTH = 4
RG_C = 8.0
ROPE_THETA = 500000.0
ROPE_DIM = HEAD_DIM // 4
RET_THETA = 10000.0
POST_PARTS = 4
XA_HEADS = 4
XA_WIDTH = XA_HEADS * HEAD_DIM
EPS = 1e-6
VMEM_LIMIT = 56 * 1024 * 1024
MASKED = -1e30

_NT = (((1,), (1,)), ((), ()))


def _dot(a, b):
    return jnp.dot(a, b, preferred_element_type=F32)


def _dot_nt(a, b):
    return lax.dot_general(a, b, _NT, preferred_element_type=F32)


LOG2E = math.log2(math.e)


def _sigmoid(x):
    return 1.0 / (1.0 + jnp.exp2(x * (-LOG2E)))


def _silu(x):
    return x * _sigmoid(x)


def _run_staggered(parts, n_stages):
    for step in range(n_stages + len(parts) - 1):
        for p, gen in enumerate(parts):
            if 0 <= step - p < n_stages:
                next(gen)


def _params(*sem):
    return pltpu.CompilerParams(dimension_semantics=sem, vmem_limit_bytes=VMEM_LIMIT)


def _norm_matmul_kernel(x_ref, g_ref, w_ref, o_ref, xn_ref):
    @pl.when(pl.program_id(1) == 0)
    def _():
        x = x_ref[...]
        ms = jnp.mean(x * x, axis=-1, keepdims=True)
        xn_ref[...] = (x * lax.rsqrt(ms + EPS) * g_ref[...]).astype(BF16)

    o_ref[...] = _dot(xn_ref[...], w_ref[...].astype(BF16)).astype(o_ref.dtype)


def norm_matmul(x, gain, w, out_dtype, tm, tn, name):
    t, d = x.shape
    n = w.shape[1]
    return pl.pallas_call(
        _norm_matmul_kernel,
        out_shape=jax.ShapeDtypeStruct((t, n), out_dtype),
        grid=(t // tm, n // tn),
        in_specs=[
            pl.BlockSpec((tm, d), lambda i, j: (i, 0)),
            pl.BlockSpec((1, d), lambda i, j: (0, 0)),
            pl.BlockSpec((d, tn), lambda i, j: (0, j)),
        ],
        out_specs=pl.BlockSpec((tm, tn), lambda i, j: (i, j)),
        scratch_shapes=[pltpu.VMEM((tm, d), BF16)],
        compiler_params=_params("arbitrary", "arbitrary"),
        name=name,
    )(x, gain.reshape(1, d), w)


def _split2(x):
    hi = x.astype(BF16)
    lo = (x - hi.astype(F32)).astype(BF16)
    return hi, lo


def _hgrn2_kernel(q_ref, f_ref, i_ref, lb_ref, gain_ref, o_ref, st_ref):
    seq = q_ref.shape[0]
    tile, chunk = HGRN_TILE, HGRN_CHUNK
    per_tile = tile // chunk
    part_rows = seq // HGRN_PARTS
    nt = part_rows // tile
    lb = lb_ref[...]
    gain = gain_ref[...]

    def tiles(x):
        return x.reshape(nt, tile, HEAD_DIM)

    def bmm(spec, a, b):
        return jnp.einsum(spec, a, b, preferred_element_type=F32)

    ti = lax.broadcasted_iota(jnp.int32, (tile, tile), 0)
    tj = lax.broadcasted_iota(jnp.int32, (tile, tile), 1)
    intra_mask = (((ti // chunk) == (tj // chunk)) & (tj <= ti))[None]
    sum_mat = jnp.broadcast_to(jnp.where(intra_mask, 1.0, 0.0).astype(BF16), (nt, tile, tile))
    row_chunk = (lax.broadcasted_iota(jnp.int32, (tile, HEAD_DIM), 0) // chunk)[None]

    def per_chunk_lanes(x):
        zero = jnp.zeros_like(x)
        return jnp.concatenate([jnp.where(row_chunk == ci, x, zero) for ci in range(per_tile)], axis=-1)

    carried = [jnp.zeros((HEAD_DIM, HEAD_DIM), F32)]

    def part(p):
        rows = slice(p * part_rows, (p + 1) * part_rows)
        f = lb + (1.0 - lb) * _sigmoid(f_ref[rows, :].astype(F32))
        k = tiles(1.0 - f)
        hi, lo = _split2(jnp.log2(f))
        yield
        b = bmm('trc,tcd->trd', sum_mat, tiles(hi)) + bmm('trc,tcd->trd', sum_mat, tiles(lo))
        b_chunks = b.reshape(part_rows // chunk, chunk, HEAD_DIM)
        b_tot = jnp.broadcast_to(b_chunks[:, chunk - 1:, :], b_chunks.shape).reshape(b.shape)
        yield
        q_dec = (tiles(q_ref[rows, :].astype(F32)) * jnp.exp2(b)).astype(BF16)
        k_inv = (k * jnp.exp2(-b)).astype(BF16)
        k_end = (k * jnp.exp2(b_tot - b)).astype(BF16)
        v = tiles(i_ref[rows, :].astype(F32))
        yield
        att = bmm('tid,tjd->tij', q_dec, k_inv)
        att = jnp.where(intra_mask, att, 0.0).astype(BF16)
        v_t = jnp.swapaxes(v, 1, 2).astype(BF16)
        incr = bmm('tvn,tnk->tvk', v_t, per_chunk_lanes(k_end))
        yield
        o_intra = bmm('tij,tjd->tid', att, v.astype(BF16))
        state_t = carried[0]
        for t in range(nt):
            for ci in range(per_tile):
                lanes = slice(ci * HEAD_DIM, (ci + 1) * HEAD_DIM)
                st_ref[p * nt + t, :, lanes] = state_t.astype(BF16)
                decay = jnp.exp2(b_tot[t, ci * chunk:ci * chunk + 1, :])
                state_t = state_t * decay + incr[t, :, lanes]
        carried[0] = state_t
        yield
        o = o_intra + bmm('tnk,tvk->tnv', per_chunk_lanes(q_dec), st_ref[p * nt:(p + 1) * nt])
        o = o.reshape(part_rows, HEAD_DIM)
        y = o * lax.rsqrt(jnp.mean(o * o, axis=-1, keepdims=True) + EPS) * gain
        o_ref[rows, :] = y.astype(o_ref.dtype)
        yield

    _run_staggered([part(p) for p in range(HGRN_PARTS)], n_stages=6)


def hgrn2_mixer(h3, lb, gain, col_q, col_f, col_i):
    bsz, seq, _ = h3.shape

    def col(off):
        return pl.BlockSpec((None, seq, HEAD_DIM), lambda b, h: (b, 0, off + h))

    vec = pl.BlockSpec((None, 1, HEAD_DIM), lambda b, h: (h, 0, 0))
    return pl.pallas_call(
        _hgrn2_kernel,
        out_shape=jax.ShapeDtypeStruct((bsz, seq, WIDTH), BF16),
        grid=(bsz, N_HEADS),
        in_specs=[col(col_q), col(col_f), col(col_i), vec, vec],
        out_specs=pl.BlockSpec((None, seq, HEAD_DIM), lambda b, h: (b, 0, h)),
        scratch_shapes=[pltpu.VMEM((seq // HGRN_TILE, HEAD_DIM, HGRN_TILE // HGRN_CHUNK * HEAD_DIM), BF16)],
        compiler_params=_params("arbitrary", "arbitrary"),
        name="hgrn2",
    )(h3, h3, h3, lb.reshape(N_HEADS, 1, HEAD_DIM), gain.reshape(N_HEADS, 1, HEAD_DIM))


RET_HALF = HEAD_DIM // 2
MOBA_HALF = ROPE_DIM // 2


def _rope_table_kernel(pos_ref, inv_ref, ret_cos_ref, ret_sin_ref, moba_cos_ref, moba_lo_ref, moba_hi_ref):
    ang = pos_ref[...] * inv_ref[...]
    cos = jnp.cos(ang)
    sin = jnp.sin(ang)
    lane = lax.broadcasted_iota(jnp.int32, ang.shape, 1)
    low = lane < RET_HALF
    cos_up = pltpu.roll(cos, RET_HALF, 1)
    sin_up = pltpu.roll(sin, RET_HALF, 1)
    ret_cos_ref[...] = jnp.where(low, cos, cos_up)
    ret_sin_ref[...] = jnp.where(low, -sin, sin_up)
    first = lane < MOBA_HALF
    second = (lane >= MOBA_HALF) & (lane < 2 * MOBA_HALF)
    moba_cos_ref[...] = jnp.where(first, cos_up, jnp.where(second, pltpu.roll(cos_up, MOBA_HALF, 1), 1.0))
    moba_lo_ref[...] = jnp.where(first, -sin_up, 0.0)
    moba_hi_ref[...] = jnp.where(second, pltpu.roll(sin_up, MOBA_HALF, 1), 0.0)


def rope_tables(positions):
    bsz, seq = positions.shape
    inv_ret = RET_THETA ** (-jnp.arange(RET_HALF, dtype=F32) / RET_HALF)
    inv_moba = ROPE_THETA ** (-jnp.arange(MOBA_HALF, dtype=F32) / MOBA_HALF)
    inv = jnp.concatenate([inv_ret, inv_moba, jnp.zeros((HEAD_DIM - RET_HALF - MOBA_HALF,), F32)])
    posf = jnp.broadcast_to(positions.astype(F32)[:, :, None], (bsz, seq, HEAD_DIM))
    spec = pl.BlockSpec((None, seq, HEAD_DIM), lambda b: (b, 0, 0))
    return pl.pallas_call(
        _rope_table_kernel,
        out_shape=[jax.ShapeDtypeStruct((bsz, seq, HEAD_DIM), F32)] * 5,
        grid=(bsz,),
        in_specs=[spec, pl.BlockSpec((1, HEAD_DIM), lambda b: (0, 0))],
        out_specs=[spec] * 5,
        compiler_params=_params("arbitrary"),
        name="rope_tables",
    )(posf, inv.reshape(1, HEAD_DIM))


def _moba_kernel(q_ref, k_ref, v_ref, cos_ref, sin_lo_ref, sin_hi_ref, o_ref,
                 qr_ref, ka_ref, va_ref, kmean_ref):
    seq = q_ref.shape[0]
    blk = MOBA_BLOCK
    nb = seq // blk
    exp2_scale = HEAD_DIM ** -0.5 * LOG2E

    row = lax.broadcasted_iota(jnp.int32, (blk, blk), 0)
    colm = lax.broadcasted_iota(jnp.int32, (blk, blk), 1)
    causal = colm <= row
    eye = jnp.where(colm == row, 1.0, 0.0).astype(BF16)
    blk_row = lax.broadcasted_iota(jnp.int32, (nb, blk), 0)
    lane = lax.broadcasted_iota(jnp.int32, (blk, HEAD_DIM), 1)
    kmean_ref[...] = jnp.zeros(kmean_ref.shape, kmean_ref.dtype)

    def prepare(qb):
        rows = slice(qb * blk, (qb + 1) * blk)

        def rope(x):
            return (x * cos_ref[rows, :]
                    + pltpu.roll(x, HEAD_DIM - MOBA_HALF, 1) * sin_lo_ref[rows, :]
                    + pltpu.roll(x, MOBA_HALF, 1) * sin_hi_ref[rows, :])

        kr = rope(k_ref[rows, :].astype(F32))
        kmean_ref[qb:qb + 1, :] = jnp.mean(kr, axis=0, keepdims=True)
        ka_ref[rows, :HEAD_DIM] = kr.astype(BF16)
        ka_ref[rows, HEAD_DIM:] = jnp.where(lane == qb, MASKED, 0.0).astype(BF16)
        qr_ref[rows, :] = (rope(q_ref[rows, :].astype(F32)) * exp2_scale).astype(BF16)
        va_ref[rows, :HEAD_DIM] = v_ref[rows, :]
        va_ref[rows, HEAD_DIM:] = jnp.ones((blk, HEAD_DIM), BF16)

    def scores(qb):
        rows = slice(qb * blk, (qb + 1) * blk)
        q = qr_ref[rows, :]
        n_keys = (qb + 1) * blk
        if qb > MOBA_TOPK:
            gate = _dot_nt(kmean_ref[...].astype(BF16), q)
            rank = jnp.zeros((nb, blk), F32)
            for j in range(qb):
                gj = gate[j:j + 1, :]
                ahead = (gj > gate) | ((gj == gate) & (blk_row > j))
                rank = rank + jnp.where(ahead, 1.0, 0.0)
            drop = jnp.where((rank >= float(MOBA_TOPK)) & (blk_row < qb), 1.0, 0.0)
            drop = jnp.concatenate([drop, jnp.zeros((HEAD_DIM - nb, blk), F32)], axis=0).astype(BF16)
            drop_col = _dot_nt(eye, drop).astype(BF16)
            s = _dot_nt(jnp.concatenate([q, drop_col], axis=1), ka_ref[:n_keys, :])
        else:
            s = _dot_nt(q, ka_ref[:n_keys, :HEAD_DIM])
        own = jnp.where(causal, s[:, qb * blk:], MASKED)
        return jnp.concatenate([s[:, :qb * blk], own], axis=1) if qb else own

    prepare(0)
    s_next = scores(0)
    for qb in range(nb):
        rows = slice(qb * blk, (qb + 1) * blk)
        s = s_next
        if qb + 1 < nb:
            prepare(qb + 1)
            s_next = scores(qb + 1)
        m = jnp.max(s, axis=-1, keepdims=True)
        p = jnp.exp2(s - m).astype(BF16)
        o = _dot(p, va_ref[:(qb + 1) * blk, :])
        o_ref[rows, :] = (o[:, :HEAD_DIM] / o[:, HEAD_DIM:]).astype(o_ref.dtype)


def moba_mixer(h3, cos, sin_lo, sin_hi, col_q, col_k, col_v):
    bsz, seq, _ = h3.shape

    def col(off):
        return pl.BlockSpec((None, seq, HEAD_DIM), lambda b, h: (b, 0, off + h))

    table = pl.BlockSpec((None, seq, HEAD_DIM), lambda b, h: (b, 0, 0))
    return pl.pallas_call(
        _moba_kernel,
        out_shape=jax.ShapeDtypeStruct((bsz, seq, WIDTH), BF16),
        grid=(bsz, N_HEADS),
        in_specs=[col(col_q), col(col_k), col(col_v), table, table, table],
        out_specs=pl.BlockSpec((None, seq, HEAD_DIM), lambda b, h: (b, 0, h)),
        scratch_shapes=[pltpu.VMEM((seq, HEAD_DIM), BF16), pltpu.VMEM((seq, 2 * HEAD_DIM), BF16),
                        pltpu.VMEM((seq, 2 * HEAD_DIM), BF16), pltpu.VMEM((seq // MOBA_BLOCK, HEAD_DIM), F32)],
        compiler_params=_params("arbitrary", "arbitrary"),
        name="moba",
    )(h3, h3, h3, cos, sin_lo, sin_hi)


def _rglru_kernel(x_ref, cw_ref, cb_ref, wa_ref, ba_ref, wx_ref, bx_ref, lam_ref,
                  o_ref, a_ref, u_ref, xpad_ref):
    seq, width = x_ref.shape
    sub = 8
    groups = seq // sub

    xpad_ref[:sub, :] = jnp.zeros((sub, width), F32)
    xpad_ref[sub:, :] = x_ref[...].astype(F32)
    xf = cb_ref[...]
    for d in range(CONV_WIDTH):
        xf = xf + xpad_ref[sub - d:sub - d + seq, :] * cw_ref[CONV_WIDTH - 1 - d:CONV_WIDTH - d, :]

    xb = xf.astype(BF16)

    def gate(w_ref, b_ref):
        pre = [_dot(xb[:, j * HEAD_DIM:(j + 1) * HEAD_DIM], w_ref[j]) for j in range(width // HEAD_DIM)]
        return _sigmoid(jnp.concatenate(pre, axis=1) + b_ref[...])

    r = gate(wa_ref, ba_ref)
    ig = gate(wx_ref, bx_ref)
    nl = -lam_ref[...]
    softplus = jnp.maximum(nl, 0.0) + jnp.log1p(jnp.exp(-jnp.abs(nl)))
    a = jnp.exp2(r * ((-RG_C * LOG2E) * softplus))
    u = jnp.sqrt(1.0 - a * a) * ig * xf

    a3 = a.reshape(groups, sub, width)
    u3 = u.reshape(groups, sub, width)
    s_idx = lax.broadcasted_iota(jnp.int32, a3.shape, 1)
    for d in (1, 2, 4):
        keep = s_idx >= d
        a_prev = jnp.where(keep, pltpu.roll(a3, d, 1), 1.0)
        u_prev = jnp.where(keep, pltpu.roll(u3, d, 1), 0.0)
        u3 = u3 + a3 * u_prev
        a3 = a3 * a_prev
    a_ref[...] = a3
    u_ref[...] = u3

    def group_body(g, carry):
        hg = a_ref[g] * carry + u_ref[g]
        u_ref[g] = hg
        return jnp.broadcast_to(hg[sub - 1:sub, :], (sub, width))

    lax.fori_loop(0, groups, group_body, jnp.zeros((sub, width), F32), unroll=8)
    o_ref[...] = u_ref[...].reshape(seq, width).astype(o_ref.dtype)


def rglru_mixer(h3, conv_w, conv_b, wa, ba, wx, bx, lam, col_x, blocks_per_step=2):
    bsz, seq, _ = h3.shape
    width = blocks_per_step * HEAD_DIM

    def col(off):
        return pl.BlockSpec((None, seq, width), lambda b, c: (b, 0, off // blocks_per_step + c))

    def vec(rows):
        return pl.BlockSpec((rows, width), lambda b, c: (0, c))

    wspec = pl.BlockSpec((blocks_per_step, HEAD_DIM, HEAD_DIM), lambda b, c: (c, 0, 0))
    return pl.pallas_call(
        _rglru_kernel,
        out_shape=jax.ShapeDtypeStruct((bsz, seq, WIDTH), BF16),
        grid=(bsz, WIDTH // width),
        in_specs=[col(col_x), vec(CONV_WIDTH), vec(1), wspec, vec(1), wspec, vec(1), vec(1)],
        out_specs=pl.BlockSpec((None, seq, width), lambda b, c: (b, 0, c)),
        scratch_shapes=[pltpu.VMEM((seq // 8, 8, width), F32)] * 2 + [pltpu.VMEM((seq + 8, width), F32)],
        compiler_params=_params("arbitrary", "arbitrary"),
        name="rglru",
    )(h3, conv_w, conv_b.reshape(1, WIDTH), wa.astype(BF16), ba.reshape(1, WIDTH),
      wx.astype(BF16), bx.reshape(1, WIDTH), lam.reshape(1, WIDTH))


def _retention_kernel(q_ref, k_ref, v_ref, cos_ref, sin_ref, logg_ref, gain_ref, bias_ref,
                      o_ref, st_ref):
    seq = q_ref.shape[0]
    ck = RET_CHUNK
    half = RET_HALF
    nc = seq // ck // RET_PARTS

    def chunks(x):
        return x.reshape(nc, ck, HEAD_DIM)

    def bmm(spec, a, b):
        return jnp.einsum(spec, a, b, preferred_element_type=F32)

    log_g = logg_ref[...]
    ri = lax.broadcasted_iota(jnp.int32, (ck, ck), 0)
    ci = lax.broadcasted_iota(jnp.int32, (ck, ck), 1)
    diff = (ri - ci).astype(F32)
    dmask = jnp.where(ri >= ci, jnp.exp(jnp.maximum(diff, 0.0) * log_g[:, :1]), 0.0)
    idx = lax.broadcasted_iota(jnp.int32, (ck, HEAD_DIM), 0).astype(F32)
    q_fac = jnp.exp((idx + 1.0) * log_g)
    k_fac = jnp.exp((ck - 1.0 - idx) * log_g)
    chunk_decay = jnp.exp(float(ck) * log_g)

    carried = [jnp.zeros((HEAD_DIM, HEAD_DIM), F32)]

    def part(p):
        rows = slice(p * nc * ck, (p + 1) * nc * ck)
        cos = cos_ref[rows, :]
        sin = sin_ref[rows, :]
        q = q_ref[rows, :].astype(F32)
        k = k_ref[rows, :].astype(F32)
        q = chunks(q * cos + pltpu.roll(q, half, 1) * sin)
        k = chunks((k * cos + pltpu.roll(k, half, 1) * sin) * (HEAD_DIM ** -0.5))
        v = chunks(v_ref[rows, :])
        yield
        att = bmm('cid,cjd->cij', q.astype(BF16), k.astype(BF16)) * dmask[None]
        k_end_t = jnp.swapaxes(k * k_fac[None], 1, 2).astype(BF16)
        incr = bmm('ckn,cnv->ckv', k_end_t, v)
        yield
        o_intra = bmm('cij,cjd->cid', att.astype(BF16), v)
        state = carried[0]
        for c in range(nc):
            st_ref[p * nc + c] = state.astype(BF16)
            state = state * chunk_decay + incr[c]
        carried[0] = state
        yield
        o = o_intra + bmm('cnk,ckv->cnv', (q * q_fac[None]).astype(BF16), st_ref[p * nc:(p + 1) * nc])
        o = o.reshape(nc * ck, HEAD_DIM)
        mu = jnp.mean(o, axis=-1, keepdims=True)
        oc = o - mu
        var = jnp.mean(oc * oc, axis=-1, keepdims=True)
        o_ref[rows, :] = (oc * lax.rsqrt(var + EPS) * gain_ref[...] + bias_ref[...]).astype(o_ref.dtype)
        yield

    _run_staggered([part(p) for p in range(RET_PARTS)], n_stages=4)


def retention_mixer(h3, cos, sin, log_g, gain, bias, col_q, col_k, col_v):
    bsz, seq, _ = h3.shape

    def col(off):
        return pl.BlockSpec((None, seq, HEAD_DIM), lambda b, h: (b, 0, off + h))

    table = pl.BlockSpec((None, seq, HEAD_DIM), lambda b, h: (b, 0, 0))
    vec = pl.BlockSpec((None, 1, HEAD_DIM), lambda b, h: (h, 0, 0))
    return pl.pallas_call(
        _retention_kernel,
        out_shape=jax.ShapeDtypeStruct((bsz, seq, WIDTH), BF16),
        grid=(bsz, N_HEADS),
        in_specs=[col(col_q), col(col_k), col(col_v), table, table, vec, vec, vec],
        out_specs=pl.BlockSpec((None, seq, HEAD_DIM), lambda b, h: (b, 0, h)),
        scratch_shapes=[pltpu.VMEM((seq // RET_CHUNK, HEAD_DIM, HEAD_DIM), BF16)],
        compiler_params=_params("arbitrary", "arbitrary"),
        name="retention",
    )(h3, h3, h3, cos, sin, log_g,
      gain.reshape(N_HEADS, 1, HEAD_DIM), bias.reshape(N_HEADS, 1, HEAD_DIM))


def _post_kernel(x_ref, ya_ref, yb_ref, z_ref, woa_ref, wob_ref, g_ref, wq_ref, kv_ref, wo_ref, fg_ref,
                 o_ref, *, final):
    part_rows = x_ref.shape[0] // POST_PARTS
    scale = HEAD_DIM ** -0.5

    def gated(y_ref, rows, cols):
        return (y_ref[rows, :].astype(F32) * _silu(z_ref[rows, cols].astype(F32))).astype(BF16)

    def part(p):
        rows = slice(p * part_rows, (p + 1) * part_rows)
        ga = gated(ya_ref, rows, slice(0, WIDTH))
        yield
        gb = gated(yb_ref, rows, slice(WIDTH, 2 * WIDTH))
        x1 = x_ref[rows, :] + _dot(ga, woa_ref[...])
        yield
        x1 = x1 + _dot(gb, wob_ref[...])
        yield
        ms = jnp.mean(x1 * x1, axis=-1, keepdims=True)
        xn = (x1 * lax.rsqrt(ms + EPS) * g_ref[...]).astype(BF16)
        q = _dot(xn, wq_ref[...]).astype(BF16)
        yield
        head_cols = [slice(hd * HEAD_DIM, (hd + 1) * HEAD_DIM) for hd in range(XA_HEADS)]
        scores = [_dot_nt(q[:, cs], kv_ref[:, cs]) * scale for cs in head_cols]
        yield
        probs = []
        for s in scores:
            pr = jnp.exp(s - jnp.max(s, axis=-1, keepdims=True))
            probs.append((pr / jnp.sum(pr, axis=-1, keepdims=True)).astype(BF16))
        yield
        o = jnp.concatenate([_dot(pr, kv_ref[:, XA_WIDTH + cs.start:XA_WIDTH + cs.stop]).astype(BF16)
                             for pr, cs in zip(probs, head_cols)], axis=1)
        yield
        x2 = x1 + _dot(o, wo_ref[...])
        if final:
            ms2 = jnp.mean(x2 * x2, axis=-1, keepdims=True)
            x2 = x2 * lax.rsqrt(ms2 + EPS) * fg_ref[...]
        o_ref[rows, :] = x2
        yield

    _run_staggered([part(p) for p in range(POST_PARTS)], n_stages=8)


def post_mixer(x3, ya, yb, h3, z_block, w_out, xa_gain, wq, kv, wo, final_gain, final, tm=1024):
    bsz, seq, d = x3.shape
    full = lambda shape: pl.BlockSpec(shape, lambda b, i: (0,) * len(shape))
    tile = lambda w: pl.BlockSpec((None, tm, w), lambda b, i: (b, i, 0))
    return pl.pallas_call(
        functools.partial(_post_kernel, final=final),
        out_shape=jax.ShapeDtypeStruct((bsz, seq, d), F32),
        grid=(bsz, seq // tm),
        in_specs=[tile(d), tile(WIDTH), tile(WIDTH),
                  pl.BlockSpec((None, tm, 2 * WIDTH), lambda b, i: (b, i, z_block)),
                  full((WIDTH, d)), full((WIDTH, d)), full((1, d)), full((d, XA_WIDTH)),
                  pl.BlockSpec((None, MEM_LEN, 2 * XA_WIDTH), lambda b, i: (b, 0, 0)),
                  full((XA_WIDTH, d)), full((1, d))],
        out_specs=tile(d),
        compiler_params=_params("arbitrary", "arbitrary"),
        name="post_final" if final else "post",
    )(x3, ya, yb, h3, w_out[:WIDTH].astype(BF16), w_out[WIDTH:].astype(BF16), xa_gain.reshape(1, d),
      wq.astype(BF16), kv, wo.astype(BF16), final_gain.reshape(1, d))


def kernel(x, mem, positions, hgrn_lb_logits, ev_norm, ev_w_in, ev_hgrn_gain, ev_w_out, od_norm, od_w_in, od_conv_w, od_conv_b, od_rg_wa, od_rg_ba, od_rg_wx, od_rg_bx, od_rg_lambda, od_ret_gain, od_ret_bias, od_w_out, xa_norm, xa_mem_norm, xa_wq, xa_wkv, xa_wo, final_norm):
    bsz, seq, d = x.shape
    tokens = bsz * seq
    lb_all = jnp.cumsum(jax.nn.softmax(hgrn_lb_logits.astype(F32), axis=0), axis=0)
    ret_cos, ret_sin, moba_cos, moba_lo, moba_hi = rope_tables(positions)
    log_g = jnp.log(1.0 - 2.0 ** (-5.0 - jnp.arange(N_HEADS, dtype=F32)))
    log_g = jnp.broadcast_to(log_g[:, None, None], (N_HEADS, 1, HEAD_DIM))
    mem2 = mem.reshape(bsz * MEM_LEN, d)

    def memory_kv(layer):
        kv = norm_matmul(mem2, xa_mem_norm[layer], xa_wkv[layer], BF16,
                         tm=bsz * MEM_LEN // 2, tn=2 * XA_WIDTH, name="mem_kv")
        return kv.reshape(bsz, MEM_LEN, 2 * XA_WIDTH)

    h = norm_matmul(x.reshape(tokens, d), ev_norm[0], ev_w_in.reshape(ev_w_in.shape[1:]), BF16,
                    tm=2048, tn=1024, name="in_proj_even")
    h3 = h.reshape(bsz, seq, -1)
    nb = WIDTH // HEAD_DIM
    ya = hgrn2_mixer(h3, lb_all[0], ev_hgrn_gain[0], 0, nb, 2 * nb)
    yb = moba_mixer(h3, moba_cos, moba_lo, moba_hi, 3 * nb, 4 * nb, 5 * nb)
    x1 = post_mixer(x, ya, yb, h3, 3, ev_w_out[0], xa_norm[0], xa_wq[0], memory_kv(0), xa_wo[0],
                    final_norm, final=False)

    h = norm_matmul(x1.reshape(tokens, d), od_norm[0], od_w_in.reshape(od_w_in.shape[1:]), BF16,
                    tm=2048, tn=1024, name="in_proj_odd")
    h3 = h.reshape(bsz, seq, -1)
    yc = rglru_mixer(h3, od_conv_w[0], od_conv_b[0], od_rg_wa[0], od_rg_ba[0], od_rg_wx[0],
                     od_rg_bx[0], od_rg_lambda[0], 0)
    yd = retention_mixer(h3, ret_cos, ret_sin, log_g, od_ret_gain[0], od_ret_bias[0],
                         nb, 2 * nb, 3 * nb)
    return post_mixer(x1, yc, yd, h3, 2, od_w_out[0], xa_norm[1], xa_wq[1], memory_kv(1), xa_wo[1],
                      final_norm, final=True)
```

```python
import functools
import math

import jax
import jax.numpy as jnp
from jax import lax
from jax.experimental import pallas as pl
from jax.experimental.pallas import tpu as pltpu

F32 = jnp.float32
BF16 = jnp.bfloat16

D_MODEL = 1024
HEAD_DIM = 128
N_HEADS = 8
WIDTH = N_HEADS * HEAD_DIM
MEM_LEN = 256
HGRN_CHUNK = 32
HGRN_TILE = 128
HGRN_PARTS = 16
MOBA_BLOCK = 256
MOBA_TOPK = 3
RET_CHUNK = 256
RET_PARTS = 2
CONV_WIDTH = 4
RG_C = 8.0
ROPE_THETA = 500000.0
ROPE_DIM = HEAD_DIM // 4
RET_THETA = 10000.0
PROJ_PIECES = 2
POST_PARTS = 4
XA_HEADS = 4
XA_WIDTH = XA_HEADS * HEAD_DIM
EPS = 1e-6
VMEM_LIMIT = 56 * 1024 * 1024
MASKED = -1e30

_NT = (((1,), (1,)), ((), ()))


def _dot(a, b):
    return jnp.dot(a, b, preferred_element_type=F32)


def _dot_nt(a, b):
    return lax.dot_general(a, b, _NT, preferred_element_type=F32)


LOG2E = math.log2(math.e)


def _sigmoid(x):
    return 1.0 / (1.0 + jnp.exp2(x * (-LOG2E)))


def _silu(x):
    return x * _sigmoid(x)


def _run_staggered(parts, n_stages):
    for step in range(n_stages + len(parts) - 1):
        for p, gen in enumerate(parts):
            if 0 <= step - p < n_stages:
                next(gen)


def _params(*sem):
    return pltpu.CompilerParams(dimension_semantics=sem, vmem_limit_bytes=VMEM_LIMIT)


def _norm_matmul_kernel(x_ref, g_ref, w_ref, o_ref, xn_ref):
    @pl.when(pl.program_id(1) == 0)
    def _():
        x = x_ref[...]
        ms = jnp.mean(x * x, axis=-1, keepdims=True)
        xn_ref[...] = (x * lax.rsqrt(ms + EPS) * g_ref[...]).astype(BF16)

    o_ref[...] = _dot(xn_ref[...], w_ref[...].astype(BF16)).astype(o_ref.dtype)


def norm_matmul(x, gain, w, out_dtype, tm, tn, name):
    t, d = x.shape
    n = w.shape[1]
    return pl.pallas_call(
        _norm_matmul_kernel,
        out_shape=jax.ShapeDtypeStruct((t, n), out_dtype),
        grid=(t // tm, n // tn),
        in_specs=[
            pl.BlockSpec((tm, d), lambda i, j: (i, 0)),
            pl.BlockSpec((1, d), lambda i, j: (0, 0)),
            pl.BlockSpec((d, tn), lambda i, j: (0, j)),
        ],
        out_specs=pl.BlockSpec((tm, tn), lambda i, j: (i, j)),
        scratch_shapes=[pltpu.VMEM((tm, d), BF16)],
        compiler_params=_params("arbitrary", "arbitrary"),
        name=name,
    )(x, gain.reshape(1, d), w)


def _split2(x):
    hi = x.astype(BF16)
    lo = (x - hi.astype(F32)).astype(BF16)
    return hi, lo


def _hgrn2_kernel(q_ref, f_ref, i_ref, lb_ref, gain_ref, o_ref, st_ref):
    seq = q_ref.shape[0]
    tile, chunk = HGRN_TILE, HGRN_CHUNK
    per_tile = tile // chunk
    part_rows = seq // HGRN_PARTS
    nt = part_rows // tile
    lb = lb_ref[...]
    gain = gain_ref[...]

    def tiles(x):
        return x.reshape(nt, tile, HEAD_DIM)

    def bmm(spec, a, b):
        return jnp.einsum(spec, a, b, preferred_element_type=F32)

    ti = lax.broadcasted_iota(jnp.int32, (tile, tile), 0)
    tj = lax.broadcasted_iota(jnp.int32, (tile, tile), 1)
    intra_mask = (((ti // chunk) == (tj // chunk)) & (tj <= ti))[None]
    sum_mat = jnp.broadcast_to(jnp.where(intra_mask, 1.0, 0.0).astype(BF16), (nt, tile, tile))
    row_chunk = (lax.broadcasted_iota(jnp.int32, (tile, HEAD_DIM), 0) // chunk)[None]

    def per_chunk_lanes(x):
        zero = jnp.zeros_like(x)
        return jnp.concatenate([jnp.where(row_chunk == ci, x, zero) for ci in range(per_tile)], axis=-1)

    carried = [jnp.zeros((HEAD_DIM, HEAD_DIM), F32)]

    def part(p):
        rows = slice(p * part_rows, (p + 1) * part_rows)
        f = lb + (1.0 - lb) * _sigmoid(f_ref[rows, :].astype(F32))
        k = tiles(1.0 - f)
        hi, lo = _split2(jnp.log2(f))
        yield
        b = bmm('trc,tcd->trd', sum_mat, tiles(hi)) + bmm('trc,tcd->trd', sum_mat, tiles(lo))
        b_chunks = b.reshape(part_rows // chunk, chunk, HEAD_DIM)
        b_tot = jnp.broadcast_to(b_chunks[:, chunk - 1:, :], b_chunks.shape).reshape(b.shape)
        yield
        q_dec = (tiles(q_ref[rows, :].astype(F32)) * jnp.exp2(b)).astype(BF16)
        k_inv = (k * jnp.exp2(-b)).astype(BF16)
        k_end = (k * jnp.exp2(b_tot - b)).astype(BF16)
        v = tiles(i_ref[rows, :].astype(F32))
        yield
        att = bmm('tid,tjd->tij', q_dec, k_inv)
        att = jnp.where(intra_mask, att, 0.0).astype(BF16)
        v_t = jnp.swapaxes(v, 1, 2).astype(BF16)
        incr = bmm('tvn,tnk->tvk', v_t, per_chunk_lanes(k_end))
        yield
        o_intra = bmm('tij,tjd->tid', att, v.astype(BF16))
        state_t = carried[0]
        for t in range(nt):
            for ci in range(per_tile):
                lanes = slice(ci * HEAD_DIM, (ci + 1) * HEAD_DIM)
                st_ref[p * nt + t, :, lanes] = state_t.astype(BF16)
                decay = jnp.exp2(b_tot[t, ci * chunk:ci * chunk + 1, :])
                state_t = state_t * decay + incr[t, :, lanes]
        carried[0] = state_t
        yield
        o = o_intra + bmm('tnk,tvk->tnv', per_chunk_lanes(q_dec), st_ref[p * nt:(p + 1) * nt])
        o = o.reshape(part_rows, HEAD_DIM)
        y = o * lax.rsqrt(jnp.mean(o * o, axis=-1, keepdims=True) + EPS) * gain
        o_ref[rows, :] = y.astype(o_ref.dtype)
        yield

    _run_staggered([part(p) for p in range(HGRN_PARTS)], n_stages=6)


def hgrn2_mixer(h3, lb, gain, col_q, col_f, col_i):
    bsz, seq, _ = h3.shape

    def col(off):
        return pl.BlockSpec((None, seq, HEAD_DIM), lambda b, h: (b, 0, off + h))

    vec = pl.BlockSpec((None, 1, HEAD_DIM), lambda b, h: (h, 0, 0))
    return pl.pallas_call(
        _hgrn2_kernel,
        out_shape=jax.ShapeDtypeStruct((bsz, seq, WIDTH), BF16),
        grid=(bsz, N_HEADS),
        in_specs=[col(col_q), col(col_f), col(col_i), vec, vec],
        out_specs=pl.BlockSpec((None, seq, HEAD_DIM), lambda b, h: (b, 0, h)),
        scratch_shapes=[pltpu.VMEM((seq // HGRN_TILE, HEAD_DIM, HGRN_TILE // HGRN_CHUNK * HEAD_DIM), BF16)],
        compiler_params=_params("arbitrary", "arbitrary"),
        name="hgrn2",
    )(h3, h3, h3, lb.reshape(N_HEADS, 1, HEAD_DIM), gain.reshape(N_HEADS, 1, HEAD_DIM))


RET_HALF = HEAD_DIM // 2
MOBA_HALF = ROPE_DIM // 2


def _rope_table_kernel(pos_ref, inv_ref, ret_cos_ref, ret_sin_ref, moba_cos_ref, moba_lo_ref, moba_hi_ref):
    ang = pos_ref[...] * inv_ref[...]
    cos = jnp.cos(ang)
    sin = jnp.sin(ang)
    lane = lax.broadcasted_iota(jnp.int32, ang.shape, 1)
    low = lane < RET_HALF
    cos_up = pltpu.roll(cos, RET_HALF, 1)
    sin_up = pltpu.roll(sin, RET_HALF, 1)
    ret_cos_ref[...] = jnp.where(low, cos, cos_up)
    ret_sin_ref[...] = jnp.where(low, -sin, sin_up)
    first = lane < MOBA_HALF
    second = (lane >= MOBA_HALF) & (lane < 2 * MOBA_HALF)
    moba_cos_ref[...] = jnp.where(first, cos_up, jnp.where(second, pltpu.roll(cos_up, MOBA_HALF, 1), 1.0))
    moba_lo_ref[...] = jnp.where(first, -sin_up, 0.0)
    moba_hi_ref[...] = jnp.where(second, pltpu.roll(sin_up, MOBA_HALF, 1), 0.0)


def _proj_tables_kernel(x_ref, g_ref, w_ref, pos_ref, inv_ref, o_ref, *table_and_scratch):
    table_refs, xn_ref = table_and_scratch[:-1], table_and_scratch[-1]

    @pl.when(pl.program_id(1) == 0)
    def _():
        x = x_ref[...]
        ms = jnp.mean(x * x, axis=-1, keepdims=True)
        xn_ref[...] = (x * lax.rsqrt(ms + EPS) * g_ref[...]).astype(BF16)

    w = w_ref[...].astype(BF16)
    rows = xn_ref.shape[0] // PROJ_PIECES
    trows = pos_ref.shape[0] // PROJ_PIECES
    for c in range(PROJ_PIECES):
        o_ref[c * rows:(c + 1) * rows, :] = _dot(xn_ref[c * rows:(c + 1) * rows, :], w).astype(o_ref.dtype)
        piece = slice(c * trows, (c + 1) * trows)
        _rope_table_kernel(pos_ref.at[piece, :], inv_ref, *(r.at[piece, :] for r in table_refs))


def project_with_rope_tables(x, gain, w, positions, tm, tn):
    t, d = x.shape
    n = w.shape[1]
    n_steps = n // tn
    rows = tm // n_steps
    inv_ret = RET_THETA ** (-jnp.arange(RET_HALF, dtype=F32) / RET_HALF)
    inv_moba = ROPE_THETA ** (-jnp.arange(MOBA_HALF, dtype=F32) / MOBA_HALF)
    inv = jnp.concatenate([inv_ret, inv_moba, jnp.zeros((HEAD_DIM - RET_HALF - MOBA_HALF,), F32)])
    posf = jnp.broadcast_to(positions.astype(F32).reshape(t, 1), (t, HEAD_DIM))
    table_spec = pl.BlockSpec((rows, HEAD_DIM), lambda i, j: (i * n_steps + j, 0))
    return pl.pallas_call(
        _proj_tables_kernel,
        out_shape=[jax.ShapeDtypeStruct((t, n), BF16)] + [jax.ShapeDtypeStruct((t, HEAD_DIM), F32)] * 5,
        grid=(t // tm, n_steps),
        in_specs=[
            pl.BlockSpec((tm, d), lambda i, j: (i, 0)),
            pl.BlockSpec((1, d), lambda i, j: (0, 0)),
            pl.BlockSpec((d, tn), lambda i, j: (0, j)),
            table_spec,
            pl.BlockSpec((1, HEAD_DIM), lambda i, j: (0, 0)),
        ],
        out_specs=[pl.BlockSpec((tm, tn), lambda i, j: (i, j))] + [table_spec] * 5,
        scratch_shapes=[pltpu.VMEM((tm, d), BF16)],
        compiler_params=_params("arbitrary", "arbitrary"),
        name="in_proj_even",
    )(x, gain.reshape(1, d), w, posf, inv.reshape(1, HEAD_DIM))


def _moba_kernel(q_ref, k_ref, v_ref, cos_ref, sin_lo_ref, sin_hi_ref, o_ref,
                 qr_ref, ka_ref, va_ref, kmean_ref):
    seq = q_ref.shape[0]
    blk = MOBA_BLOCK
    nb = seq // blk
    exp2_scale = HEAD_DIM ** -0.5 * LOG2E

    row = lax.broadcasted_iota(jnp.int32, (blk, blk), 0)
    colm = lax.broadcasted_iota(jnp.int32, (blk, blk), 1)
    causal = colm <= row
    eye = jnp.where(colm == row, 1.0, 0.0).astype(BF16)
    blk_row = lax.broadcasted_iota(jnp.int32, (nb, blk), 0)
    lane = lax.broadcasted_iota(jnp.int32, (blk, HEAD_DIM), 1)
    kmean_ref[...] = jnp.zeros(kmean_ref.shape, kmean_ref.dtype)

    def prepare(qb):
        rows = slice(qb * blk, (qb + 1) * blk)

        def rope(x):
            return (x * cos_ref[rows, :]
                    + pltpu.roll(x, HEAD_DIM - MOBA_HALF, 1) * sin_lo_ref[rows, :]
                    + pltpu.roll(x, MOBA_HALF, 1) * sin_hi_ref[rows, :])

        kr = rope(k_ref[rows, :].astype(F32))
        kmean_ref[qb:qb + 1, :] = jnp.mean(kr, axis=0, keepdims=True)
        ka_ref[rows, :HEAD_DIM] = kr.astype(BF16)
        ka_ref[rows, HEAD_DIM:] = jnp.where(lane == qb, MASKED, 0.0).astype(BF16)
        qr_ref[rows, :] = (rope(q_ref[rows, :].astype(F32)) * exp2_scale).astype(BF16)
        va_ref[rows, :HEAD_DIM] = v_ref[rows, :]
        va_ref[rows, HEAD_DIM:] = jnp.ones((blk, HEAD_DIM), BF16)

    def scores(qb):
        rows = slice(qb * blk, (qb + 1) * blk)
        q = qr_ref[rows, :]
        n_keys = (qb + 1) * blk
        if qb > MOBA_TOPK:
            gate = _dot_nt(kmean_ref[...].astype(BF16), q)
            rank = jnp.zeros((nb, blk), F32)
            for j in range(qb):
                gj = gate[j:j + 1, :]
                ahead = (gj > gate) | ((gj == gate) & (blk_row > j))
                rank = rank + jnp.where(ahead, 1.0, 0.0)
            drop = jnp.where((rank >= float(MOBA_TOPK)) & (blk_row < qb), 1.0, 0.0)
            drop = jnp.concatenate([drop, jnp.zeros((HEAD_DIM - nb, blk), F32)], axis=0).astype(BF16)
            drop_col = _dot_nt(eye, drop).astype(BF16)
            s = _dot_nt(jnp.concatenate([q, drop_col], axis=1), ka_ref[:n_keys, :])
        else:
            s = _dot_nt(q, ka_ref[:n_keys, :HEAD_DIM])
        own = jnp.where(causal, s[:, qb * blk:], MASKED)
        return jnp.concatenate([s[:, :qb * blk], own], axis=1) if qb else own

    prepare(0)
    s_next = scores(0)
    for qb in range(nb):
        rows = slice(qb * blk, (qb + 1) * blk)
        s = s_next
        if qb + 1 < nb:
            prepare(qb + 1)
            s_next = scores(qb + 1)
        m = jnp.max(s, axis=-1, keepdims=True)
        p = jnp.exp2(s - m).astype(BF16)
        o = _dot(p, va_ref[:(qb + 1) * blk, :])
        o_ref[rows, :] = (o[:, :HEAD_DIM] / o[:, HEAD_DIM:]).astype(o_ref.dtype)


def moba_mixer(h3, cos, sin_lo, sin_hi, col_q, col_k, col_v):
    bsz, seq, _ = h3.shape

    def col(off):
        return pl.BlockSpec((None, seq, HEAD_DIM), lambda b, h: (b, 0, off + h))

    table = pl.BlockSpec((None, seq, HEAD_DIM), lambda b, h: (b, 0, 0))
    return pl.pallas_call(
        _moba_kernel,
        out_shape=jax.ShapeDtypeStruct((bsz, seq, WIDTH), BF16),
        grid=(bsz, N_HEADS),
        in_specs=[col(col_q), col(col_k), col(col_v), table, table, table],
        out_specs=pl.BlockSpec((None, seq, HEAD_DIM), lambda b, h: (b, 0, h)),
        scratch_shapes=[pltpu.VMEM((seq, HEAD_DIM), BF16), pltpu.VMEM((seq, 2 * HEAD_DIM), BF16),
                        pltpu.VMEM((seq, 2 * HEAD_DIM), BF16), pltpu.VMEM((seq // MOBA_BLOCK, HEAD_DIM), F32)],
        compiler_params=_params("arbitrary", "arbitrary"),
        name="moba",
    )(h3, h3, h3, cos, sin_lo, sin_hi)


def _rglru_kernel(x_ref, cw_ref, cb_ref, wa_ref, ba_ref, wx_ref, bx_ref, lam_ref,
                  o_ref, a_ref, u_ref, xpad_ref):
    seq, width = x_ref.shape
    sub = 8
    groups = seq // sub

    xpad_ref[:sub, :] = jnp.zeros((sub, width), F32)
    xpad_ref[sub:, :] = x_ref[...].astype(F32)
    xf = cb_ref[...]
    for d in range(CONV_WIDTH):
        xf = xf + xpad_ref[sub - d:sub - d + seq, :] * cw_ref[CONV_WIDTH - 1 - d:CONV_WIDTH - d, :]

    xb = xf.astype(BF16)

    def gate(w_ref, b_ref):
        pre = [_dot(xb[:, j * HEAD_DIM:(j + 1) * HEAD_DIM], w_ref[j]) for j in range(width // HEAD_DIM)]
        return _sigmoid(jnp.concatenate(pre, axis=1) + b_ref[...])

    r = gate(wa_ref, ba_ref)
    ig = gate(wx_ref, bx_ref)
    nl = -lam_ref[...]
    softplus = jnp.maximum(nl, 0.0) + jnp.log1p(jnp.exp(-jnp.abs(nl)))
    a = jnp.exp2(r * ((-RG_C * LOG2E) * softplus))
    u = jnp.sqrt(1.0 - a * a) * ig * xf

    a3 = a.reshape(groups, sub, width)
    u3 = u.reshape(groups, sub, width)
    s_idx = lax.broadcasted_iota(jnp.int32, a3.shape, 1)
    for d in (1, 2, 4):
        keep = s_idx >= d
        a_prev = jnp.where(keep, pltpu.roll(a3, d, 1), 1.0)
        u_prev = jnp.where(keep, pltpu.roll(u3, d, 1), 0.0)
        u3 = u3 + a3 * u_prev
        a3 = a3 * a_prev
    a_ref[...] = a3
    u_ref[...] = u3

    def group_body(g, carry):
        hg = a_ref[g] * carry + u_ref[g]
        u_ref[g] = hg
        return jnp.broadcast_to(hg[sub - 1:sub, :], (sub, width))

    lax.fori_loop(0, groups, group_body, jnp.zeros((sub, width), F32), unroll=8)
    o_ref[...] = u_ref[...].reshape(seq, width).astype(o_ref.dtype)


def rglru_mixer(h3, conv_w, conv_b, wa, ba, wx, bx, lam, col_x, blocks_per_step=2):
    bsz, seq, _ = h3.shape
    width = blocks_per_step * HEAD_DIM

    def col(off):
        return pl.BlockSpec((None, seq, width), lambda b, c: (b, 0, off // blocks_per_step + c))

    def vec(rows):
        return pl.BlockSpec((rows, width), lambda b, c: (0, c))

    wspec = pl.BlockSpec((blocks_per_step, HEAD_DIM, HEAD_DIM), lambda b, c: (c, 0, 0))
    return pl.pallas_call(
        _rglru_kernel,
        out_shape=jax.ShapeDtypeStruct((bsz, seq, WIDTH), BF16),
        grid=(bsz, WIDTH // width),
        in_specs=[col(col_x), vec(CONV_WIDTH), vec(1), wspec, vec(1), wspec, vec(1), vec(1)],
        out_specs=pl.BlockSpec((None, seq, width), lambda b, c: (b, 0, c)),
        scratch_shapes=[pltpu.VMEM((seq // 8, 8, width), F32)] * 2 + [pltpu.VMEM((seq + 8, width), F32)],
        compiler_params=_params("arbitrary", "arbitrary"),
        name="rglru",
    )(h3, conv_w, conv_b.reshape(1, WIDTH), wa.astype(BF16), ba.reshape(1, WIDTH),
      wx.astype(BF16), bx.reshape(1, WIDTH), lam.reshape(1, WIDTH))


def _retention_kernel(q_ref, k_ref, v_ref, cos_ref, sin_ref, logg_ref, gain_ref, bias_ref,
                      o_ref, st_ref):
    seq = q_ref.shape[0]
    ck = RET_CHUNK
    half = RET_HALF
    nc = seq // ck // RET_PARTS

    def chunks(x):
        return x.reshape(nc, ck, HEAD_DIM)

    def bmm(spec, a, b):
        return jnp.einsum(spec, a, b, preferred_element_type=F32)

    log_g = logg_ref[...]
    ri = lax.broadcasted_iota(jnp.int32, (ck, ck), 0)
    ci = lax.broadcasted_iota(jnp.int32, (ck, ck), 1)
    diff = (ri - ci).astype(F32)
    dmask = jnp.where(ri >= ci, jnp.exp(jnp.maximum(diff, 0.0) * log_g[:, :1]), 0.0)
    idx = lax.broadcasted_iota(jnp.int32, (ck, HEAD_DIM), 0).astype(F32)
    q_fac = jnp.exp((idx + 1.0) * log_g)
    k_fac = jnp.exp((ck - 1.0 - idx) * log_g)
    chunk_decay = jnp.exp(float(ck) * log_g)

    carried = [jnp.zeros((HEAD_DIM, HEAD_DIM), F32)]

    def part(p):
        rows = slice(p * nc * ck, (p + 1) * nc * ck)
        cos = cos_ref[rows, :]
        sin = sin_ref[rows, :]
        q = q_ref[rows, :].astype(F32)
        k = k_ref[rows, :].astype(F32)
        q = chunks(q * cos + pltpu.roll(q, half, 1) * sin)
        k = chunks((k * cos + pltpu.roll(k, half, 1) * sin) * (HEAD_DIM ** -0.5))
        v = chunks(v_ref[rows, :])
        yield
        att = bmm('cid,cjd->cij', q.astype(BF16), k.astype(BF16)) * dmask[None]
        k_end_t = jnp.swapaxes(k * k_fac[None], 1, 2).astype(BF16)
        incr = bmm('ckn,cnv->ckv', k_end_t, v)
        yield
        o_intra = bmm('cij,cjd->cid', att.astype(BF16), v)
        state = carried[0]
        for c in range(nc):
            st_ref[p * nc + c] = state.astype(BF16)
            state = state * chunk_decay + incr[c]
        carried[0] = state
        yield
        o = o_intra + bmm('cnk,ckv->cnv', (q * q_fac[None]).astype(BF16), st_ref[p * nc:(p + 1) * nc])
        o = o.reshape(nc * ck, HEAD_DIM)
        mu = jnp.mean(o, axis=-1, keepdims=True)
        oc = o - mu
        var = jnp.mean(oc * oc, axis=-1, keepdims=True)
        o_ref[rows, :] = (oc * lax.rsqrt(var + EPS) * gain_ref[...] + bias_ref[...]).astype(o_ref.dtype)
        yield

    _run_staggered([part(p) for p in range(RET_PARTS)], n_stages=4)


def retention_mixer(h3, cos, sin, log_g, gain, bias, col_q, col_k, col_v):
    bsz, seq, _ = h3.shape

    def col(off):
        return pl.BlockSpec((None, seq, HEAD_DIM), lambda b, h: (b, 0, off + h))

    table = pl.BlockSpec((None, seq, HEAD_DIM), lambda b, h: (b, 0, 0))
    vec = pl.BlockSpec((None, 1, HEAD_DIM), lambda b, h: (h, 0, 0))
    return pl.pallas_call(
        _retention_kernel,
        out_shape=jax.ShapeDtypeStruct((bsz, seq, WIDTH), BF16),
        grid=(bsz, N_HEADS),
        in_specs=[col(col_q), col(col_k), col(col_v), table, table, vec, vec, vec],
        out_specs=pl.BlockSpec((None, seq, HEAD_DIM), lambda b, h: (b, 0, h)),
        scratch_shapes=[pltpu.VMEM((seq // RET_CHUNK, HEAD_DIM, HEAD_DIM), BF16)],
        compiler_params=_params("arbitrary", "arbitrary"),
        name="retention",
    )(h3, h3, h3, cos, sin, log_g,
      gain.reshape(N_HEADS, 1, HEAD_DIM), bias.reshape(N_HEADS, 1, HEAD_DIM))


def _post_kernel(x_ref, ya_ref, yb_ref, z_ref, woa_ref, wob_ref, g_ref, wq_ref, kv_ref, wo_ref, fg_ref,
                 o_ref, *, final):
    part_rows = x_ref.shape[0] // POST_PARTS
    scale = HEAD_DIM ** -0.5

    def gated(y_ref, rows, cols):
        return (y_ref[rows, :].astype(F32) * _silu(z_ref[rows, cols].astype(F32))).astype(BF16)

    def part(p):
        rows = slice(p * part_rows, (p + 1) * part_rows)
        ga = gated(ya_ref, rows, slice(0, WIDTH))
        yield
        gb = gated(yb_ref, rows, slice(WIDTH, 2 * WIDTH))
        x1 = x_ref[rows, :] + _dot(ga, woa_ref[...])
        yield
        x1 = x1 + _dot(gb, wob_ref[...])
        yield
        ms = jnp.mean(x1 * x1, axis=-1, keepdims=True)
        xn = (x1 * lax.rsqrt(ms + EPS) * g_ref[...]).astype(BF16)
        q = _dot(xn, wq_ref[...]).astype(BF16)
        yield
        head_cols = [slice(hd * HEAD_DIM, (hd + 1) * HEAD_DIM) for hd in range(XA_HEADS)]
        scores = [_dot_nt(q[:, cs], kv_ref[:, cs]) * scale for cs in head_cols]
        yield
        probs = []
        for s in scores:
            pr = jnp.exp(s - jnp.max(s, axis=-1, keepdims=True))
            probs.append((pr / jnp.sum(pr, axis=-1, keepdims=True)).astype(BF16))
        yield
        o = jnp.concatenate([_dot(pr, kv_ref[:, XA_WIDTH + cs.start:XA_WIDTH + cs.stop]).astype(BF16)
                             for pr, cs in zip(probs, head_cols)], axis=1)
        yield
        x2 = x1 + _dot(o, wo_ref[...])
        if final:
            ms2 = jnp.mean(x2 * x2, axis=-1, keepdims=True)
            x2 = x2 * lax.rsqrt(ms2 + EPS) * fg_ref[...]
        o_ref[rows, :] = x2
        yield

    _run_staggered([part(p) for p in range(POST_PARTS)], n_stages=8)


def post_mixer(x3, ya, yb, h3, z_block, w_out, xa_gain, wq, kv, wo, final_gain, final, tm=1024):
    bsz, seq, d = x3.shape
    full = lambda shape: pl.BlockSpec(shape, lambda b, i: (0,) * len(shape))
    tile = lambda w: pl.BlockSpec((None, tm, w), lambda b, i: (b, i, 0))
    return pl.pallas_call(
        functools.partial(_post_kernel, final=final),
        out_shape=jax.ShapeDtypeStruct((bsz, seq, d), F32),
        grid=(bsz, seq // tm),
        in_specs=[tile(d), tile(WIDTH), tile(WIDTH),
                  pl.BlockSpec((None, tm, 2 * WIDTH), lambda b, i: (b, i, z_block)),
                  full((WIDTH, d)), full((WIDTH, d)), full((1, d)), full((d, XA_WIDTH)),
                  pl.BlockSpec((None, MEM_LEN, 2 * XA_WIDTH), lambda b, i: (b, 0, 0)),
                  full((XA_WIDTH, d)), full((1, d))],
        out_specs=tile(d),
        compiler_params=_params("arbitrary", "arbitrary"),
        name="post_final" if final else "post",
    )(x3, ya, yb, h3, w_out[:WIDTH].astype(BF16), w_out[WIDTH:].astype(BF16), xa_gain.reshape(1, d),
      wq.astype(BF16), kv, wo.astype(BF16), final_gain.reshape(1, d))


def kernel(x, mem, positions, hgrn_lb_logits, ev_norm, ev_w_in, ev_hgrn_gain, ev_w_out, od_norm, od_w_in, od_conv_w, od_conv_b, od_rg_wa, od_rg_ba, od_rg_wx, od_rg_bx, od_rg_lambda, od_ret_gain, od_ret_bias, od_w_out, xa_norm, xa_mem_norm, xa_wq, xa_wkv, xa_wo, final_norm):
    bsz, seq, d = x.shape
    tokens = bsz * seq
    lb_all = jnp.cumsum(jax.nn.softmax(hgrn_lb_logits.astype(F32), axis=0), axis=0)
    log_g = jnp.log(1.0 - 2.0 ** (-5.0 - jnp.arange(N_HEADS, dtype=F32)))
    log_g = jnp.broadcast_to(log_g[:, None, None], (N_HEADS, 1, HEAD_DIM))
    mem2 = mem.reshape(bsz * MEM_LEN, d)

    def memory_kv(layer):
        kv = norm_matmul(mem2, xa_mem_norm[layer], xa_wkv[layer], BF16,
                         tm=bsz * MEM_LEN // 2, tn=2 * XA_WIDTH, name="mem_kv")
        return kv.reshape(bsz, MEM_LEN, 2 * XA_WIDTH)

    h, *tables = project_with_rope_tables(x.reshape(tokens, d), ev_norm[0], ev_w_in.reshape(ev_w_in.shape[1:]),
                                          positions, tm=seq, tn=1024)
    ret_cos, ret_sin, moba_cos, moba_lo, moba_hi = (tb.reshape(bsz, seq, HEAD_DIM) for tb in tables)
    h3 = h.reshape(bsz, seq, -1)
    nb = WIDTH // HEAD_DIM
    ya = hgrn2_mixer(h3, lb_all[0], ev_hgrn_gain[0], 0, nb, 2 * nb)
    yb = moba_mixer(h3, moba_cos, moba_lo, moba_hi, 3 * nb, 4 * nb, 5 * nb)
    x1 = post_mixer(x, ya, yb, h3, 3, ev_w_out[0], xa_norm[0], xa_wq[0], memory_kv(0), xa_wo[0],
                    final_norm, final=False)

    h = norm_matmul(x1.reshape(tokens, d), od_norm[0], od_w_in.reshape(od_w_in.shape[1:]), BF16,
                    tm=2048, tn=1024, name="in_proj_odd")
    h3 = h.reshape(bsz, seq, -1)
    yc = rglru_mixer(h3, od_conv_w[0], od_conv_b[0], od_rg_wa[0], od_rg_ba[0], od_rg_wx[0],
                     od_rg_bx[0], od_rg_lambda[0], 0)
    yd = retention_mixer(h3, ret_cos, ret_sin, log_g, od_ret_gain[0], od_ret_bias[0],
                         nb, 2 * nb, 3 * nb)
    return post_mixer(x1, yc, yd, h3, 2, od_w_out[0], xa_norm[1], xa_wq[1], memory_kv(1), xa_wo[1],
                      final_norm, final=True)
```

```python
import functools
import math

import jax
import jax.numpy as jnp
from jax import lax
from jax.experimental import pallas as pl
from jax.experimental.pallas import tpu as pltpu

F32 = jnp.float32
BF16 = jnp.bfloat16

D_MODEL = 1024
HEAD_DIM = 128
N_HEADS = 8
WIDTH = N_HEADS * HEAD_DIM
MEM_LEN = 256
HGRN_CHUNK = 32
HGRN_TILE = 128
HGRN_PARTS = 16
MOBA_BLOCK = 256
MOBA_TOPK = 3
RET_CHUNK = 256
RET_PARTS = 2
CONV_WIDTH = 4
RG_C = 8.0
ROPE_THETA = 500000.0
ROPE_DIM = HEAD_DIM // 4
RET_THETA = 10000.0
PROJ_PIECES = 4
POST_PARTS = 4
XA_HEADS = 4
XA_WIDTH = XA_HEADS * HEAD_DIM
EPS = 1e-6
VMEM_LIMIT = 56 * 1024 * 1024
MASKED = -1e30

_NT = (((1,), (1,)), ((), ()))


def _dot(a, b):
    return jnp.dot(a, b, preferred_element_type=F32)


def _dot_nt(a, b):
    return lax.dot_general(a, b, _NT, preferred_element_type=F32)


LOG2E = math.log2(math.e)


def _sigmoid(x):
    return 1.0 / (1.0 + jnp.exp2(x * (-LOG2E)))


def _silu(x):
    return x * _sigmoid(x)


def _run_staggered(parts, n_stages):
    for step in range(n_stages + len(parts) - 1):
        for p, gen in enumerate(parts):
            if 0 <= step - p < n_stages:
                next(gen)


def _params(*sem):
    return pltpu.CompilerParams(dimension_semantics=sem, vmem_limit_bytes=VMEM_LIMIT)


def _norm_matmul_pieces(x_ref, g_ref, w_ref, o_ref, xn_ref, between=None):
    rows = x_ref.shape[0] // PROJ_PIECES

    def run(with_norm):
        w = w_ref[...].astype(BF16)
        for c in range(PROJ_PIECES):
            piece = slice(c * rows, (c + 1) * rows)
            if with_norm:
                x = x_ref[piece, :]
                ms = jnp.mean(x * x, axis=-1, keepdims=True)
                xn_ref[piece, :] = (x * lax.rsqrt(ms + EPS) * g_ref[...]).astype(BF16)
            o_ref[piece, :] = _dot(xn_ref[piece, :], w).astype(o_ref.dtype)
            if between is not None:
                between(c)

    @pl.when(pl.program_id(1) == 0)
    def _():
        run(True)

    @pl.when(pl.program_id(1) != 0)
    def _():
        run(False)


def _norm_matmul_kernel(x_ref, g_ref, w_ref, o_ref, xn_ref):
    _norm_matmul_pieces(x_ref, g_ref, w_ref, o_ref, xn_ref)


def norm_matmul(x, gain, w, out_dtype, tm, tn, name):
    t, d = x.shape
    n = w.shape[1]
    return pl.pallas_call(
        _norm_matmul_kernel,
        out_shape=jax.ShapeDtypeStruct((t, n), out_dtype),
        grid=(t // tm, n // tn),
        in_specs=[
            pl.BlockSpec((tm, d), lambda i, j: (i, 0)),
            pl.BlockSpec((1, d), lambda i, j: (0, 0)),
            pl.BlockSpec((d, tn), lambda i, j: (0, j)),
        ],
        out_specs=pl.BlockSpec((tm, tn), lambda i, j: (i, j)),
        scratch_shapes=[pltpu.VMEM((tm, d), BF16)],
        compiler_params=_params("arbitrary", "arbitrary"),
        name=name,
    )(x, gain.reshape(1, d), w)


def _split2(x):
    hi = x.astype(BF16)
    lo = (x - hi.astype(F32)).astype(BF16)
    return hi, lo


def _hgrn2_kernel(q_ref, f_ref, i_ref, lb_ref, gain_ref, o_ref, st_ref):
    seq = q_ref.shape[0]
    tile, chunk = HGRN_TILE, HGRN_CHUNK
    per_tile = tile // chunk
    part_rows = seq // HGRN_PARTS
    nt = part_rows // tile
    lb = lb_ref[...]
    gain = gain_ref[...]

    def tiles(x):
        return x.reshape(nt, tile, HEAD_DIM)

    def bmm(spec, a, b):
        return jnp.einsum(spec, a, b, preferred_element_type=F32)

    ti = lax.broadcasted_iota(jnp.int32, (tile, tile), 0)
    tj = lax.broadcasted_iota(jnp.int32, (tile, tile), 1)
    intra_mask = (((ti // chunk) == (tj // chunk)) & (tj <= ti))[None]
    sum_mat = jnp.broadcast_to(jnp.where(intra_mask, 1.0, 0.0).astype(BF16), (nt, tile, tile))
    row_chunk = (lax.broadcasted_iota(jnp.int32, (tile, HEAD_DIM), 0) // chunk)[None]

    def per_chunk_lanes(x):
        zero = jnp.zeros_like(x)
        return jnp.concatenate([jnp.where(row_chunk == ci, x, zero) for ci in range(per_tile)], axis=-1)

    carried = [jnp.zeros((HEAD_DIM, HEAD_DIM), F32)]

    def part(p):
        rows = slice(p * part_rows, (p + 1) * part_rows)
        f = lb + (1.0 - lb) * _sigmoid(f_ref[rows, :].astype(F32))
        k = tiles(1.0 - f)
        hi, lo = _split2(jnp.log2(f))
        yield
        b = bmm('trc,tcd->trd', sum_mat, tiles(hi)) + bmm('trc,tcd->trd', sum_mat, tiles(lo))
        b_chunks = b.reshape(part_rows // chunk, chunk, HEAD_DIM)
        b_tot = jnp.broadcast_to(b_chunks[:, chunk - 1:, :], b_chunks.shape).reshape(b.shape)
        yield
        q_dec = (tiles(q_ref[rows, :].astype(F32)) * jnp.exp2(b)).astype(BF16)
        k_inv = (k * jnp.exp2(-b)).astype(BF16)
        k_end = (k * jnp.exp2(b_tot - b)).astype(BF16)
        v = tiles(i_ref[rows, :].astype(F32))
        yield
        att = bmm('tid,tjd->tij', q_dec, k_inv)
        att = jnp.where(intra_mask, att, 0.0).astype(BF16)
        v_t = jnp.swapaxes(v, 1, 2).astype(BF16)
        incr = bmm('tvn,tnk->tvk', v_t, per_chunk_lanes(k_end))
        yield
        o_intra = bmm('tij,tjd->tid', att, v.astype(BF16))
        state_t = carried[0]
        for t in range(nt):
            for ci in range(per_tile):
                lanes = slice(ci * HEAD_DIM, (ci + 1) * HEAD_DIM)
                st_ref[p * nt + t, :, lanes] = state_t.astype(BF16)
                decay = jnp.exp2(b_tot[t, ci * chunk:ci * chunk + 1, :])
                state_t = state_t * decay + incr[t, :, lanes]
        carried[0] = state_t
        yield
        o = o_intra + bmm('tnk,tvk->tnv', per_chunk_lanes(q_dec), st_ref[p * nt:(p + 1) * nt])
        o = o.reshape(part_rows, HEAD_DIM)
        y = o * lax.rsqrt(jnp.mean(o * o, axis=-1, keepdims=True) + EPS) * gain
        o_ref[rows, :] = y.astype(o_ref.dtype)
        yield

    _run_staggered([part(p) for p in range(HGRN_PARTS)], n_stages=6)


def hgrn2_mixer(h3, lb, gain, col_q, col_f, col_i):
    bsz, seq, _ = h3.shape

    def col(off):
        return pl.BlockSpec((None, seq, HEAD_DIM), lambda b, h: (b, 0, off + h))

    vec = pl.BlockSpec((None, 1, HEAD_DIM), lambda b, h: (h, 0, 0))
    return pl.pallas_call(
        _hgrn2_kernel,
        out_shape=jax.ShapeDtypeStruct((bsz, seq, WIDTH), BF16),
        grid=(bsz, N_HEADS),
        in_specs=[col(col_q), col(col_f), col(col_i), vec, vec],
        out_specs=pl.BlockSpec((None, seq, HEAD_DIM), lambda b, h: (b, 0, h)),
        scratch_shapes=[pltpu.VMEM((seq // HGRN_TILE, HEAD_DIM, HGRN_TILE // HGRN_CHUNK * HEAD_DIM), BF16)],
        compiler_params=_params("arbitrary", "arbitrary"),
        name="hgrn2",
    )(h3, h3, h3, lb.reshape(N_HEADS, 1, HEAD_DIM), gain.reshape(N_HEADS, 1, HEAD_DIM))


RET_HALF = HEAD_DIM // 2
MOBA_HALF = ROPE_DIM // 2


def _rope_table_kernel(pos_ref, inv_ref, ret_cos_ref, ret_sin_ref, moba_cos_ref, moba_lo_ref, moba_hi_ref):
    ang = pos_ref[...] * inv_ref[...]
    cos = jnp.cos(ang)
    sin = jnp.sin(ang)
    lane = lax.broadcasted_iota(jnp.int32, ang.shape, 1)
    low = lane < RET_HALF
    cos_up = pltpu.roll(cos, RET_HALF, 1)
    sin_up = pltpu.roll(sin, RET_HALF, 1)
    ret_cos_ref[...] = jnp.where(low, cos, cos_up)
    ret_sin_ref[...] = jnp.where(low, -sin, sin_up)
    first = lane < MOBA_HALF
    second = (lane >= MOBA_HALF) & (lane < 2 * MOBA_HALF)
    moba_cos_ref[...] = jnp.where(first, cos_up, jnp.where(second, pltpu.roll(cos_up, MOBA_HALF, 1), 1.0))
    moba_lo_ref[...] = jnp.where(first, -sin_up, 0.0)
    moba_hi_ref[...] = jnp.where(second, pltpu.roll(sin_up, MOBA_HALF, 1), 0.0)


def _proj_tables_kernel(x_ref, g_ref, w_ref, pos_ref, inv_ref, o_ref, *table_and_scratch):
    table_refs, xn_ref = table_and_scratch[:-1], table_and_scratch[-1]
    trows = pos_ref.shape[0] // PROJ_PIECES

    def table_piece(c):
        piece = slice(c * trows, (c + 1) * trows)
        _rope_table_kernel(pos_ref.at[piece, :], inv_ref, *(r.at[piece, :] for r in table_refs))

    _norm_matmul_pieces(x_ref, g_ref, w_ref, o_ref, xn_ref, between=table_piece)


def project_with_rope_tables(x, gain, w, positions, tm, tn):
    t, d = x.shape
    n = w.shape[1]
    n_steps = n // tn
    rows = tm // n_steps
    inv_ret = RET_THETA ** (-jnp.arange(RET_HALF, dtype=F32) / RET_HALF)
    inv_moba = ROPE_THETA ** (-jnp.arange(MOBA_HALF, dtype=F32) / MOBA_HALF)
    inv = jnp.concatenate([inv_ret, inv_moba, jnp.zeros((HEAD_DIM - RET_HALF - MOBA_HALF,), F32)])
    posf = jnp.broadcast_to(positions.astype(F32).reshape(t, 1), (t, HEAD_DIM))
    table_spec = pl.BlockSpec((rows, HEAD_DIM), lambda i, j: (i * n_steps + j, 0))
    return pl.pallas_call(
        _proj_tables_kernel,
        out_shape=[jax.ShapeDtypeStruct((t, n), BF16)] + [jax.ShapeDtypeStruct((t, HEAD_DIM), F32)] * 5,
        grid=(t // tm, n_steps),
        in_specs=[
            pl.BlockSpec((tm, d), lambda i, j: (i, 0)),
            pl.BlockSpec((1, d), lambda i, j: (0, 0)),
            pl.BlockSpec((d, tn), lambda i, j: (0, j)),
            table_spec,
            pl.BlockSpec((1, HEAD_DIM), lambda i, j: (0, 0)),
        ],
        out_specs=[pl.BlockSpec((tm, tn), lambda i, j: (i, j))] + [table_spec] * 5,
        scratch_shapes=[pltpu.VMEM((tm, d), BF16)],
        compiler_params=_params("arbitrary", "arbitrary"),
        name="in_proj_even",
    )(x, gain.reshape(1, d), w, posf, inv.reshape(1, HEAD_DIM))


def _moba_kernel(q_ref, k_ref, v_ref, cos_ref, sin_lo_ref, sin_hi_ref, o_ref,
                 qr_ref, ka_ref, va_ref, kmean_ref):
    seq = q_ref.shape[0]
    blk = MOBA_BLOCK
    nb = seq // blk
    exp2_scale = HEAD_DIM ** -0.5 * LOG2E

    row = lax.broadcasted_iota(jnp.int32, (blk, blk), 0)
    colm = lax.broadcasted_iota(jnp.int32, (blk, blk), 1)
    causal = colm <= row
    eye = jnp.where(colm == row, 1.0, 0.0).astype(BF16)
    blk_row = lax.broadcasted_iota(jnp.int32, (nb, blk), 0)
    lane = lax.broadcasted_iota(jnp.int32, (blk, HEAD_DIM), 1)
    kmean_ref[...] = jnp.zeros(kmean_ref.shape, kmean_ref.dtype)

    def prepare(qb):
        rows = slice(qb * blk, (qb + 1) * blk)

        def rope(x):
            return (x * cos_ref[rows, :]
                    + pltpu.roll(x, HEAD_DIM - MOBA_HALF, 1) * sin_lo_ref[rows, :]
                    + pltpu.roll(x, MOBA_HALF, 1) * sin_hi_ref[rows, :])

        kr = rope(k_ref[rows, :].astype(F32))
        kmean_ref[qb:qb + 1, :] = jnp.mean(kr, axis=0, keepdims=True)
        ka_ref[rows, :HEAD_DIM] = kr.astype(BF16)
        ka_ref[rows, HEAD_DIM:] = jnp.where(lane == qb, MASKED, 0.0).astype(BF16)
        qr_ref[rows, :] = (rope(q_ref[rows, :].astype(F32)) * exp2_scale).astype(BF16)
        va_ref[rows, :HEAD_DIM] = v_ref[rows, :]
        va_ref[rows, HEAD_DIM:] = jnp.ones((blk, HEAD_DIM), BF16)

    def scores(qb):
        rows = slice(qb * blk, (qb + 1) * blk)
        q = qr_ref[rows, :]
        n_keys = (qb + 1) * blk
        if qb > MOBA_TOPK:
            gate = _dot_nt(kmean_ref[...].astype(BF16), q)
            rank = jnp.zeros((nb, blk), F32)
            for j in range(qb):
                gj = gate[j:j + 1, :]
                ahead = (gj > gate) | ((gj == gate) & (blk_row > j))
                rank = rank + jnp.where(ahead, 1.0, 0.0)
            drop = jnp.where((rank >= float(MOBA_TOPK)) & (blk_row < qb), 1.0, 0.0)
            drop = jnp.concatenate([drop, jnp.zeros((HEAD_DIM - nb, blk), F32)], axis=0).astype(BF16)
            drop_col = _dot_nt(eye, drop).astype(BF16)
            s = _dot_nt(jnp.concatenate([q, drop_col], axis=1), ka_ref[:n_keys, :])
        else:
            s = _dot_nt(q, ka_ref[:n_keys, :HEAD_DIM])
        own = jnp.where(causal, s[:, qb * blk:], MASKED)
        return jnp.concatenate([s[:, :qb * blk], own], axis=1) if qb else own

    prepare(0)
    s_next = scores(0)
    for qb in range(nb):
        rows = slice(qb * blk, (qb + 1) * blk)
        s = s_next
        if qb + 1 < nb:
            prepare(qb + 1)
            s_next = scores(qb + 1)
        m = jnp.max(s, axis=-1, keepdims=True)
        p = jnp.exp2(s - m).astype(BF16)
        o = _dot(p, va_ref[:(qb + 1) * blk, :])
        o_ref[rows, :] = (o[:, :HEAD_DIM] / o[:, HEAD_DIM:]).astype(o_ref.dtype)


def moba_mixer(h3, cos, sin_lo, sin_hi, col_q, col_k, col_v):
    bsz, seq, _ = h3.shape

    def col(off):
        return pl.BlockSpec((None, seq, HEAD_DIM), lambda b, h: (b, 0, off + h))

    table = pl.BlockSpec((None, seq, HEAD_DIM), lambda b, h: (b, 0, 0))
    return pl.pallas_call(
        _moba_kernel,
        out_shape=jax.ShapeDtypeStruct((bsz, seq, WIDTH), BF16),
        grid=(bsz, N_HEADS),
        in_specs=[col(col_q), col(col_k), col(col_v), table, table, table],
        out_specs=pl.BlockSpec((None, seq, HEAD_DIM), lambda b, h: (b, 0, h)),
        scratch_shapes=[pltpu.VMEM((seq, HEAD_DIM), BF16), pltpu.VMEM((seq, 2 * HEAD_DIM), BF16),
                        pltpu.VMEM((seq, 2 * HEAD_DIM), BF16), pltpu.VMEM((seq // MOBA_BLOCK, HEAD_DIM), F32)],
        compiler_params=_params("arbitrary", "arbitrary"),
        name="moba",
    )(h3, h3, h3, cos, sin_lo, sin_hi)


def _rglru_kernel(x_ref, cw_ref, cb_ref, wa_ref, ba_ref, wx_ref, bx_ref, lam_ref,
                  o_ref, a_ref, u_ref, xpad_ref):
    seq, width = x_ref.shape
    sub = 8
    groups = seq // sub

    xpad_ref[:sub, :] = jnp.zeros((sub, width), F32)
    xpad_ref[sub:, :] = x_ref[...].astype(F32)
    xf = cb_ref[...]
    for d in range(CONV_WIDTH):
        xf = xf + xpad_ref[sub - d:sub - d + seq, :] * cw_ref[CONV_WIDTH - 1 - d:CONV_WIDTH - d, :]

    xb = xf.astype(BF16)

    def gate(w_ref, b_ref):
        pre = [_dot(xb[:, j * HEAD_DIM:(j + 1) * HEAD_DIM], w_ref[j]) for j in range(width // HEAD_DIM)]
        return _sigmoid(jnp.concatenate(pre, axis=1) + b_ref[...])

    r = gate(wa_ref, ba_ref)
    ig = gate(wx_ref, bx_ref)
    nl = -lam_ref[...]
    softplus = jnp.maximum(nl, 0.0) + jnp.log1p(jnp.exp(-jnp.abs(nl)))
    a = jnp.exp2(r * ((-RG_C * LOG2E) * softplus))
    u = jnp.sqrt(1.0 - a * a) * ig * xf

    a3 = a.reshape(groups, sub, width)
    u3 = u.reshape(groups, sub, width)
    s_idx = lax.broadcasted_iota(jnp.int32, a3.shape, 1)
    for d in (1, 2, 4):
        keep = s_idx >= d
        a_prev = jnp.where(keep, pltpu.roll(a3, d, 1), 1.0)
        u_prev = jnp.where(keep, pltpu.roll(u3, d, 1), 0.0)
        u3 = u3 + a3 * u_prev
        a3 = a3 * a_prev
    a_ref[...] = a3
    u_ref[...] = u3

    def group_body(g, carry):
        hg = a_ref[g] * carry + u_ref[g]
        u_ref[g] = hg
        return jnp.broadcast_to(hg[sub - 1:sub, :], (sub, width))

    lax.fori_loop(0, groups, group_body, jnp.zeros((sub, width), F32), unroll=8)
    o_ref[...] = u_ref[...].reshape(seq, width).astype(o_ref.dtype)


def rglru_mixer(h3, conv_w, conv_b, wa, ba, wx, bx, lam, col_x, blocks_per_step=2):
    bsz, seq, _ = h3.shape
    width = blocks_per_step * HEAD_DIM

    def col(off):
        return pl.BlockSpec((None, seq, width), lambda b, c: (b, 0, off // blocks_per_step + c))

    def vec(rows):
        return pl.BlockSpec((rows, width), lambda b, c: (0, c))

    wspec = pl.BlockSpec((blocks_per_step, HEAD_DIM, HEAD_DIM), lambda b, c: (c, 0, 0))
    return pl.pallas_call(
        _rglru_kernel,
        out_shape=jax.ShapeDtypeStruct((bsz, seq, WIDTH), BF16),
        grid=(bsz, WIDTH // width),
        in_specs=[col(col_x), vec(CONV_WIDTH), vec(1), wspec, vec(1), wspec, vec(1), vec(1)],
        out_specs=pl.BlockSpec((None, seq, width), lambda b, c: (b, 0, c)),
        scratch_shapes=[pltpu.VMEM((seq // 8, 8, width), F32)] * 2 + [pltpu.VMEM((seq + 8, width), F32)],
        compiler_params=_params("arbitrary", "arbitrary"),
        name="rglru",
    )(h3, conv_w, conv_b.reshape(1, WIDTH), wa.astype(BF16), ba.reshape(1, WIDTH),
      wx.astype(BF16), bx.reshape(1, WIDTH), lam.reshape(1, WIDTH))


def _retention_kernel(q_ref, k_ref, v_ref, cos_ref, sin_ref, logg_ref, gain_ref, bias_ref,
                      o_ref, st_ref):
    seq = q_ref.shape[0]
    ck = RET_CHUNK
    half = RET_HALF
    nc = seq // ck // RET_PARTS

    def chunks(x):
        return x.reshape(nc, ck, HEAD_DIM)

    def bmm(spec, a, b):
        return jnp.einsum(spec, a, b, preferred_element_type=F32)

    log_g = logg_ref[...]
    ri = lax.broadcasted_iota(jnp.int32, (ck, ck), 0)
    ci = lax.broadcasted_iota(jnp.int32, (ck, ck), 1)
    diff = (ri - ci).astype(F32)
    dmask = jnp.where(ri >= ci, jnp.exp(jnp.maximum(diff, 0.0) * log_g[:, :1]), 0.0)
    idx = lax.broadcasted_iota(jnp.int32, (ck, HEAD_DIM), 0).astype(F32)
    q_fac = jnp.exp((idx + 1.0) * log_g)
    k_fac = jnp.exp((ck - 1.0 - idx) * log_g)
    chunk_decay = jnp.exp(float(ck) * log_g)

    carried = [jnp.zeros((HEAD_DIM, HEAD_DIM), F32)]

    def part(p):
        rows = slice(p * nc * ck, (p + 1) * nc * ck)
        cos = cos_ref[rows, :]
        sin = sin_ref[rows, :]
        q = q_ref[rows, :].astype(F32)
        k = k_ref[rows, :].astype(F32)
        q = chunks(q * cos + pltpu.roll(q, half, 1) * sin)
        k = chunks((k * cos + pltpu.roll(k, half, 1) * sin) * (HEAD_DIM ** -0.5))
        v = chunks(v_ref[rows, :])
        yield
        att = bmm('cid,cjd->cij', q.astype(BF16), k.astype(BF16)) * dmask[None]
        k_end_t = jnp.swapaxes(k * k_fac[None], 1, 2).astype(BF16)
        incr = bmm('ckn,cnv->ckv', k_end_t, v)
        yield
        o_intra = bmm('cij,cjd->cid', att.astype(BF16), v)
        state = carried[0]
        for c in range(nc):
            st_ref[p * nc + c] = state.astype(BF16)
            state = state * chunk_decay + incr[c]
        carried[0] = state
        yield
        o = o_intra + bmm('cnk,ckv->cnv', (q * q_fac[None]).astype(BF16), st_ref[p * nc:(p + 1) * nc])
        o = o.reshape(nc * ck, HEAD_DIM)
        mu = jnp.mean(o, axis=-1, keepdims=True)
        oc = o - mu
        var = jnp.mean(oc * oc, axis=-1, keepdims=True)
        o_ref[rows, :] = (oc * lax.rsqrt(var + EPS) * gain_ref[...] + bias_ref[...]).astype(o_ref.dtype)
        yield

    _run_staggered([part(p) for p in range(RET_PARTS)], n_stages=4)


def retention_mixer(h3, cos, sin, log_g, gain, bias, col_q, col_k, col_v):
    bsz, seq, _ = h3.shape

    def col(off):
        return pl.BlockSpec((None, seq, HEAD_DIM), lambda b, h: (b, 0, off + h))

    table = pl.BlockSpec((None, seq, HEAD_DIM), lambda b, h: (b, 0, 0))
    vec = pl.BlockSpec((None, 1, HEAD_DIM), lambda b, h: (h, 0, 0))
    return pl.pallas_call(
        _retention_kernel,
        out_shape=jax.ShapeDtypeStruct((bsz, seq, WIDTH), BF16),
        grid=(bsz, N_HEADS),
        in_specs=[col(col_q), col(col_k), col(col_v), table, table, vec, vec, vec],
        out_specs=pl.BlockSpec((None, seq, HEAD_DIM), lambda b, h: (b, 0, h)),
        scratch_shapes=[pltpu.VMEM((seq // RET_CHUNK, HEAD_DIM, HEAD_DIM), BF16)],
        compiler_params=_params("arbitrary", "arbitrary"),
        name="retention",
    )(h3, h3, h3, cos, sin, log_g,
      gain.reshape(N_HEADS, 1, HEAD_DIM), bias.reshape(N_HEADS, 1, HEAD_DIM))


def _post_kernel(x_ref, ya_ref, yb_ref, z_ref, woa_ref, wob_ref, g_ref, wq_ref, kv_ref, wo_ref, fg_ref,
                 o_ref, *, final):
    part_rows = x_ref.shape[0] // POST_PARTS
    scale = HEAD_DIM ** -0.5

    def gated(y_ref, rows, cols):
        return (y_ref[rows, :].astype(F32) * _silu(z_ref[rows, cols].astype(F32))).astype(BF16)

    def part(p):
        rows = slice(p * part_rows, (p + 1) * part_rows)
        ga = gated(ya_ref, rows, slice(0, WIDTH))
        yield
        gb = gated(yb_ref, rows, slice(WIDTH, 2 * WIDTH))
        x1 = x_ref[rows, :] + _dot(ga, woa_ref[...])
        yield
        x1 = x1 + _dot(gb, wob_ref[...])
        yield
        ms = jnp.mean(x1 * x1, axis=-1, keepdims=True)
        xn = (x1 * lax.rsqrt(ms + EPS) * g_ref[...]).astype(BF16)
        q = _dot(xn, wq_ref[...]).astype(BF16)
        yield
        head_cols = [slice(hd * HEAD_DIM, (hd + 1) * HEAD_DIM) for hd in range(XA_HEADS)]
        scores = [_dot_nt(q[:, cs], kv_ref[:, cs]) * scale for cs in head_cols]
        yield
        probs = []
        for s in scores:
            pr = jnp.exp(s - jnp.max(s, axis=-1, keepdims=True))
            probs.append((pr / jnp.sum(pr, axis=-1, keepdims=True)).astype(BF16))
        yield
        o = jnp.concatenate([_dot(pr, kv_ref[:, XA_WIDTH + cs.start:XA_WIDTH + cs.stop]).astype(BF16)
                             for pr, cs in zip(probs, head_cols)], axis=1)
        yield
        x2 = x1 + _dot(o, wo_ref[...])
        if final:
            ms2 = jnp.mean(x2 * x2, axis=-1, keepdims=True)
            x2 = x2 * lax.rsqrt(ms2 + EPS) * fg_ref[...]
        o_ref[rows, :] = x2
        yield

    _run_staggered([part(p) for p in range(POST_PARTS)], n_stages=8)


def post_mixer(x3, ya, yb, h3, z_block, w_out, xa_gain, wq, kv, wo, final_gain, final, tm=1024):
    bsz, seq, d = x3.shape
    full = lambda shape: pl.BlockSpec(shape, lambda b, i: (0,) * len(shape))
    tile = lambda w: pl.BlockSpec((None, tm, w), lambda b, i: (b, i, 0))
    return pl.pallas_call(
        functools.partial(_post_kernel, final=final),
        out_shape=jax.ShapeDtypeStruct((bsz, seq, d), F32),
        grid=(bsz, seq // tm),
        in_specs=[tile(d), tile(WIDTH), tile(WIDTH),
                  pl.BlockSpec((None, tm, 2 * WIDTH), lambda b, i: (b, i, z_block)),
                  full((WIDTH, d)), full((WIDTH, d)), full((1, d)), full((d, XA_WIDTH)),
                  pl.BlockSpec((None, MEM_LEN, 2 * XA_WIDTH), lambda b, i: (b, 0, 0)),
                  full((XA_WIDTH, d)), full((1, d))],
        out_specs=tile(d),
        compiler_params=_params("arbitrary", "arbitrary"),
        name="post_final" if final else "post",
    )(x3, ya, yb, h3, w_out[:WIDTH].astype(BF16), w_out[WIDTH:].astype(BF16), xa_gain.reshape(1, d),
      wq.astype(BF16), kv, wo.astype(BF16), final_gain.reshape(1, d))


def kernel(x, mem, positions, hgrn_lb_logits, ev_norm, ev_w_in, ev_hgrn_gain, ev_w_out, od_norm, od_w_in, od_conv_w, od_conv_b, od_rg_wa, od_rg_ba, od_rg_wx, od_rg_bx, od_rg_lambda, od_ret_gain, od_ret_bias, od_w_out, xa_norm, xa_mem_norm, xa_wq, xa_wkv, xa_wo, final_norm):
    bsz, seq, d = x.shape
    tokens = bsz * seq
    lb_all = jnp.cumsum(jax.nn.softmax(hgrn_lb_logits.astype(F32), axis=0), axis=0)
    log_g = jnp.log(1.0 - 2.0 ** (-5.0 - jnp.arange(N_HEADS, dtype=F32)))
    log_g = jnp.broadcast_to(log_g[:, None, None], (N_HEADS, 1, HEAD_DIM))
    mem2 = mem.reshape(bsz * MEM_LEN, d)

    def memory_kv(layer):
        kv = norm_matmul(mem2, xa_mem_norm[layer], xa_wkv[layer], BF16,
                         tm=bsz * MEM_LEN // 2, tn=2 * XA_WIDTH, name="mem_kv")
        return kv.reshape(bsz, MEM_LEN, 2 * XA_WIDTH)

    h, *tables = project_with_rope_tables(x.reshape(tokens, d), ev_norm[0], ev_w_in.reshape(ev_w_in.shape[1:]),
                                          positions, tm=seq, tn=1024)
    ret_cos, ret_sin, moba_cos, moba_lo, moba_hi = (tb.reshape(bsz, seq, HEAD_DIM) for tb in tables)
    h3 = h.reshape(bsz, seq, -1)
    nb = WIDTH // HEAD_DIM
    ya = hgrn2_mixer(h3, lb_all[0], ev_hgrn_gain[0], 0, nb, 2 * nb)
    yb = moba_mixer(h3, moba_cos, moba_lo, moba_hi, 3 * nb, 4 * nb, 5 * nb)
    x1 = post_mixer(x, ya, yb, h3, 3, ev_w_out[0], xa_norm[0], xa_wq[0], memory_kv(0), xa_wo[0],
                    final_norm, final=False)

    h = norm_matmul(x1.reshape(tokens, d), od_norm[0], od_w_in.reshape(od_w_in.shape[1:]), BF16,
                    tm=2048, tn=1024, name="in_proj_odd")
    h3 = h.reshape(bsz, seq, -1)
    yc = rglru_mixer(h3, od_conv_w[0], od_conv_b[0], od_rg_wa[0], od_rg_ba[0], od_rg_wx[0],
                     od_rg_bx[0], od_rg_lambda[0], 0)
    yd = retention_mixer(h3, ret_cos, ret_sin, log_g, od_ret_gain[0], od_ret_bias[0],
                         nb, 2 * nb, 3 * nb)
    return post_mixer(x1, yc, yd, h3, 2, od_w_out[0], xa_norm[1], xa_wq[1], memory_kv(1), xa_wo[1],
                      final_norm, final=True)
```

```python
import functools
import math

import jax
import jax.numpy as jnp
from jax import lax
from jax.experimental import pallas as pl
from jax.experimental.pallas import tpu as pltpu

F32 = jnp.float32
BF16 = jnp.bfloat16

D_MODEL = 1024
HEAD_DIM = 128
N_HEADS = 8
WIDTH = N_HEADS * HEAD_DIM
MEM_LEN = 256
HGRN_CHUNK = 32
HGRN_TILE = 128
HGRN_PARTS = 16
MOBA_BLOCK = 256
MOBA_TOPK = 3
RET_CHUNK = 256
RET_PARTS = 2
CONV_WIDTH = 4
RG_C = 8.0
ROPE_THETA = 500000.0
ROPE_DIM = HEAD_DIM // 4
RET_THETA = 10000.0
PROJ_PIECES = 4
POST_PARTS = 4
XA_HEADS = 4
XA_WIDTH = XA_HEADS * HEAD_DIM
EPS = 1e-6
VMEM_LIMIT = 56 * 1024 * 1024
MASKED = -1e30

_NT = (((1,), (1,)), ((), ()))


def _dot(a, b):
    return jnp.dot(a, b, preferred_element_type=F32)


def _dot_nt(a, b):
    return lax.dot_general(a, b, _NT, preferred_element_type=F32)


LOG2E = math.log2(math.e)


def _sigmoid(x):
    return 1.0 / (1.0 + jnp.exp2(x * (-LOG2E)))


def _silu(x):
    return x * _sigmoid(x)


def _run_staggered(parts, n_stages):
    for step in range(n_stages + len(parts) - 1):
        for p, gen in enumerate(parts):
            if 0 <= step - p < n_stages:
                next(gen)


def _params(*sem):
    return pltpu.CompilerParams(dimension_semantics=sem, vmem_limit_bytes=VMEM_LIMIT)


def _norm_matmul_pieces(x_ref, g_ref, w_ref, o_ref, xn_ref, between=None):
    rows = x_ref.shape[0] // PROJ_PIECES

    def run(with_norm):
        w = w_ref[...].astype(BF16)
        for c in range(PROJ_PIECES):
            piece = slice(c * rows, (c + 1) * rows)
            if with_norm:
                x = x_ref[piece, :]
                ms = jnp.mean(x * x, axis=-1, keepdims=True)
                xn_ref[piece, :] = (x * lax.rsqrt(ms + EPS) * g_ref[...]).astype(BF16)
            o_ref[piece, :] = _dot(xn_ref[piece, :], w).astype(o_ref.dtype)
            if between is not None:
                between(c)

    @pl.when(pl.program_id(1) == 0)
    def _():
        run(True)

    @pl.when(pl.program_id(1) != 0)
    def _():
        run(False)


def _norm_matmul_kernel(x_ref, g_ref, w_ref, o_ref, xn_ref):
    _norm_matmul_pieces(x_ref, g_ref, w_ref, o_ref, xn_ref)


def norm_matmul(x, gain, w, out_dtype, tm, tn, name):
    t, d = x.shape
    n = w.shape[1]
    return pl.pallas_call(
        _norm_matmul_kernel,
        out_shape=jax.ShapeDtypeStruct((t, n), out_dtype),
        grid=(t // tm, n // tn),
        in_specs=[
            pl.BlockSpec((tm, d), lambda i, j: (i, 0)),
            pl.BlockSpec((1, d), lambda i, j: (0, 0)),
            pl.BlockSpec((d, tn), lambda i, j: (0, j)),
        ],
        out_specs=pl.BlockSpec((tm, tn), lambda i, j: (i, j)),
        scratch_shapes=[pltpu.VMEM((tm, d), BF16)],
        compiler_params=_params("arbitrary", "arbitrary"),
        name=name,
    )(x, gain.reshape(1, d), w)


def _split2(x):
    hi = x.astype(BF16)
    lo = (x - hi.astype(F32)).astype(BF16)
    return hi, lo


def _hgrn2_kernel(q_ref, f_ref, i_ref, lb_ref, gain_ref, o_ref, st_ref):
    seq = q_ref.shape[0]
    tile, chunk = HGRN_TILE, HGRN_CHUNK
    per_tile = tile // chunk
    part_rows = seq // HGRN_PARTS
    nt = part_rows // tile
    lb = lb_ref[...]
    gain = gain_ref[...]

    def tiles(x):
        return x.reshape(nt, tile, HEAD_DIM)

    def bmm(spec, a, b):
        return jnp.einsum(spec, a, b, preferred_element_type=F32)

    ti = lax.broadcasted_iota(jnp.int32, (tile, tile), 0)
    tj = lax.broadcasted_iota(jnp.int32, (tile, tile), 1)
    intra_mask = (((ti // chunk) == (tj // chunk)) & (tj <= ti))[None]
    sum_mat = jnp.broadcast_to(jnp.where(intra_mask, 1.0, 0.0).astype(BF16), (nt, tile, tile))
    row_chunk = (lax.broadcasted_iota(jnp.int32, (tile, HEAD_DIM), 0) // chunk)[None]

    def per_chunk_lanes(x):
        zero = jnp.zeros_like(x)
        return jnp.concatenate([jnp.where(row_chunk == ci, x, zero) for ci in range(per_tile)], axis=-1)

    carried = [jnp.zeros((HEAD_DIM, HEAD_DIM), F32)]

    def part(p):
        rows = slice(p * part_rows, (p + 1) * part_rows)
        f = lb + (1.0 - lb) * _sigmoid(f_ref[rows, :].astype(F32))
        k = tiles(1.0 - f)
        hi, lo = _split2(jnp.log2(f))
        yield
        b = bmm('trc,tcd->trd', sum_mat, tiles(hi)) + bmm('trc,tcd->trd', sum_mat, tiles(lo))
        b_chunks = b.reshape(part_rows // chunk, chunk, HEAD_DIM)
        b_tot = jnp.broadcast_to(b_chunks[:, chunk - 1:, :], b_chunks.shape).reshape(b.shape)
        yield
        q_dec = (tiles(q_ref[rows, :].astype(F32)) * jnp.exp2(b)).astype(BF16)
        k_inv = (k * jnp.exp2(-b)).astype(BF16)
        k_end = (k * jnp.exp2(b_tot - b)).astype(BF16)
        v = tiles(i_ref[rows, :].astype(F32))
        yield
        att = bmm('tid,tjd->tij', q_dec, k_inv)
        att = jnp.where(intra_mask, att, 0.0).astype(BF16)
        v_t = jnp.swapaxes(v, 1, 2).astype(BF16)
        incr = bmm('tvn,tnk->tvk', v_t, per_chunk_lanes(k_end))
        yield
        o_intra = bmm('tij,tjd->tid', att, v.astype(BF16))
        state_t = carried[0]
        for t in range(nt):
            for ci in range(per_tile):
                lanes = slice(ci * HEAD_DIM, (ci + 1) * HEAD_DIM)
                st_ref[p * nt + t, :, lanes] = state_t.astype(BF16)
                decay = jnp.exp2(b_tot[t, ci * chunk:ci * chunk + 1, :])
                state_t = state_t * decay + incr[t, :, lanes]
        carried[0] = state_t
        yield
        o = o_intra + bmm('tnk,tvk->tnv', per_chunk_lanes(q_dec), st_ref[p * nt:(p + 1) * nt])
        o = o.reshape(part_rows, HEAD_DIM)
        y = o * lax.rsqrt(jnp.mean(o * o, axis=-1, keepdims=True) + EPS) * gain
        o_ref[rows, :] = y.astype(o_ref.dtype)
        yield

    _run_staggered([part(p) for p in range(HGRN_PARTS)], n_stages=6)


def hgrn2_mixer(h3, lb, gain, col_q, col_f, col_i):
    bsz, seq, _ = h3.shape

    def col(off):
        return pl.BlockSpec((None, seq, HEAD_DIM), lambda b, h: (b, 0, off + h))

    vec = pl.BlockSpec((None, 1, HEAD_DIM), lambda b, h: (h, 0, 0))
    return pl.pallas_call(
        _hgrn2_kernel,
        out_shape=jax.ShapeDtypeStruct((bsz, seq, WIDTH), BF16),
        grid=(bsz, N_HEADS),
        in_specs=[col(col_q), col(col_f), col(col_i), vec, vec],
        out_specs=pl.BlockSpec((None, seq, HEAD_DIM), lambda b, h: (b, 0, h)),
        scratch_shapes=[pltpu.VMEM((seq // HGRN_TILE, HEAD_DIM, HGRN_TILE // HGRN_CHUNK * HEAD_DIM), BF16)],
        compiler_params=_params("arbitrary", "arbitrary"),
        name="hgrn2",
    )(h3, h3, h3, lb.reshape(N_HEADS, 1, HEAD_DIM), gain.reshape(N_HEADS, 1, HEAD_DIM))


RET_HALF = HEAD_DIM // 2
MOBA_HALF = ROPE_DIM // 2


def _rope_table_kernel(pos_ref, inv_ref, ret_cos_ref, ret_sin_ref, moba_cos_ref, moba_lo_ref, moba_hi_ref):
    ang = pos_ref[...] * inv_ref[...]
    cos = jnp.cos(ang)
    sin = jnp.sin(ang)
    lane = lax.broadcasted_iota(jnp.int32, ang.shape, 1)
    low = lane < RET_HALF
    cos_up = pltpu.roll(cos, RET_HALF, 1)
    sin_up = pltpu.roll(sin, RET_HALF, 1)
    ret_cos_ref[...] = jnp.where(low, cos, cos_up)
    ret_sin_ref[...] = jnp.where(low, -sin, sin_up)
    first = lane < MOBA_HALF
    second = (lane >= MOBA_HALF) & (lane < 2 * MOBA_HALF)
    moba_cos_ref[...] = jnp.where(first, cos_up, jnp.where(second, pltpu.roll(cos_up, MOBA_HALF, 1), 1.0))
    moba_lo_ref[...] = jnp.where(first, -sin_up, 0.0)
    moba_hi_ref[...] = jnp.where(second, pltpu.roll(sin_up, MOBA_HALF, 1), 0.0)


def _proj_tables_kernel(x_ref, g_ref, w_ref, pos_ref, inv_ref, o_ref, *table_and_scratch):
    table_refs, xn_ref = table_and_scratch[:-1], table_and_scratch[-1]
    trows = pos_ref.shape[0] // PROJ_PIECES

    def table_piece(c):
        piece = slice(c * trows, (c + 1) * trows)
        _rope_table_kernel(pos_ref.at[piece, :], inv_ref, *(r.at[piece, :] for r in table_refs))

    _norm_matmul_pieces(x_ref, g_ref, w_ref, o_ref, xn_ref, between=table_piece)


def project_with_rope_tables(x, gain, w, positions, tm, tn):
    t, d = x.shape
    n = w.shape[1]
    n_steps = n // tn
    rows = tm // n_steps
    inv_ret = RET_THETA ** (-jnp.arange(RET_HALF, dtype=F32) / RET_HALF)
    inv_moba = ROPE_THETA ** (-jnp.arange(MOBA_HALF, dtype=F32) / MOBA_HALF)
    inv = jnp.concatenate([inv_ret, inv_moba, jnp.zeros((HEAD_DIM - RET_HALF - MOBA_HALF,), F32)])
    posf = jnp.broadcast_to(positions.astype(F32).reshape(t, 1), (t, HEAD_DIM))
    table_spec = pl.BlockSpec((rows, HEAD_DIM), lambda i, j: (i * n_steps + j, 0))
    return pl.pallas_call(
        _proj_tables_kernel,
        out_shape=[jax.ShapeDtypeStruct((t, n), BF16)] + [jax.ShapeDtypeStruct((t, HEAD_DIM), F32)] * 5,
        grid=(t // tm, n_steps),
        in_specs=[
            pl.BlockSpec((tm, d), lambda i, j: (i, 0)),
            pl.BlockSpec((1, d), lambda i, j: (0, 0)),
            pl.BlockSpec((d, tn), lambda i, j: (0, j)),
            table_spec,
            pl.BlockSpec((1, HEAD_DIM), lambda i, j: (0, 0)),
        ],
        out_specs=[pl.BlockSpec((tm, tn), lambda i, j: (i, j))] + [table_spec] * 5,
        scratch_shapes=[pltpu.VMEM((tm, d), BF16)],
        compiler_params=_params("arbitrary", "arbitrary"),
        name="in_proj_even",
    )(x, gain.reshape(1, d), w, posf, inv.reshape(1, HEAD_DIM))


def _moba_kernel(q_ref, k_ref, v_ref, cos_ref, sin_lo_ref, sin_hi_ref, o_ref,
                 qr_ref, ka_ref, va_ref, kmean_ref):
    seq = q_ref.shape[0]
    blk = MOBA_BLOCK
    nb = seq // blk
    exp2_scale = HEAD_DIM ** -0.5 * LOG2E

    row = lax.broadcasted_iota(jnp.int32, (blk, blk), 0)
    colm = lax.broadcasted_iota(jnp.int32, (blk, blk), 1)
    causal = colm <= row
    eye = jnp.where(colm == row, 1.0, 0.0).astype(BF16)
    blk_row = lax.broadcasted_iota(jnp.int32, (nb, blk), 0)
    lane = lax.broadcasted_iota(jnp.int32, (blk, HEAD_DIM), 1)
    kmean_ref[...] = jnp.zeros(kmean_ref.shape, kmean_ref.dtype)

    def prepare(qb):
        rows = slice(qb * blk, (qb + 1) * blk)

        def rope(x):
            return (x * cos_ref[rows, :]
                    + pltpu.roll(x, HEAD_DIM - MOBA_HALF, 1) * sin_lo_ref[rows, :]
                    + pltpu.roll(x, MOBA_HALF, 1) * sin_hi_ref[rows, :])

        kr = rope(k_ref[rows, :].astype(F32))
        kmean_ref[qb:qb + 1, :] = jnp.mean(kr, axis=0, keepdims=True)
        ka_ref[rows, :HEAD_DIM] = kr.astype(BF16)
        ka_ref[rows, HEAD_DIM:] = jnp.where(lane == qb, MASKED, 0.0).astype(BF16)
        qr_ref[rows, :] = (rope(q_ref[rows, :].astype(F32)) * exp2_scale).astype(BF16)
        va_ref[rows, :HEAD_DIM] = v_ref[rows, :]
        va_ref[rows, HEAD_DIM:] = jnp.ones((blk, HEAD_DIM), BF16)

    def scores(qb):
        rows = slice(qb * blk, (qb + 1) * blk)
        q = qr_ref[rows, :]
        n_keys = (qb + 1) * blk
        if qb > MOBA_TOPK:
            gate = _dot_nt(kmean_ref[...].astype(BF16), q)
            rank = jnp.zeros((nb, blk), F32)
            for j in range(qb):
                gj = gate[j:j + 1, :]
                ahead = (gj > gate) | ((gj == gate) & (blk_row > j))
                rank = rank + jnp.where(ahead, 1.0, 0.0)
            drop = jnp.where((rank >= float(MOBA_TOPK)) & (blk_row < qb), 1.0, 0.0)
            drop = jnp.concatenate([drop, jnp.zeros((HEAD_DIM - nb, blk), F32)], axis=0).astype(BF16)
            drop_col = _dot_nt(eye, drop).astype(BF16)
            s = _dot_nt(jnp.concatenate([q, drop_col], axis=1), ka_ref[:n_keys, :])
        else:
            s = _dot_nt(q, ka_ref[:n_keys, :HEAD_DIM])
        own = jnp.where(causal, s[:, qb * blk:], MASKED)
        return jnp.concatenate([s[:, :qb * blk], own], axis=1) if qb else own

    prepare(0)
    s_next = scores(0)
    for qb in range(nb):
        rows = slice(qb * blk, (qb + 1) * blk)
        s = s_next
        if qb + 1 < nb:
            prepare(qb + 1)
            s_next = scores(qb + 1)
        m = jnp.max(s, axis=-1, keepdims=True)
        p = jnp.exp2(s - m).astype(BF16)
        o = _dot(p, va_ref[:(qb + 1) * blk, :])
        o_ref[rows, :] = (o[:, :HEAD_DIM] / o[:, HEAD_DIM:]).astype(o_ref.dtype)


def moba_mixer(h3, cos, sin_lo, sin_hi, col_q, col_k, col_v):
    bsz, seq, _ = h3.shape

    def col(off):
        return pl.BlockSpec((None, seq, HEAD_DIM), lambda b, h: (b, 0, off + h))

    table = pl.BlockSpec((None, seq, HEAD_DIM), lambda b, h: (b, 0, 0))
    return pl.pallas_call(
        _moba_kernel,
        out_shape=jax.ShapeDtypeStruct((bsz, seq, WIDTH), BF16),
        grid=(bsz, N_HEADS),
        in_specs=[col(col_q), col(col_k), col(col_v), table, table, table],
        out_specs=pl.BlockSpec((None, seq, HEAD_DIM), lambda b, h: (b, 0, h)),
        scratch_shapes=[pltpu.VMEM((seq, HEAD_DIM), BF16), pltpu.VMEM((seq, 2 * HEAD_DIM), BF16),
                        pltpu.VMEM((seq, 2 * HEAD_DIM), BF16), pltpu.VMEM((seq // MOBA_BLOCK, HEAD_DIM), F32)],
        compiler_params=_params("arbitrary", "arbitrary"),
        name="moba",
    )(h3, h3, h3, cos, sin_lo, sin_hi)


def _rglru_kernel(x_ref, cw_ref, cb_ref, wa_ref, ba_ref, wx_ref, bx_ref, lam_ref,
                  o_ref, a_ref, u_ref, xpad_ref):
    seq, width = x_ref.shape
    sub = 8
    groups = seq // sub

    xpad_ref[:sub, :] = jnp.zeros((sub, width), F32)
    xpad_ref[sub:, :] = x_ref[...].astype(F32)
    xf = cb_ref[...]
    for d in range(CONV_WIDTH):
        xf = xf + xpad_ref[sub - d:sub - d + seq, :] * cw_ref[CONV_WIDTH - 1 - d:CONV_WIDTH - d, :]

    xb = xf.astype(BF16)

    def gate(w_ref, b_ref):
        pre = [_dot(xb[:, j * HEAD_DIM:(j + 1) * HEAD_DIM], w_ref[j]) for j in range(width // HEAD_DIM)]
        return _sigmoid(jnp.concatenate(pre, axis=1) + b_ref[...])

    r = gate(wa_ref, ba_ref)
    ig = gate(wx_ref, bx_ref)
    nl = -lam_ref[...]
    softplus = jnp.maximum(nl, 0.0) + jnp.log1p(jnp.exp(-jnp.abs(nl)))
    a = jnp.exp2(r * ((-RG_C * LOG2E) * softplus))
    u = jnp.sqrt(1.0 - a * a) * ig * xf

    a3 = a.reshape(groups, sub, width)
    u3 = u.reshape(groups, sub, width)
    s_idx = lax.broadcasted_iota(jnp.int32, a3.shape, 1)
    for d in (1, 2, 4):
        keep = s_idx >= d
        a_prev = jnp.where(keep, pltpu.roll(a3, d, 1), 1.0)
        u_prev = jnp.where(keep, pltpu.roll(u3, d, 1), 0.0)
        u3 = u3 + a3 * u_prev
        a3 = a3 * a_prev
    a_ref[...] = a3
    u_ref[...] = u3

    def group_body(g, carry):
        hg = a_ref[g] * carry + u_ref[g]
        u_ref[g] = hg
        return jnp.broadcast_to(hg[sub - 1:sub, :], (sub, width))

    lax.fori_loop(0, groups, group_body, jnp.zeros((sub, width), F32), unroll=8)
    o_ref[...] = u_ref[...].reshape(seq, width).astype(o_ref.dtype)


def rglru_mixer(h3, conv_w, conv_b, wa, ba, wx, bx, lam, col_x, blocks_per_step=4):
    bsz, seq, _ = h3.shape
    width = blocks_per_step * HEAD_DIM

    def col(off):
        return pl.BlockSpec((None, seq, width), lambda b, c: (b, 0, off // blocks_per_step + c))

    def vec(rows):
        return pl.BlockSpec((rows, width), lambda b, c: (0, c))

    wspec = pl.BlockSpec((blocks_per_step, HEAD_DIM, HEAD_DIM), lambda b, c: (c, 0, 0))
    return pl.pallas_call(
        _rglru_kernel,
        out_shape=jax.ShapeDtypeStruct((bsz, seq, WIDTH), BF16),
        grid=(bsz, WIDTH // width),
        in_specs=[col(col_x), vec(CONV_WIDTH), vec(1), wspec, vec(1), wspec, vec(1), vec(1)],
        out_specs=pl.BlockSpec((None, seq, width), lambda b, c: (b, 0, c)),
        scratch_shapes=[pltpu.VMEM((seq // 8, 8, width), F32)] * 2 + [pltpu.VMEM((seq + 8, width), F32)],
        compiler_params=_params("arbitrary", "arbitrary"),
        name="rglru",
    )(h3, conv_w, conv_b.reshape(1, WIDTH), wa.astype(BF16), ba.reshape(1, WIDTH),
      wx.astype(BF16), bx.reshape(1, WIDTH), lam.reshape(1, WIDTH))


def _retention_kernel(q_ref, k_ref, v_ref, cos_ref, sin_ref, logg_ref, gain_ref, bias_ref,
                      o_ref, st_ref):
    seq = q_ref.shape[0]
    ck = RET_CHUNK
    half = RET_HALF
    nc = seq // ck // RET_PARTS

    def chunks(x):
        return x.reshape(nc, ck, HEAD_DIM)

    def bmm(spec, a, b):
        return jnp.einsum(spec, a, b, preferred_element_type=F32)

    log_g = logg_ref[...]
    ri = lax.broadcasted_iota(jnp.int32, (ck, ck), 0)
    ci = lax.broadcasted_iota(jnp.int32, (ck, ck), 1)
    diff = (ri - ci).astype(F32)
    dmask = jnp.where(ri >= ci, jnp.exp(jnp.maximum(diff, 0.0) * log_g[:, :1]), 0.0)
    idx = lax.broadcasted_iota(jnp.int32, (ck, HEAD_DIM), 0).astype(F32)
    q_fac = jnp.exp((idx + 1.0) * log_g)
    k_fac = jnp.exp((ck - 1.0 - idx) * log_g)
    chunk_decay = jnp.exp(float(ck) * log_g)

    carried = [jnp.zeros((HEAD_DIM, HEAD_DIM), F32)]

    def part(p):
        rows = slice(p * nc * ck, (p + 1) * nc * ck)
        cos = cos_ref[rows, :]
        sin = sin_ref[rows, :]
        q = q_ref[rows, :].astype(F32)
        k = k_ref[rows, :].astype(F32)
        q = chunks(q * cos + pltpu.roll(q, half, 1) * sin)
        k = chunks((k * cos + pltpu.roll(k, half, 1) * sin) * (HEAD_DIM ** -0.5))
        v = chunks(v_ref[rows, :])
        yield
        att = bmm('cid,cjd->cij', q.astype(BF16), k.astype(BF16)) * dmask[None]
        k_end_t = jnp.swapaxes(k * k_fac[None], 1, 2).astype(BF16)
        incr = bmm('ckn,cnv->ckv', k_end_t, v)
        yield
        o_intra = bmm('cij,cjd->cid', att.astype(BF16), v)
        state = carried[0]
        for c in range(nc):
            st_ref[p * nc + c] = state.astype(BF16)
            state = state * chunk_decay + incr[c]
        carried[0] = state
        yield
        o = o_intra + bmm('cnk,ckv->cnv', (q * q_fac[None]).astype(BF16), st_ref[p * nc:(p + 1) * nc])
        o = o.reshape(nc * ck, HEAD_DIM)
        mu = jnp.mean(o, axis=-1, keepdims=True)
        oc = o - mu
        var = jnp.mean(oc * oc, axis=-1, keepdims=True)
        o_ref[rows, :] = (oc * lax.rsqrt(var + EPS) * gain_ref[...] + bias_ref[...]).astype(o_ref.dtype)
        yield

    _run_staggered([part(p) for p in range(RET_PARTS)], n_stages=4)


def retention_mixer(h3, cos, sin, log_g, gain, bias, col_q, col_k, col_v):
    bsz, seq, _ = h3.shape

    def col(off):
        return pl.BlockSpec((None, seq, HEAD_DIM), lambda b, h: (b, 0, off + h))

    table = pl.BlockSpec((None, seq, HEAD_DIM), lambda b, h: (b, 0, 0))
    vec = pl.BlockSpec((None, 1, HEAD_DIM), lambda b, h: (h, 0, 0))
    return pl.pallas_call(
        _retention_kernel,
        out_shape=jax.ShapeDtypeStruct((bsz, seq, WIDTH), BF16),
        grid=(bsz, N_HEADS),
        in_specs=[col(col_q), col(col_k), col(col_v), table, table, vec, vec, vec],
        out_specs=pl.BlockSpec((None, seq, HEAD_DIM), lambda b, h: (b, 0, h)),
        scratch_shapes=[pltpu.VMEM((seq // RET_CHUNK, HEAD_DIM, HEAD_DIM), BF16)],
        compiler_params=_params("arbitrary", "arbitrary"),
        name="retention",
    )(h3, h3, h3, cos, sin, log_g,
      gain.reshape(N_HEADS, 1, HEAD_DIM), bias.reshape(N_HEADS, 1, HEAD_DIM))


def _post_kernel(x_ref, ya_ref, yb_ref, z_ref, woa_ref, wob_ref, g_ref, wq_ref, kv_ref, wo_ref, fg_ref,
                 o_ref, *, final):
    part_rows = x_ref.shape[0] // POST_PARTS
    scale = HEAD_DIM ** -0.5

    def gated(y_ref, rows, cols):
        return (y_ref[rows, :].astype(F32) * _silu(z_ref[rows, cols].astype(F32))).astype(BF16)

    def part(p):
        rows = slice(p * part_rows, (p + 1) * part_rows)
        ga = gated(ya_ref, rows, slice(0, WIDTH))
        yield
        gb = gated(yb_ref, rows, slice(WIDTH, 2 * WIDTH))
        x1 = x_ref[rows, :] + _dot(ga, woa_ref[...])
        yield
        x1 = x1 + _dot(gb, wob_ref[...])
        yield
        ms = jnp.mean(x1 * x1, axis=-1, keepdims=True)
        xn = (x1 * lax.rsqrt(ms + EPS) * g_ref[...]).astype(BF16)
        q = _dot(xn, wq_ref[...]).astype(BF16)
        yield
        head_cols = [slice(hd * HEAD_DIM, (hd + 1) * HEAD_DIM) for hd in range(XA_HEADS)]
        scores = [_dot_nt(q[:, cs], kv_ref[:, cs]) * scale for cs in head_cols]
        yield
        probs = []
        for s in scores:
            pr = jnp.exp(s - jnp.max(s, axis=-1, keepdims=True))
            probs.append((pr / jnp.sum(pr, axis=-1, keepdims=True)).astype(BF16))
        yield
        o = jnp.concatenate([_dot(pr, kv_ref[:, XA_WIDTH + cs.start:XA_WIDTH + cs.stop]).astype(BF16)
                             for pr, cs in zip(probs, head_cols)], axis=1)
        yield
        x2 = x1 + _dot(o, wo_ref[...])
        if final:
            ms2 = jnp.mean(x2 * x2, axis=-1, keepdims=True)
            x2 = x2 * lax.rsqrt(ms2 + EPS) * fg_ref[...]
        o_ref[rows, :] = x2
        yield

    _run_staggered([part(p) for p in range(POST_PARTS)], n_stages=8)


def post_mixer(x3, ya, yb, h3, z_block, w_out, xa_gain, wq, kv, wo, final_gain, final, tm=1024):
    bsz, seq, d = x3.shape
    full = lambda shape: pl.BlockSpec(shape, lambda b, i: (0,) * len(shape))
    tile = lambda w: pl.BlockSpec((None, tm, w), lambda b, i: (b, i, 0))
    return pl.pallas_call(
        functools.partial(_post_kernel, final=final),
        out_shape=jax.ShapeDtypeStruct((bsz, seq, d), F32),
        grid=(bsz, seq // tm),
        in_specs=[tile(d), tile(WIDTH), tile(WIDTH),
                  pl.BlockSpec((None, tm, 2 * WIDTH), lambda b, i: (b, i, z_block)),
                  full((WIDTH, d)), full((WIDTH, d)), full((1, d)), full((d, XA_WIDTH)),
                  pl.BlockSpec((None, MEM_LEN, 2 * XA_WIDTH), lambda b, i: (b, 0, 0)),
                  full((XA_WIDTH, d)), full((1, d))],
        out_specs=tile(d),
        compiler_params=_params("arbitrary", "arbitrary"),
        name="post_final" if final else "post",
    )(x3, ya, yb, h3, w_out[:WIDTH].astype(BF16), w_out[WIDTH:].astype(BF16), xa_gain.reshape(1, d),
      wq.astype(BF16), kv, wo.astype(BF16), final_gain.reshape(1, d))


def kernel(x, mem, positions, hgrn_lb_logits, ev_norm, ev_w_in, ev_hgrn_gain, ev_w_out, od_norm, od_w_in, od_conv_w, od_conv_b, od_rg_wa, od_rg_ba, od_rg_wx, od_rg_bx, od_rg_lambda, od_ret_gain, od_ret_bias, od_w_out, xa_norm, xa_mem_norm, xa_wq, xa_wkv, xa_wo, final_norm):
    bsz, seq, d = x.shape
    tokens = bsz * seq
    lb_all = jnp.cumsum(jax.nn.softmax(hgrn_lb_logits.astype(F32), axis=0), axis=0)
    log_g = jnp.log(1.0 - 2.0 ** (-5.0 - jnp.arange(N_HEADS, dtype=F32)))
    log_g = jnp.broadcast_to(log_g[:, None, None], (N_HEADS, 1, HEAD_DIM))
    mem2 = mem.reshape(bsz * MEM_LEN, d)

    def memory_kv(layer):
        kv = norm_matmul(mem2, xa_mem_norm[layer], xa_wkv[layer], BF16,
                         tm=bsz * MEM_LEN // 2, tn=2 * XA_WIDTH, name="mem_kv")
        return kv.reshape(bsz, MEM_LEN, 2 * XA_WIDTH)

    h, *tables = project_with_rope_tables(x.reshape(tokens, d), ev_norm[0], ev_w_in.reshape(ev_w_in.shape[1:]),
                                          positions, tm=seq, tn=1024)
    ret_cos, ret_sin, moba_cos, moba_lo, moba_hi = (tb.reshape(bsz, seq, HEAD_DIM) for tb in tables)
    h3 = h.reshape(bsz, seq, -1)
    nb = WIDTH // HEAD_DIM
    ya = hgrn2_mixer(h3, lb_all[0], ev_hgrn_gain[0], 0, nb, 2 * nb)
    yb = moba_mixer(h3, moba_cos, moba_lo, moba_hi, 3 * nb, 4 * nb, 5 * nb)
    x1 = post_mixer(x, ya, yb, h3, 3, ev_w_out[0], xa_norm[0], xa_wq[0], memory_kv(0), xa_wo[0],
                    final_norm, final=False)

    h = norm_matmul(x1.reshape(tokens, d), od_norm[0], od_w_in.reshape(od_w_in.shape[1:]), BF16,
                    tm=2048, tn=1024, name="in_proj_odd")
    h3 = h.reshape(bsz, seq, -1)
    yc = rglru_mixer(h3, od_conv_w[0], od_conv_b[0], od_rg_wa[0], od_rg_ba[0], od_rg_wx[0],
                     od_rg_bx[0], od_rg_lambda[0], 0)
    yd = retention_mixer(h3, ret_cos, ret_sin, log_g, od_ret_gain[0], od_ret_bias[0],
                         nb, 2 * nb, 3 * nb)
    return post_mixer(x1, yc, yd, h3, 2, od_w_out[0], xa_norm[1], xa_wq[1], memory_kv(1), xa_wo[1],
                      final_norm, final=True)
```

```python
import functools
import math

import jax
import jax.numpy as jnp
from jax import lax
from jax.experimental import pallas as pl
from jax.experimental.pallas import tpu as pltpu

F32 = jnp.float32
BF16 = jnp.bfloat16

D_MODEL = 1024
HEAD_DIM = 128
N_HEADS = 8
WIDTH = N_HEADS * HEAD_DIM
MEM_LEN = 256
HGRN_CHUNK = 32
HGRN_TILE = 128
HGRN_PARTS = 16
MOBA_BLOCK = 256
MOBA_TOPK = 3
RET_CHUNK = 256
RET_PARTS = 2
CONV_WIDTH = 4
RG_C = 8.0
RG_PARTS = 8
ROPE_THETA = 500000.0
ROPE_DIM = HEAD_DIM // 4
RET_THETA = 10000.0
PROJ_PIECES = 4
POST_PARTS = 4
XA_HEADS = 4
XA_WIDTH = XA_HEADS * HEAD_DIM
EPS = 1e-6
VMEM_LIMIT = 56 * 1024 * 1024
MASKED = -1e30

_NT = (((1,), (1,)), ((), ()))


def _dot(a, b):
    return jnp.dot(a, b, preferred_element_type=F32)


def _dot_nt(a, b):
    return lax.dot_general(a, b, _NT, preferred_element_type=F32)


LOG2E = math.log2(math.e)


def _sigmoid(x):
    return 1.0 / (1.0 + jnp.exp2(x * (-LOG2E)))


def _silu(x):
    return x * _sigmoid(x)


def _run_staggered(parts, n_stages):
    for step in range(n_stages + len(parts) - 1):
        for p, gen in enumerate(parts):
            if 0 <= step - p < n_stages:
                next(gen)


def _params(*sem):
    return pltpu.CompilerParams(dimension_semantics=sem, vmem_limit_bytes=VMEM_LIMIT)


def _norm_matmul_pieces(x_ref, g_ref, w_ref, o_ref, xn_ref, between=None):
    rows = x_ref.shape[0] // PROJ_PIECES

    def run(with_norm):
        w = w_ref[...].astype(BF16)
        for c in range(PROJ_PIECES):
            piece = slice(c * rows, (c + 1) * rows)
            if with_norm:
                x = x_ref[piece, :]
                ms = jnp.mean(x * x, axis=-1, keepdims=True)
                xn_ref[piece, :] = (x * lax.rsqrt(ms + EPS) * g_ref[...]).astype(BF16)
            o_ref[piece, :] = _dot(xn_ref[piece, :], w).astype(o_ref.dtype)
            if between is not None:
                between(c)

    @pl.when(pl.program_id(1) == 0)
    def _():
        run(True)

    @pl.when(pl.program_id(1) != 0)
    def _():
        run(False)


def _norm_matmul_kernel(x_ref, g_ref, w_ref, o_ref, xn_ref):
    _norm_matmul_pieces(x_ref, g_ref, w_ref, o_ref, xn_ref)


def norm_matmul(x, gain, w, out_dtype, tm, tn, name):
    t, d = x.shape
    n = w.shape[1]
    return pl.pallas_call(
        _norm_matmul_kernel,
        out_shape=jax.ShapeDtypeStruct((t, n), out_dtype),
        grid=(t // tm, n // tn),
        in_specs=[
            pl.BlockSpec((tm, d), lambda i, j: (i, 0)),
            pl.BlockSpec((1, d), lambda i, j: (0, 0)),
            pl.BlockSpec((d, tn), lambda i, j: (0, j)),
        ],
        out_specs=pl.BlockSpec((tm, tn), lambda i, j: (i, j)),
        scratch_shapes=[pltpu.VMEM((tm, d), BF16)],
        compiler_params=_params("arbitrary", "arbitrary"),
        name=name,
    )(x, gain.reshape(1, d), w)


def _split2(x):
    hi = x.astype(BF16)
    lo = (x - hi.astype(F32)).astype(BF16)
    return hi, lo


def _hgrn2_kernel(q_ref, f_ref, i_ref, lb_ref, gain_ref, o_ref, st_ref):
    seq = q_ref.shape[0]
    tile, chunk = HGRN_TILE, HGRN_CHUNK
    per_tile = tile // chunk
    part_rows = seq // HGRN_PARTS
    nt = part_rows // tile
    lb = lb_ref[...]
    gain = gain_ref[...]

    def tiles(x):
        return x.reshape(nt, tile, HEAD_DIM)

    def bmm(spec, a, b):
        return jnp.einsum(spec, a, b, preferred_element_type=F32)

    ti = lax.broadcasted_iota(jnp.int32, (tile, tile), 0)
    tj = lax.broadcasted_iota(jnp.int32, (tile, tile), 1)
    intra_mask = (((ti // chunk) == (tj // chunk)) & (tj <= ti))[None]
    sum_mat = jnp.broadcast_to(jnp.where(intra_mask, 1.0, 0.0).astype(BF16), (nt, tile, tile))
    row_chunk = (lax.broadcasted_iota(jnp.int32, (tile, HEAD_DIM), 0) // chunk)[None]

    def per_chunk_lanes(x):
        zero = jnp.zeros_like(x)
        return jnp.concatenate([jnp.where(row_chunk == ci, x, zero) for ci in range(per_tile)], axis=-1)

    carried = [jnp.zeros((HEAD_DIM, HEAD_DIM), F32)]

    def part(p):
        rows = slice(p * part_rows, (p + 1) * part_rows)
        f = lb + (1.0 - lb) * _sigmoid(f_ref[rows, :].astype(F32))
        k = tiles(1.0 - f)
        hi, lo = _split2(jnp.log2(f))
        yield
        b = bmm('trc,tcd->trd', sum_mat, tiles(hi)) + bmm('trc,tcd->trd', sum_mat, tiles(lo))
        b_chunks = b.reshape(part_rows // chunk, chunk, HEAD_DIM)
        b_tot = jnp.broadcast_to(b_chunks[:, chunk - 1:, :], b_chunks.shape).reshape(b.shape)
        yield
        q_dec = (tiles(q_ref[rows, :].astype(F32)) * jnp.exp2(b)).astype(BF16)
        k_inv = (k * jnp.exp2(-b)).astype(BF16)
        k_end = (k * jnp.exp2(b_tot - b)).astype(BF16)
        v = tiles(i_ref[rows, :].astype(F32))
        yield
        att = bmm('tid,tjd->tij', q_dec, k_inv)
        att = jnp.where(intra_mask, att, 0.0).astype(BF16)
        v_t = jnp.swapaxes(v, 1, 2).astype(BF16)
        incr = bmm('tvn,tnk->tvk', v_t, per_chunk_lanes(k_end))
        yield
        o_intra = bmm('tij,tjd->tid', att, v.astype(BF16))
        state_t = carried[0]
        for t in range(nt):
            for ci in range(per_tile):
                lanes = slice(ci * HEAD_DIM, (ci + 1) * HEAD_DIM)
                st_ref[p * nt + t, :, lanes] = state_t.astype(BF16)
                decay = jnp.exp2(b_tot[t, ci * chunk:ci * chunk + 1, :])
                state_t = state_t * decay + incr[t, :, lanes]
        carried[0] = state_t
        yield
        o = o_intra + bmm('tnk,tvk->tnv', per_chunk_lanes(q_dec), st_ref[p * nt:(p + 1) * nt])
        o = o.reshape(part_rows, HEAD_DIM)
        y = o * lax.rsqrt(jnp.mean(o * o, axis=-1, keepdims=True) + EPS) * gain
        o_ref[rows, :] = y.astype(o_ref.dtype)
        yield

    _run_staggered([part(p) for p in range(HGRN_PARTS)], n_stages=6)


def hgrn2_mixer(h3, lb, gain, col_q, col_f, col_i):
    bsz, seq, _ = h3.shape

    def col(off):
        return pl.BlockSpec((None, seq, HEAD_DIM), lambda b, h: (b, 0, off + h))

    vec = pl.BlockSpec((None, 1, HEAD_DIM), lambda b, h: (h, 0, 0))
    return pl.pallas_call(
        _hgrn2_kernel,
        out_shape=jax.ShapeDtypeStruct((bsz, seq, WIDTH), BF16),
        grid=(bsz, N_HEADS),
        in_specs=[col(col_q), col(col_f), col(col_i), vec, vec],
        out_specs=pl.BlockSpec((None, seq, HEAD_DIM), lambda b, h: (b, 0, h)),
        scratch_shapes=[pltpu.VMEM((seq // HGRN_TILE, HEAD_DIM, HGRN_TILE // HGRN_CHUNK * HEAD_DIM), BF16)],
        compiler_params=_params("arbitrary", "arbitrary"),
        name="hgrn2",
    )(h3, h3, h3, lb.reshape(N_HEADS, 1, HEAD_DIM), gain.reshape(N_HEADS, 1, HEAD_DIM))


RET_HALF = HEAD_DIM // 2
MOBA_HALF = ROPE_DIM // 2


def _rope_table_kernel(pos_ref, inv_ref, ret_cos_ref, ret_sin_ref, moba_cos_ref, moba_lo_ref, moba_hi_ref):
    ang = pos_ref[...] * inv_ref[...]
    cos = jnp.cos(ang)
    sin = jnp.sin(ang)
    lane = lax.broadcasted_iota(jnp.int32, ang.shape, 1)
    low = lane < RET_HALF
    cos_up = pltpu.roll(cos, RET_HALF, 1)
    sin_up = pltpu.roll(sin, RET_HALF, 1)
    ret_cos_ref[...] = jnp.where(low, cos, cos_up)
    ret_sin_ref[...] = jnp.where(low, -sin, sin_up)
    first = lane < MOBA_HALF
    second = (lane >= MOBA_HALF) & (lane < 2 * MOBA_HALF)
    moba_cos_ref[...] = jnp.where(first, cos_up, jnp.where(second, pltpu.roll(cos_up, MOBA_HALF, 1), 1.0))
    moba_lo_ref[...] = jnp.where(first, -sin_up, 0.0)
    moba_hi_ref[...] = jnp.where(second, pltpu.roll(sin_up, MOBA_HALF, 1), 0.0)


def _proj_tables_kernel(x_ref, g_ref, w_ref, pos_ref, inv_ref, o_ref, *table_and_scratch):
    table_refs, xn_ref = table_and_scratch[:-1], table_and_scratch[-1]
    trows = pos_ref.shape[0] // PROJ_PIECES

    def table_piece(c):
        piece = slice(c * trows, (c + 1) * trows)
        _rope_table_kernel(pos_ref.at[piece, :], inv_ref, *(r.at[piece, :] for r in table_refs))

    _norm_matmul_pieces(x_ref, g_ref, w_ref, o_ref, xn_ref, between=table_piece)


def project_with_rope_tables(x, gain, w, positions, tm, tn):
    t, d = x.shape
    n = w.shape[1]
    n_steps = n // tn
    rows = tm // n_steps
    inv_ret = RET_THETA ** (-jnp.arange(RET_HALF, dtype=F32) / RET_HALF)
    inv_moba = ROPE_THETA ** (-jnp.arange(MOBA_HALF, dtype=F32) / MOBA_HALF)
    inv = jnp.concatenate([inv_ret, inv_moba, jnp.zeros((HEAD_DIM - RET_HALF - MOBA_HALF,), F32)])
    posf = jnp.broadcast_to(positions.astype(F32).reshape(t, 1), (t, HEAD_DIM))
    table_spec = pl.BlockSpec((rows, HEAD_DIM), lambda i, j: (i * n_steps + j, 0))
    return pl.pallas_call(
        _proj_tables_kernel,
        out_shape=[jax.ShapeDtypeStruct((t, n), BF16)] + [jax.ShapeDtypeStruct((t, HEAD_DIM), F32)] * 5,
        grid=(t // tm, n_steps),
        in_specs=[
            pl.BlockSpec((tm, d), lambda i, j: (i, 0)),
            pl.BlockSpec((1, d), lambda i, j: (0, 0)),
            pl.BlockSpec((d, tn), lambda i, j: (0, j)),
            table_spec,
            pl.BlockSpec((1, HEAD_DIM), lambda i, j: (0, 0)),
        ],
        out_specs=[pl.BlockSpec((tm, tn), lambda i, j: (i, j))] + [table_spec] * 5,
        scratch_shapes=[pltpu.VMEM((tm, d), BF16)],
        compiler_params=_params("arbitrary", "arbitrary"),
        name="in_proj_even",
    )(x, gain.reshape(1, d), w, posf, inv.reshape(1, HEAD_DIM))


def _moba_kernel(q_ref, k_ref, v_ref, cos_ref, sin_lo_ref, sin_hi_ref, o_ref,
                 qr_ref, ka_ref, va_ref, kmean_ref):
    seq = q_ref.shape[0]
    blk = MOBA_BLOCK
    nb = seq // blk
    exp2_scale = HEAD_DIM ** -0.5 * LOG2E

    row = lax.broadcasted_iota(jnp.int32, (blk, blk), 0)
    colm = lax.broadcasted_iota(jnp.int32, (blk, blk), 1)
    causal = colm <= row
    eye = jnp.where(colm == row, 1.0, 0.0).astype(BF16)
    blk_row = lax.broadcasted_iota(jnp.int32, (nb, blk), 0)
    lane = lax.broadcasted_iota(jnp.int32, (blk, HEAD_DIM), 1)
    kmean_ref[...] = jnp.zeros(kmean_ref.shape, kmean_ref.dtype)

    def prepare(qb):
        rows = slice(qb * blk, (qb + 1) * blk)

        def rope(x):
            return (x * cos_ref[rows, :]
                    + pltpu.roll(x, HEAD_DIM - MOBA_HALF, 1) * sin_lo_ref[rows, :]
                    + pltpu.roll(x, MOBA_HALF, 1) * sin_hi_ref[rows, :])

        kr = rope(k_ref[rows, :].astype(F32))
        kmean_ref[qb:qb + 1, :] = jnp.mean(kr, axis=0, keepdims=True)
        ka_ref[rows, :HEAD_DIM] = kr.astype(BF16)
        ka_ref[rows, HEAD_DIM:] = jnp.where(lane == qb, MASKED, 0.0).astype(BF16)
        qr_ref[rows, :] = (rope(q_ref[rows, :].astype(F32)) * exp2_scale).astype(BF16)
        va_ref[rows, :HEAD_DIM] = v_ref[rows, :]
        va_ref[rows, HEAD_DIM:] = jnp.ones((blk, HEAD_DIM), BF16)

    def scores(qb):
        rows = slice(qb * blk, (qb + 1) * blk)
        q = qr_ref[rows, :]
        n_keys = (qb + 1) * blk
        if qb > MOBA_TOPK:
            gate = _dot_nt(kmean_ref[...].astype(BF16), q)
            rank = jnp.zeros((nb, blk), F32)
            for j in range(qb):
                gj = gate[j:j + 1, :]
                ahead = (gj > gate) | ((gj == gate) & (blk_row > j))
                rank = rank + jnp.where(ahead, 1.0, 0.0)
            drop = jnp.where((rank >= float(MOBA_TOPK)) & (blk_row < qb), 1.0, 0.0)
            drop = jnp.concatenate([drop, jnp.zeros((HEAD_DIM - nb, blk), F32)], axis=0).astype(BF16)
            drop_col = _dot_nt(eye, drop).astype(BF16)
            s = _dot_nt(jnp.concatenate([q, drop_col], axis=1), ka_ref[:n_keys, :])
        else:
            s = _dot_nt(q, ka_ref[:n_keys, :HEAD_DIM])
        own = jnp.where(causal, s[:, qb * blk:], MASKED)
        return jnp.concatenate([s[:, :qb * blk], own], axis=1) if qb else own

    prepare(0)
    s_next = scores(0)
    for qb in range(nb):
        rows = slice(qb * blk, (qb + 1) * blk)
        s = s_next
        if qb + 1 < nb:
            prepare(qb + 1)
            s_next = scores(qb + 1)
        m = jnp.max(s, axis=-1, keepdims=True)
        p = jnp.exp2(s - m).astype(BF16)
        o = _dot(p, va_ref[:(qb + 1) * blk, :])
        o_ref[rows, :] = (o[:, :HEAD_DIM] / o[:, HEAD_DIM:]).astype(o_ref.dtype)


def moba_mixer(h3, cos, sin_lo, sin_hi, col_q, col_k, col_v):
    bsz, seq, _ = h3.shape

    def col(off):
        return pl.BlockSpec((None, seq, HEAD_DIM), lambda b, h: (b, 0, off + h))

    table = pl.BlockSpec((None, seq, HEAD_DIM), lambda b, h: (b, 0, 0))
    return pl.pallas_call(
        _moba_kernel,
        out_shape=jax.ShapeDtypeStruct((bsz, seq, WIDTH), BF16),
        grid=(bsz, N_HEADS),
        in_specs=[col(col_q), col(col_k), col(col_v), table, table, table],
        out_specs=pl.BlockSpec((None, seq, HEAD_DIM), lambda b, h: (b, 0, h)),
        scratch_shapes=[pltpu.VMEM((seq, HEAD_DIM), BF16), pltpu.VMEM((seq, 2 * HEAD_DIM), BF16),
                        pltpu.VMEM((seq, 2 * HEAD_DIM), BF16), pltpu.VMEM((seq // MOBA_BLOCK, HEAD_DIM), F32)],
        compiler_params=_params("arbitrary", "arbitrary"),
        name="moba",
    )(h3, h3, h3, cos, sin_lo, sin_hi)


def _rglru_kernel(x_ref, cw_ref, cb_ref, wa_ref, ba_ref, wx_ref, bx_ref, lam_ref,
                  o_ref, xpad_ref):
    seq, width = x_ref.shape
    sub = 8
    part_rows = seq // RG_PARTS
    part_groups = part_rows // sub

    xpad_ref[:sub, :] = jnp.zeros((sub, width), F32)
    xpad_ref[sub:, :] = x_ref[...].astype(F32)
    nl = -lam_ref[...]
    softplus = jnp.maximum(nl, 0.0) + jnp.log1p(jnp.exp(-jnp.abs(nl)))
    log2_a_scale = (-RG_C * LOG2E) * softplus
    s_idx = lax.broadcasted_iota(jnp.int32, (part_groups, sub, width), 1)
    carried = [jnp.zeros((sub, width), F32)]

    def part(p):
        r0 = p * part_rows
        xf = cb_ref[...]
        for d in range(CONV_WIDTH):
            xf = xf + (xpad_ref[sub - d + r0:sub - d + r0 + part_rows, :]
                       * cw_ref[CONV_WIDTH - 1 - d:CONV_WIDTH - d, :])
        xb = xf.astype(BF16)
        yield

        def gate(w_ref, b_ref):
            pre = [_dot(xb[:, j * HEAD_DIM:(j + 1) * HEAD_DIM], w_ref[j]) for j in range(width // HEAD_DIM)]
            return _sigmoid(jnp.concatenate(pre, axis=1) + b_ref[...])

        r = gate(wa_ref, ba_ref)
        ig = gate(wx_ref, bx_ref)
        yield
        a = jnp.exp2(r * log2_a_scale)
        u = jnp.sqrt(1.0 - a * a) * ig * xf
        yield
        a3 = a.reshape(part_groups, sub, width)
        u3 = u.reshape(part_groups, sub, width)
        for d in (1, 2, 4):
            keep = s_idx >= d
            a_prev = jnp.where(keep, pltpu.roll(a3, d, 1), 1.0)
            u_prev = jnp.where(keep, pltpu.roll(u3, d, 1), 0.0)
            u3 = u3 + a3 * u_prev
            a3 = a3 * a_prev
        yield
        carry = carried[0]
        hs = []
        for g in range(part_groups):
            hg = a3[g] * carry + u3[g]
            hs.append(hg)
            carry = jnp.broadcast_to(hg[sub - 1:sub, :], (sub, width))
        carried[0] = carry
        o_ref[r0:r0 + part_rows, :] = jnp.concatenate(hs, axis=0).astype(o_ref.dtype)
        yield

    _run_staggered([part(p) for p in range(RG_PARTS)], n_stages=5)


def rglru_mixer(h3, conv_w, conv_b, wa, ba, wx, bx, lam, col_x, blocks_per_step=4):
    bsz, seq, _ = h3.shape
    width = blocks_per_step * HEAD_DIM

    def col(off):
        return pl.BlockSpec((None, seq, width), lambda b, c: (b, 0, off // blocks_per_step + c))

    def vec(rows):
        return pl.BlockSpec((rows, width), lambda b, c: (0, c))

    wspec = pl.BlockSpec((blocks_per_step, HEAD_DIM, HEAD_DIM), lambda b, c: (c, 0, 0))
    return pl.pallas_call(
        _rglru_kernel,
        out_shape=jax.ShapeDtypeStruct((bsz, seq, WIDTH), BF16),
        grid=(bsz, WIDTH // width),
        in_specs=[col(col_x), vec(CONV_WIDTH), vec(1), wspec, vec(1), wspec, vec(1), vec(1)],
        out_specs=pl.BlockSpec((None, seq, width), lambda b, c: (b, 0, c)),
        scratch_shapes=[pltpu.VMEM((seq + 8, width), F32)],
        compiler_params=_params("arbitrary", "arbitrary"),
        name="rglru",
    )(h3, conv_w, conv_b.reshape(1, WIDTH), wa.astype(BF16), ba.reshape(1, WIDTH),
      wx.astype(BF16), bx.reshape(1, WIDTH), lam.reshape(1, WIDTH))


def _retention_kernel(q_ref, k_ref, v_ref, cos_ref, sin_ref, logg_ref, gain_ref, bias_ref,
                      o_ref, st_ref):
    seq = q_ref.shape[0]
    ck = RET_CHUNK
    half = RET_HALF
    nc = seq // ck // RET_PARTS

    def chunks(x):
        return x.reshape(nc, ck, HEAD_DIM)

    def bmm(spec, a, b):
        return jnp.einsum(spec, a, b, preferred_element_type=F32)

    log_g = logg_ref[...]
    ri = lax.broadcasted_iota(jnp.int32, (ck, ck), 0)
    ci = lax.broadcasted_iota(jnp.int32, (ck, ck), 1)
    diff = (ri - ci).astype(F32)
    dmask = jnp.where(ri >= ci, jnp.exp(jnp.maximum(diff, 0.0) * log_g[:, :1]), 0.0)
    idx = lax.broadcasted_iota(jnp.int32, (ck, HEAD_DIM), 0).astype(F32)
    q_fac = jnp.exp((idx + 1.0) * log_g)
    k_fac = jnp.exp((ck - 1.0 - idx) * log_g)
    chunk_decay = jnp.exp(float(ck) * log_g)

    carried = [jnp.zeros((HEAD_DIM, HEAD_DIM), F32)]

    def part(p):
        rows = slice(p * nc * ck, (p + 1) * nc * ck)
        cos = cos_ref[rows, :]
        sin = sin_ref[rows, :]
        q = q_ref[rows, :].astype(F32)
        k = k_ref[rows, :].astype(F32)
        q = chunks(q * cos + pltpu.roll(q, half, 1) * sin)
        k = chunks((k * cos + pltpu.roll(k, half, 1) * sin) * (HEAD_DIM ** -0.5))
        v = chunks(v_ref[rows, :])
        yield
        att = bmm('cid,cjd->cij', q.astype(BF16), k.astype(BF16)) * dmask[None]
        k_end_t = jnp.swapaxes(k * k_fac[None], 1, 2).astype(BF16)
        incr = bmm('ckn,cnv->ckv', k_end_t, v)
        yield
        o_intra = bmm('cij,cjd->cid', att.astype(BF16), v)
        state = carried[0]
        for c in range(nc):
            st_ref[p * nc + c] = state.astype(BF16)
            state = state * chunk_decay + incr[c]
        carried[0] = state
        yield
        o = o_intra + bmm('cnk,ckv->cnv', (q * q_fac[None]).astype(BF16), st_ref[p * nc:(p + 1) * nc])
        o = o.reshape(nc * ck, HEAD_DIM)
        mu = jnp.mean(o, axis=-1, keepdims=True)
        oc = o - mu
        var = jnp.mean(oc * oc, axis=-1, keepdims=True)
        o_ref[rows, :] = (oc * lax.rsqrt(var + EPS) * gain_ref[...] + bias_ref[...]).astype(o_ref.dtype)
        yield

    _run_staggered([part(p) for p in range(RET_PARTS)], n_stages=4)


def retention_mixer(h3, cos, sin, log_g, gain, bias, col_q, col_k, col_v):
    bsz, seq, _ = h3.shape

    def col(off):
        return pl.BlockSpec((None, seq, HEAD_DIM), lambda b, h: (b, 0, off + h))

    table = pl.BlockSpec((None, seq, HEAD_DIM), lambda b, h: (b, 0, 0))
    vec = pl.BlockSpec((None, 1, HEAD_DIM), lambda b, h: (h, 0, 0))
    return pl.pallas_call(
        _retention_kernel,
        out_shape=jax.ShapeDtypeStruct((bsz, seq, WIDTH), BF16),
        grid=(bsz, N_HEADS),
        in_specs=[col(col_q), col(col_k), col(col_v), table, table, vec, vec, vec],
        out_specs=pl.BlockSpec((None, seq, HEAD_DIM), lambda b, h: (b, 0, h)),
        scratch_shapes=[pltpu.VMEM((seq // RET_CHUNK, HEAD_DIM, HEAD_DIM), BF16)],
        compiler_params=_params("arbitrary", "arbitrary"),
        name="retention",
    )(h3, h3, h3, cos, sin, log_g,
      gain.reshape(N_HEADS, 1, HEAD_DIM), bias.reshape(N_HEADS, 1, HEAD_DIM))


def _post_kernel(x_ref, ya_ref, yb_ref, z_ref, woa_ref, wob_ref, g_ref, wq_ref, kv_ref, wo_ref, fg_ref,
                 o_ref, *, final):
    part_rows = x_ref.shape[0] // POST_PARTS
    scale = HEAD_DIM ** -0.5

    def gated(y_ref, rows, cols):
        return (y_ref[rows, :].astype(F32) * _silu(z_ref[rows, cols].astype(F32))).astype(BF16)

    def part(p):
        rows = slice(p * part_rows, (p + 1) * part_rows)
        ga = gated(ya_ref, rows, slice(0, WIDTH))
        yield
        gb = gated(yb_ref, rows, slice(WIDTH, 2 * WIDTH))
        x1 = x_ref[rows, :] + _dot(ga, woa_ref[...])
        yield
        x1 = x1 + _dot(gb, wob_ref[...])
        yield
        ms = jnp.mean(x1 * x1, axis=-1, keepdims=True)
        xn = (x1 * lax.rsqrt(ms + EPS) * g_ref[...]).astype(BF16)
        q = _dot(xn, wq_ref[...]).astype(BF16)
        yield
        head_cols = [slice(hd * HEAD_DIM, (hd + 1) * HEAD_DIM) for hd in range(XA_HEADS)]
        scores = [_dot_nt(q[:, cs], kv_ref[:, cs]) * scale for cs in head_cols]
        yield
        probs = []
        for s in scores:
            pr = jnp.exp(s - jnp.max(s, axis=-1, keepdims=True))
            probs.append((pr / jnp.sum(pr, axis=-1, keepdims=True)).astype(BF16))
        yield
        o = jnp.concatenate([_dot(pr, kv_ref[:, XA_WIDTH + cs.start:XA_WIDTH + cs.stop]).astype(BF16)
                             for pr, cs in zip(probs, head_cols)], axis=1)
        yield
        x2 = x1 + _dot(o, wo_ref[...])
        if final:
            ms2 = jnp.mean(x2 * x2, axis=-1, keepdims=True)
            x2 = x2 * lax.rsqrt(ms2 + EPS) * fg_ref[...]
        o_ref[rows, :] = x2
        yield

    _run_staggered([part(p) for p in range(POST_PARTS)], n_stages=8)


def post_mixer(x3, ya, yb, h3, z_block, w_out, xa_gain, wq, kv, wo, final_gain, final, tm=1024):
    bsz, seq, d = x3.shape
    full = lambda shape: pl.BlockSpec(shape, lambda b, i: (0,) * len(shape))
    tile = lambda w: pl.BlockSpec((None, tm, w), lambda b, i: (b, i, 0))
    return pl.pallas_call(
        functools.partial(_post_kernel, final=final),
        out_shape=jax.ShapeDtypeStruct((bsz, seq, d), F32),
        grid=(bsz, seq // tm),
        in_specs=[tile(d), tile(WIDTH), tile(WIDTH),
                  pl.BlockSpec((None, tm, 2 * WIDTH), lambda b, i: (b, i, z_block)),
                  full((WIDTH, d)), full((WIDTH, d)), full((1, d)), full((d, XA_WIDTH)),
                  pl.BlockSpec((None, MEM_LEN, 2 * XA_WIDTH), lambda b, i: (b, 0, 0)),
                  full((XA_WIDTH, d)), full((1, d))],
        out_specs=tile(d),
        compiler_params=_params("arbitrary", "arbitrary"),
        name="post_final" if final else "post",
    )(x3, ya, yb, h3, w_out[:WIDTH].astype(BF16), w_out[WIDTH:].astype(BF16), xa_gain.reshape(1, d),
      wq.astype(BF16), kv, wo.astype(BF16), final_gain.reshape(1, d))


def kernel(x, mem, positions, hgrn_lb_logits, ev_norm, ev_w_in, ev_hgrn_gain, ev_w_out, od_norm, od_w_in, od_conv_w, od_conv_b, od_rg_wa, od_rg_ba, od_rg_wx, od_rg_bx, od_rg_lambda, od_ret_gain, od_ret_bias, od_w_out, xa_norm, xa_mem_norm, xa_wq, xa_wkv, xa_wo, final_norm):
    bsz, seq, d = x.shape
    tokens = bsz * seq
    lb_all = jnp.cumsum(jax.nn.softmax(hgrn_lb_logits.astype(F32), axis=0), axis=0)
    log_g = jnp.log(1.0 - 2.0 ** (-5.0 - jnp.arange(N_HEADS, dtype=F32)))
    log_g = jnp.broadcast_to(log_g[:, None, None], (N_HEADS, 1, HEAD_DIM))
    mem2 = mem.reshape(bsz * MEM_LEN, d)

    def memory_kv(layer):
        kv = norm_matmul(mem2, xa_mem_norm[layer], xa_wkv[layer], BF16,
                         tm=bsz * MEM_LEN // 2, tn=2 * XA_WIDTH, name="mem_kv")
        return kv.reshape(bsz, MEM_LEN, 2 * XA_WIDTH)

    h, *tables = project_with_rope_tables(x.reshape(tokens, d), ev_norm[0], ev_w_in.reshape(ev_w_in.shape[1:]),
                                          positions, tm=seq, tn=1024)
    ret_cos, ret_sin, moba_cos, moba_lo, moba_hi = (tb.reshape(bsz, seq, HEAD_DIM) for tb in tables)
    h3 = h.reshape(bsz, seq, -1)
    nb = WIDTH // HEAD_DIM
    ya = hgrn2_mixer(h3, lb_all[0], ev_hgrn_gain[0], 0, nb, 2 * nb)
    yb = moba_mixer(h3, moba_cos, moba_lo, moba_hi, 3 * nb, 4 * nb, 5 * nb)
    x1 = post_mixer(x, ya, yb, h3, 3, ev_w_out[0], xa_norm[0], xa_wq[0], memory_kv(0), xa_wo[0],
                    final_norm, final=False)

    h = norm_matmul(x1.reshape(tokens, d), od_norm[0], od_w_in.reshape(od_w_in.shape[1:]), BF16,
                    tm=2048, tn=1024, name="in_proj_odd")
    h3 = h.reshape(bsz, seq, -1)
    yc = rglru_mixer(h3, od_conv_w[0], od_conv_b[0], od_rg_wa[0], od_rg_ba[0], od_rg_wx[0],
                     od_rg_bx[0], od_rg_lambda[0], 0)
    yd = retention_mixer(h3, ret_cos, ret_sin, log_g, od_ret_gain[0], od_ret_bias[0],
                         nb, 2 * nb, 3 * nb)
    return post_mixer(x1, yc, yd, h3, 2, od_w_out[0], xa_norm[1], xa_wq[1], memory_kv(1), xa_wo[1],
                      final_norm, final=True)
```

```python
import functools
import math

import jax
import jax.numpy as jnp
from jax import lax
from jax.experimental import pallas as pl
from jax.experimental.pallas import tpu as pltpu

F32 = jnp.float32
BF16 = jnp.bfloat16

D_MODEL = 1024
HEAD_DIM = 128
N_HEADS = 8
WIDTH = N_HEADS * HEAD_DIM
MEM_LEN = 256
HGRN_CHUNK = 32
HGRN_TILE = 128
HGRN_PARTS = 16
MOBA_BLOCK = 256
MOBA_TOPK = 3
RET_CHUNK = 256
RET_PARTS = 2
CONV_WIDTH = 4
RG_C = 8.0
ROPE_THETA = 500000.0
ROPE_DIM = HEAD_DIM // 4
RET_THETA = 10000.0
PROJ_PIECES = 4
POST_PARTS = 4
XA_HEADS = 4
XA_WIDTH = XA_HEADS * HEAD_DIM
EPS = 1e-6
V7X_VMEM_BYTES = 64 * 1024 * 1024
VMEM_LIMIT = V7X_VMEM_BYTES * 7 // 8
PROJ_TN = 1024
POST_TM = 1024
RG_BLOCKS = 4
MASKED = -1e30

_NT = (((1,), (1,)), ((), ()))


def _dot(a, b):
    return jnp.dot(a, b, preferred_element_type=F32)


def _dot_nt(a, b):
    return lax.dot_general(a, b, _NT, preferred_element_type=F32)


LOG2E = math.log2(math.e)


def _sigmoid(x):
    return 1.0 / (1.0 + jnp.exp2(x * (-LOG2E)))


def _silu(x):
    return x * _sigmoid(x)


def _run_staggered(parts, n_stages):
    for step in range(n_stages + len(parts) - 1):
        for p, gen in enumerate(parts):
            if 0 <= step - p < n_stages:
                next(gen)


def _params(*sem):
    return pltpu.CompilerParams(dimension_semantics=sem, vmem_limit_bytes=VMEM_LIMIT)


def _norm_matmul_pieces(x_ref, g_ref, w_ref, o_ref, xn_ref, between=None):
    rows = x_ref.shape[0] // PROJ_PIECES

    def run(with_norm):
        w = w_ref[...].astype(BF16)
        for c in range(PROJ_PIECES):
            piece = slice(c * rows, (c + 1) * rows)
            if with_norm:
                x = x_ref[piece, :]
                ms = jnp.mean(x * x, axis=-1, keepdims=True)
                xn_ref[piece, :] = (x * lax.rsqrt(ms + EPS) * g_ref[...]).astype(BF16)
            o_ref[piece, :] = _dot(xn_ref[piece, :], w).astype(o_ref.dtype)
            if between is not None:
                between(c)

    @pl.when(pl.program_id(1) == 0)
    def _():
        run(True)

    @pl.when(pl.program_id(1) != 0)
    def _():
        run(False)


def _norm_matmul_kernel(x_ref, g_ref, w_ref, o_ref, xn_ref):
    _norm_matmul_pieces(x_ref, g_ref, w_ref, o_ref, xn_ref)


def norm_matmul(x, gain, w, out_dtype, tm, tn, name):
    t, d = x.shape
    n = w.shape[1]
    return pl.pallas_call(
        _norm_matmul_kernel,
        out_shape=jax.ShapeDtypeStruct((t, n), out_dtype),
        grid=(t // tm, n // tn),
        in_specs=[
            pl.BlockSpec((tm, d), lambda i, j: (i, 0)),
            pl.BlockSpec((1, d), lambda i, j: (0, 0)),
            pl.BlockSpec((d, tn), lambda i, j: (0, j)),
        ],
        out_specs=pl.BlockSpec((tm, tn), lambda i, j: (i, j)),
        scratch_shapes=[pltpu.VMEM((tm, d), BF16)],
        compiler_params=_params("arbitrary", "arbitrary"),
        name=name,
    )(x, gain.reshape(1, d), w)


def _split2(x):
    hi = x.astype(BF16)
    lo = (x - hi.astype(F32)).astype(BF16)
    return hi, lo


def _hgrn2_kernel(q_ref, f_ref, i_ref, lb_ref, gain_ref, o_ref, st_ref):
    seq = q_ref.shape[0]
    tile, chunk = HGRN_TILE, HGRN_CHUNK
    per_tile = tile // chunk
    part_rows = seq // HGRN_PARTS
    nt = part_rows // tile
    lb = lb_ref[...]
    gain = gain_ref[...]

    def tiles(x):
        return x.reshape(nt, tile, HEAD_DIM)

    def bmm(spec, a, b):
        return jnp.einsum(spec, a, b, preferred_element_type=F32)

    ti = lax.broadcasted_iota(jnp.int32, (tile, tile), 0)
    tj = lax.broadcasted_iota(jnp.int32, (tile, tile), 1)
    intra_mask = (((ti // chunk) == (tj // chunk)) & (tj <= ti))[None]
    sum_mat = jnp.broadcast_to(jnp.where(intra_mask, 1.0, 0.0).astype(BF16), (nt, tile, tile))
    row_chunk = (lax.broadcasted_iota(jnp.int32, (tile, HEAD_DIM), 0) // chunk)[None]

    def per_chunk_lanes(x):
        zero = jnp.zeros_like(x)
        return jnp.concatenate([jnp.where(row_chunk == ci, x, zero) for ci in range(per_tile)], axis=-1)

    carried = [jnp.zeros((HEAD_DIM, HEAD_DIM), F32)]

    def part(p):
        rows = slice(p * part_rows, (p + 1) * part_rows)
        f = lb + (1.0 - lb) * _sigmoid(f_ref[rows, :].astype(F32))
        k = tiles(1.0 - f)
        hi, lo = _split2(jnp.log2(f))
        yield
        b = bmm('trc,tcd->trd', sum_mat, tiles(hi)) + bmm('trc,tcd->trd', sum_mat, tiles(lo))
        b_chunks = b.reshape(part_rows // chunk, chunk, HEAD_DIM)
        b_tot = jnp.broadcast_to(b_chunks[:, chunk - 1:, :], b_chunks.shape).reshape(b.shape)
        yield
        q_dec = (tiles(q_ref[rows, :].astype(F32)) * jnp.exp2(b)).astype(BF16)
        k_inv = (k * jnp.exp2(-b)).astype(BF16)
        k_end = (k * jnp.exp2(b_tot - b)).astype(BF16)
        v = tiles(i_ref[rows, :].astype(F32))
        yield
        att = bmm('tid,tjd->tij', q_dec, k_inv)
        att = jnp.where(intra_mask, att, 0.0).astype(BF16)
        v_t = jnp.swapaxes(v, 1, 2).astype(BF16)
        incr = bmm('tvn,tnk->tvk', v_t, per_chunk_lanes(k_end))
        yield
        o_intra = bmm('tij,tjd->tid', att, v.astype(BF16))
        state_t = carried[0]
        for t in range(nt):
            for ci in range(per_tile):
                lanes = slice(ci * HEAD_DIM, (ci + 1) * HEAD_DIM)
                st_ref[p * nt + t, :, lanes] = state_t.astype(BF16)
                decay = jnp.exp2(b_tot[t, ci * chunk:ci * chunk + 1, :])
                state_t = state_t * decay + incr[t, :, lanes]
        carried[0] = state_t
        yield
        o = o_intra + bmm('tnk,tvk->tnv', per_chunk_lanes(q_dec), st_ref[p * nt:(p + 1) * nt])
        o = o.reshape(part_rows, HEAD_DIM)
        y = o * lax.rsqrt(jnp.mean(o * o, axis=-1, keepdims=True) + EPS) * gain
        o_ref[rows, :] = y.astype(o_ref.dtype)
        yield

    _run_staggered([part(p) for p in range(HGRN_PARTS)], n_stages=6)


def hgrn2_mixer(h3, lb, gain, col_q, col_f, col_i):
    bsz, seq, _ = h3.shape

    def col(off):
        return pl.BlockSpec((None, seq, HEAD_DIM), lambda b, h: (b, 0, off + h))

    vec = pl.BlockSpec((None, 1, HEAD_DIM), lambda b, h: (h, 0, 0))
    return pl.pallas_call(
        _hgrn2_kernel,
        out_shape=jax.ShapeDtypeStruct((bsz, seq, WIDTH), BF16),
        grid=(bsz, N_HEADS),
        in_specs=[col(col_q), col(col_f), col(col_i), vec, vec],
        out_specs=pl.BlockSpec((None, seq, HEAD_DIM), lambda b, h: (b, 0, h)),
        scratch_shapes=[pltpu.VMEM((seq // HGRN_TILE, HEAD_DIM, HGRN_TILE // HGRN_CHUNK * HEAD_DIM), BF16)],
        compiler_params=_params("arbitrary", "arbitrary"),
        name="hgrn2",
    )(h3, h3, h3, lb.reshape(N_HEADS, 1, HEAD_DIM), gain.reshape(N_HEADS, 1, HEAD_DIM))


RET_HALF = HEAD_DIM // 2
MOBA_HALF = ROPE_DIM // 2


def _rope_table_kernel(pos_ref, inv_ref, ret_cos_ref, ret_sin_ref, moba_cos_ref, moba_lo_ref, moba_hi_ref):
    ang = pos_ref[...] * inv_ref[...]
    cos = jnp.cos(ang)
    sin = jnp.sin(ang)
    lane = lax.broadcasted_iota(jnp.int32, ang.shape, 1)
    low = lane < RET_HALF
    cos_up = pltpu.roll(cos, RET_HALF, 1)
    sin_up = pltpu.roll(sin, RET_HALF, 1)
    ret_cos_ref[...] = jnp.where(low, cos, cos_up)
    ret_sin_ref[...] = jnp.where(low, -sin, sin_up)
    first = lane < MOBA_HALF
    second = (lane >= MOBA_HALF) & (lane < 2 * MOBA_HALF)
    moba_cos_ref[...] = jnp.where(first, cos_up, jnp.where(second, pltpu.roll(cos_up, MOBA_HALF, 1), 1.0))
    moba_lo_ref[...] = jnp.where(first, -sin_up, 0.0)
    moba_hi_ref[...] = jnp.where(second, pltpu.roll(sin_up, MOBA_HALF, 1), 0.0)


def _proj_tables_kernel(x_ref, g_ref, w_ref, pos_ref, inv_ref, o_ref, *table_and_scratch):
    table_refs, xn_ref = table_and_scratch[:-1], table_and_scratch[-1]
    trows = pos_ref.shape[0] // PROJ_PIECES

    def table_piece(c):
        piece = slice(c * trows, (c + 1) * trows)
        _rope_table_kernel(pos_ref.at[piece, :], inv_ref, *(r.at[piece, :] for r in table_refs))

    _norm_matmul_pieces(x_ref, g_ref, w_ref, o_ref, xn_ref, between=table_piece)


def project_with_rope_tables(x, gain, w, positions, tm, tn):
    t, d = x.shape
    n = w.shape[1]
    n_steps = n // tn
    rows = tm // n_steps
    inv_ret = RET_THETA ** (-jnp.arange(RET_HALF, dtype=F32) / RET_HALF)
    inv_moba = ROPE_THETA ** (-jnp.arange(MOBA_HALF, dtype=F32) / MOBA_HALF)
    inv = jnp.concatenate([inv_ret, inv_moba, jnp.zeros((HEAD_DIM - RET_HALF - MOBA_HALF,), F32)])
    posf = jnp.broadcast_to(positions.astype(F32).reshape(t, 1), (t, HEAD_DIM))
    table_spec = pl.BlockSpec((rows, HEAD_DIM), lambda i, j: (i * n_steps + j, 0))
    return pl.pallas_call(
        _proj_tables_kernel,
        out_shape=[jax.ShapeDtypeStruct((t, n), BF16)] + [jax.ShapeDtypeStruct((t, HEAD_DIM), F32)] * 5,
        grid=(t // tm, n_steps),
        in_specs=[
            pl.BlockSpec((tm, d), lambda i, j: (i, 0)),
            pl.BlockSpec((1, d), lambda i, j: (0, 0)),
            pl.BlockSpec((d, tn), lambda i, j: (0, j)),
            table_spec,
            pl.BlockSpec((1, HEAD_DIM), lambda i, j: (0, 0)),
        ],
        out_specs=[pl.BlockSpec((tm, tn), lambda i, j: (i, j))] + [table_spec] * 5,
        scratch_shapes=[pltpu.VMEM((tm, d), BF16)],
        compiler_params=_params("arbitrary", "arbitrary"),
        name="in_proj_even",
    )(x, gain.reshape(1, d), w, posf, inv.reshape(1, HEAD_DIM))


def _moba_kernel(q_ref, k_ref, v_ref, cos_ref, sin_lo_ref, sin_hi_ref, o_ref,
                 qr_ref, ka_ref, va_ref, kmean_ref):
    seq = q_ref.shape[0]
    blk = MOBA_BLOCK
    nb = seq // blk
    exp2_scale = HEAD_DIM ** -0.5 * LOG2E

    row = lax.broadcasted_iota(jnp.int32, (blk, blk), 0)
    colm = lax.broadcasted_iota(jnp.int32, (blk, blk), 1)
    causal = colm <= row
    eye = jnp.where(colm == row, 1.0, 0.0).astype(BF16)
    blk_row = lax.broadcasted_iota(jnp.int32, (nb, blk), 0)
    lane = lax.broadcasted_iota(jnp.int32, (blk, HEAD_DIM), 1)
    kmean_ref[...] = jnp.zeros(kmean_ref.shape, kmean_ref.dtype)

    def prepare(qb):
        rows = slice(qb * blk, (qb + 1) * blk)

        def rope(x):
            return (x * cos_ref[rows, :]
                    + pltpu.roll(x, HEAD_DIM - MOBA_HALF, 1) * sin_lo_ref[rows, :]
                    + pltpu.roll(x, MOBA_HALF, 1) * sin_hi_ref[rows, :])

        kr = rope(k_ref[rows, :].astype(F32))
        kmean_ref[qb:qb + 1, :] = jnp.mean(kr, axis=0, keepdims=True)
        ka_ref[rows, :HEAD_DIM] = kr.astype(BF16)
        ka_ref[rows, HEAD_DIM:] = jnp.where(lane == qb, MASKED, 0.0).astype(BF16)
        qr_ref[rows, :] = (rope(q_ref[rows, :].astype(F32)) * exp2_scale).astype(BF16)
        va_ref[rows, :HEAD_DIM] = v_ref[rows, :]
        va_ref[rows, HEAD_DIM:] = jnp.ones((blk, HEAD_DIM), BF16)

    def scores(qb):
        rows = slice(qb * blk, (qb + 1) * blk)
        q = qr_ref[rows, :]
        n_keys = (qb + 1) * blk
        if qb > MOBA_TOPK:
            gate = _dot_nt(kmean_ref[...].astype(BF16), q)
            rank = jnp.zeros((nb, blk), F32)
            for j in range(qb):
                gj = gate[j:j + 1, :]
                ahead = (gj > gate) | ((gj == gate) & (blk_row > j))
                rank = rank + jnp.where(ahead, 1.0, 0.0)
            drop = jnp.where((rank >= float(MOBA_TOPK)) & (blk_row < qb), 1.0, 0.0)
            drop = jnp.concatenate([drop, jnp.zeros((HEAD_DIM - nb, blk), F32)], axis=0).astype(BF16)
            drop_col = _dot_nt(eye, drop).astype(BF16)
            s = _dot_nt(jnp.concatenate([q, drop_col], axis=1), ka_ref[:n_keys, :])
        else:
            s = _dot_nt(q, ka_ref[:n_keys, :HEAD_DIM])
        own = jnp.where(causal, s[:, qb * blk:], MASKED)
        return jnp.concatenate([s[:, :qb * blk], own], axis=1) if qb else own

    prepare(0)
    s_next = scores(0)
    for qb in range(nb):
        rows = slice(qb * blk, (qb + 1) * blk)
        s = s_next
        if qb + 1 < nb:
            prepare(qb + 1)
            s_next = scores(qb + 1)
        m = jnp.max(s, axis=-1, keepdims=True)
        p = jnp.exp2(s - m).astype(BF16)
        o = _dot(p, va_ref[:(qb + 1) * blk, :])
        o_ref[rows, :] = (o[:, :HEAD_DIM] / o[:, HEAD_DIM:]).astype(o_ref.dtype)


def moba_mixer(h3, cos, sin_lo, sin_hi, col_q, col_k, col_v):
    bsz, seq, _ = h3.shape

    def col(off):
        return pl.BlockSpec((None, seq, HEAD_DIM), lambda b, h: (b, 0, off + h))

    table = pl.BlockSpec((None, seq, HEAD_DIM), lambda b, h: (b, 0, 0))
    return pl.pallas_call(
        _moba_kernel,
        out_shape=jax.ShapeDtypeStruct((bsz, seq, WIDTH), BF16),
        grid=(bsz, N_HEADS),
        in_specs=[col(col_q), col(col_k), col(col_v), table, table, table],
        out_specs=pl.BlockSpec((None, seq, HEAD_DIM), lambda b, h: (b, 0, h)),
        scratch_shapes=[pltpu.VMEM((seq, HEAD_DIM), BF16), pltpu.VMEM((seq, 2 * HEAD_DIM), BF16),
                        pltpu.VMEM((seq, 2 * HEAD_DIM), BF16), pltpu.VMEM((seq // MOBA_BLOCK, HEAD_DIM), F32)],
        compiler_params=_params("arbitrary", "arbitrary"),
        name="moba",
    )(h3, h3, h3, cos, sin_lo, sin_hi)


def _rglru_kernel(x_ref, cw_ref, cb_ref, wa_ref, ba_ref, wx_ref, bx_ref, lam_ref,
                  o_ref, a_ref, u_ref, xpad_ref):
    seq, width = x_ref.shape
    sub = 8
    groups = seq // sub

    xpad_ref[:sub, :] = jnp.zeros((sub, width), F32)
    xpad_ref[sub:, :] = x_ref[...].astype(F32)
    xf = cb_ref[...]
    for d in range(CONV_WIDTH):
        xf = xf + xpad_ref[sub - d:sub - d + seq, :] * cw_ref[CONV_WIDTH - 1 - d:CONV_WIDTH - d, :]

    xb = xf.astype(BF16)

    def gate(w_ref, b_ref):
        pre = [_dot(xb[:, j * HEAD_DIM:(j + 1) * HEAD_DIM], w_ref[j]) for j in range(width // HEAD_DIM)]
        return _sigmoid(jnp.concatenate(pre, axis=1) + b_ref[...])

    r = gate(wa_ref, ba_ref)
    ig = gate(wx_ref, bx_ref)
    nl = -lam_ref[...]
    softplus = jnp.maximum(nl, 0.0) + jnp.log1p(jnp.exp(-jnp.abs(nl)))
    a = jnp.exp2(r * ((-RG_C * LOG2E) * softplus))
    gap = 1.0 - a * a
    u = jnp.where(gap > 0.0, gap * lax.rsqrt(gap), 0.0) * ig * xf

    a3 = a.reshape(groups, sub, width)
    u3 = u.reshape(groups, sub, width)
    s_idx = lax.broadcasted_iota(jnp.int32, a3.shape, 1)
    for d in (1, 2, 4):
        keep = s_idx >= d
        a_prev = jnp.where(keep, pltpu.roll(a3, d, 1), 1.0)
        u_prev = jnp.where(keep, pltpu.roll(u3, d, 1), 0.0)
        u3 = u3 + a3 * u_prev
        a3 = a3 * a_prev
    a_ref[...] = a3
    u_ref[...] = u3

    def group_body(g, carry):
        hg = a_ref[g] * carry + u_ref[g]
        u_ref[g] = hg
        return jnp.broadcast_to(hg[sub - 1:sub, :], (sub, width))

    lax.fori_loop(0, groups, group_body, jnp.zeros((sub, width), F32), unroll=8)
    o_ref[...] = u_ref[...].reshape(seq, width).astype(o_ref.dtype)


def rglru_mixer(h3, conv_w, conv_b, wa, ba, wx, bx, lam, col_x, blocks_per_step=RG_BLOCKS):
    bsz, seq, _ = h3.shape
    width = blocks_per_step * HEAD_DIM

    def col(off):
        return pl.BlockSpec((None, seq, width), lambda b, c: (b, 0, off // blocks_per_step + c))

    def vec(rows):
        return pl.BlockSpec((rows, width), lambda b, c: (0, c))

    wspec = pl.BlockSpec((blocks_per_step, HEAD_DIM, HEAD_DIM), lambda b, c: (c, 0, 0))
    return pl.pallas_call(
        _rglru_kernel,
        out_shape=jax.ShapeDtypeStruct((bsz, seq, WIDTH), BF16),
        grid=(bsz, WIDTH // width),
        in_specs=[col(col_x), vec(CONV_WIDTH), vec(1), wspec, vec(1), wspec, vec(1), vec(1)],
        out_specs=pl.BlockSpec((None, seq, width), lambda b, c: (b, 0, c)),
        scratch_shapes=[pltpu.VMEM((seq // 8, 8, width), F32)] * 2 + [pltpu.VMEM((seq + 8, width), F32)],
        compiler_params=_params("arbitrary", "arbitrary"),
        name="rglru",
    )(h3, conv_w, conv_b.reshape(1, WIDTH), wa.astype(BF16), ba.reshape(1, WIDTH),
      wx.astype(BF16), bx.reshape(1, WIDTH), lam.reshape(1, WIDTH))


def _retention_kernel(q_ref, k_ref, v_ref, cos_ref, sin_ref, logg_ref, gain_ref, bias_ref,
                      o_ref, st_ref):
    seq = q_ref.shape[0]
    ck = RET_CHUNK
    half = RET_HALF
    nc = seq // ck // RET_PARTS

    def chunks(x):
        return x.reshape(nc, ck, HEAD_DIM)

    def bmm(spec, a, b):
        return jnp.einsum(spec, a, b, preferred_element_type=F32)

    log_g = logg_ref[...]
    ri = lax.broadcasted_iota(jnp.int32, (ck, ck), 0)
    ci = lax.broadcasted_iota(jnp.int32, (ck, ck), 1)
    diff = (ri - ci).astype(F32)
    dmask = jnp.where(ri >= ci, jnp.exp(jnp.maximum(diff, 0.0) * log_g[:, :1]), 0.0)
    idx = lax.broadcasted_iota(jnp.int32, (ck, HEAD_DIM), 0).astype(F32)
    q_fac = jnp.exp((idx + 1.0) * log_g)
    k_fac = jnp.exp((ck - 1.0 - idx) * log_g)
    chunk_decay = jnp.exp(float(ck) * log_g)

    carried = [jnp.zeros((HEAD_DIM, HEAD_DIM), F32)]

    def part(p):
        rows = slice(p * nc * ck, (p + 1) * nc * ck)
        cos = cos_ref[rows, :]
        sin = sin_ref[rows, :]
        q = q_ref[rows, :].astype(F32)
        k = k_ref[rows, :].astype(F32)
        q = chunks(q * cos + pltpu.roll(q, half, 1) * sin)
        k = chunks((k * cos + pltpu.roll(k, half, 1) * sin) * (HEAD_DIM ** -0.5))
        v = chunks(v_ref[rows, :])
        yield
        att = bmm('cid,cjd->cij', q.astype(BF16), k.astype(BF16)) * dmask[None]
        k_end_t = jnp.swapaxes(k * k_fac[None], 1, 2).astype(BF16)
        incr = bmm('ckn,cnv->ckv', k_end_t, v)
        yield
        o_intra = bmm('cij,cjd->cid', att.astype(BF16), v)
        state = carried[0]
        for c in range(nc):
            st_ref[p * nc + c] = state.astype(BF16)
            state = state * chunk_decay + incr[c]
        carried[0] = state
        yield
        o = o_intra + bmm('cnk,ckv->cnv', (q * q_fac[None]).astype(BF16), st_ref[p * nc:(p + 1) * nc])
        o = o.reshape(nc * ck, HEAD_DIM)
        mu = jnp.mean(o, axis=-1, keepdims=True)
        oc = o - mu
        var = jnp.mean(oc * oc, axis=-1, keepdims=True)
        o_ref[rows, :] = (oc * lax.rsqrt(var + EPS) * gain_ref[...] + bias_ref[...]).astype(o_ref.dtype)
        yield

    _run_staggered([part(p) for p in range(RET_PARTS)], n_stages=4)


def retention_mixer(h3, cos, sin, log_g, gain, bias, col_q, col_k, col_v):
    bsz, seq, _ = h3.shape

    def col(off):
        return pl.BlockSpec((None, seq, HEAD_DIM), lambda b, h: (b, 0, off + h))

    table = pl.BlockSpec((None, seq, HEAD_DIM), lambda b, h: (b, 0, 0))
    vec = pl.BlockSpec((None, 1, HEAD_DIM), lambda b, h: (h, 0, 0))
    return pl.pallas_call(
        _retention_kernel,
        out_shape=jax.ShapeDtypeStruct((bsz, seq, WIDTH), BF16),
        grid=(bsz, N_HEADS),
        in_specs=[col(col_q), col(col_k), col(col_v), table, table, vec, vec, vec],
        out_specs=pl.BlockSpec((None, seq, HEAD_DIM), lambda b, h: (b, 0, h)),
        scratch_shapes=[pltpu.VMEM((seq // RET_CHUNK, HEAD_DIM, HEAD_DIM), BF16)],
        compiler_params=_params("arbitrary", "arbitrary"),
        name="retention",
    )(h3, h3, h3, cos, sin, log_g,
      gain.reshape(N_HEADS, 1, HEAD_DIM), bias.reshape(N_HEADS, 1, HEAD_DIM))


def _post_kernel(x_ref, ya_ref, yb_ref, z_ref, woa_ref, wob_ref, g_ref, wq_ref, kv_ref, wo_ref, fg_ref,
                 o_ref, *, final):
    part_rows = x_ref.shape[0] // POST_PARTS
    scale = HEAD_DIM ** -0.5

    def gated(y_ref, rows, cols):
        return (y_ref[rows, :].astype(F32) * _silu(z_ref[rows, cols].astype(F32))).astype(BF16)

    def part(p):
        rows = slice(p * part_rows, (p + 1) * part_rows)
        ga = gated(ya_ref, rows, slice(0, WIDTH))
        yield
        gb = gated(yb_ref, rows, slice(WIDTH, 2 * WIDTH))
        x1 = x_ref[rows, :] + _dot(ga, woa_ref[...])
        yield
        x1 = x1 + _dot(gb, wob_ref[...])
        yield
        ms = jnp.mean(x1 * x1, axis=-1, keepdims=True)
        xn = (x1 * lax.rsqrt(ms + EPS) * g_ref[...]).astype(BF16)
        q = _dot(xn, wq_ref[...]).astype(BF16)
        yield
        head_cols = [slice(hd * HEAD_DIM, (hd + 1) * HEAD_DIM) for hd in range(XA_HEADS)]
        scores = [_dot_nt(q[:, cs], kv_ref[:, cs]) * scale for cs in head_cols]
        yield
        probs = []
        for s in scores:
            pr = jnp.exp(s - jnp.max(s, axis=-1, keepdims=True))
            probs.append((pr / jnp.sum(pr, axis=-1, keepdims=True)).astype(BF16))
        yield
        o = jnp.concatenate([_dot(pr, kv_ref[:, XA_WIDTH + cs.start:XA_WIDTH + cs.stop]).astype(BF16)
                             for pr, cs in zip(probs, head_cols)], axis=1)
        yield
        x2 = x1 + _dot(o, wo_ref[...])
        if final:
            ms2 = jnp.mean(x2 * x2, axis=-1, keepdims=True)
            x2 = x2 * lax.rsqrt(ms2 + EPS) * fg_ref[...]
        o_ref[rows, :] = x2
        yield

    _run_staggered([part(p) for p in range(POST_PARTS)], n_stages=8)


def post_mixer(x3, ya, yb, h3, z_block, w_out, xa_gain, wq, kv, wo, final_gain, final, tm=POST_TM):
    bsz, seq, d = x3.shape
    full = lambda shape: pl.BlockSpec(shape, lambda b, i: (0,) * len(shape))
    tile = lambda w: pl.BlockSpec((None, tm, w), lambda b, i: (b, i, 0))
    return pl.pallas_call(
        functools.partial(_post_kernel, final=final),
        out_shape=jax.ShapeDtypeStruct((bsz, seq, d), F32),
        grid=(bsz, seq // tm),
        in_specs=[tile(d), tile(WIDTH), tile(WIDTH),
                  pl.BlockSpec((None, tm, 2 * WIDTH), lambda b, i: (b, i, z_block)),
                  full((WIDTH, d)), full((WIDTH, d)), full((1, d)), full((d, XA_WIDTH)),
                  pl.BlockSpec((None, MEM_LEN, 2 * XA_WIDTH), lambda b, i: (b, 0, 0)),
                  full((XA_WIDTH, d)), full((1, d))],
        out_specs=tile(d),
        compiler_params=_params("arbitrary", "arbitrary"),
        name="post_final" if final else "post",
    )(x3, ya, yb, h3, w_out[:WIDTH].astype(BF16), w_out[WIDTH:].astype(BF16), xa_gain.reshape(1, d),
      wq.astype(BF16), kv, wo.astype(BF16), final_gain.reshape(1, d))


def kernel(x, mem, positions, hgrn_lb_logits, ev_norm, ev_w_in, ev_hgrn_gain, ev_w_out, od_norm, od_w_in, od_conv_w, od_conv_b, od_rg_wa, od_rg_ba, od_rg_wx, od_rg_bx, od_rg_lambda, od_ret_gain, od_ret_bias, od_w_out, xa_norm, xa_mem_norm, xa_wq, xa_wkv, xa_wo, final_norm):
    bsz, seq, d = x.shape
    tokens = bsz * seq
    lb_all = jnp.cumsum(jax.nn.softmax(hgrn_lb_logits.astype(F32), axis=0), axis=0)
    log_g = jnp.log(1.0 - 2.0 ** (-5.0 - jnp.arange(N_HEADS, dtype=F32)))
    log_g = jnp.broadcast_to(log_g[:, None, None], (N_HEADS, 1, HEAD_DIM))
    mem2 = mem.reshape(bsz * MEM_LEN, d)

    def memory_kv(layer):
        kv = norm_matmul(mem2, xa_mem_norm[layer], xa_wkv[layer], BF16,
                         tm=bsz * MEM_LEN // 2, tn=2 * XA_WIDTH, name="mem_kv")
        return kv.reshape(bsz, MEM_LEN, 2 * XA_WIDTH)

    h, *tables = project_with_rope_tables(x.reshape(tokens, d), ev_norm[0], ev_w_in.reshape(ev_w_in.shape[1:]),
                                          positions, tm=seq, tn=PROJ_TN)
    ret_cos, ret_sin, moba_cos, moba_lo, moba_hi = (tb.reshape(bsz, seq, HEAD_DIM) for tb in tables)
    h3 = h.reshape(bsz, seq, -1)
    nb = WIDTH // HEAD_DIM
    ya = hgrn2_mixer(h3, lb_all[0], ev_hgrn_gain[0], 0, nb, 2 * nb)
    yb = moba_mixer(h3, moba_cos, moba_lo, moba_hi, 3 * nb, 4 * nb, 5 * nb)
    x1 = post_mixer(x, ya, yb, h3, 3, ev_w_out[0], xa_norm[0], xa_wq[0], memory_kv(0), xa_wo[0],
                    final_norm, final=False)

    h = norm_matmul(x1.reshape(tokens, d), od_norm[0], od_w_in.reshape(od_w_in.shape[1:]), BF16,
                    tm=seq, tn=PROJ_TN, name="in_proj_odd")
    h3 = h.reshape(bsz, seq, -1)
    yc = rglru_mixer(h3, od_conv_w[0], od_conv_b[0], od_rg_wa[0], od_rg_ba[0], od_rg_wx[0],
                     od_rg_bx[0], od_rg_lambda[0], 0)
    yd = retention_mixer(h3, ret_cos, ret_sin, log_g, od_ret_gain[0], od_ret_bias[0],
                         nb, 2 * nb, 3 * nb)
    return post_mixer(x1, yc, yd, h3, 2, od_w_out[0], xa_norm[1], xa_wq[1], memory_kv(1), xa_wo[1],
                      final_norm, final=True)
```

```python
import functools
import math

import jax
import jax.numpy as jnp
from jax import lax
from jax.experimental import pallas as pl
from jax.experimental.pallas import tpu as pltpu

F32 = jnp.float32
BF16 = jnp.bfloat16

D_MODEL = 1024
HEAD_DIM = 128
N_HEADS = 8
WIDTH = N_HEADS * HEAD_DIM
MEM_LEN = 256
HGRN_CHUNK = 32
HGRN_TILE = 128
HGRN_PARTS = 16
MOBA_BLOCK = 256
MOBA_TOPK = 3
RET_CHUNK = 256
RET_PARTS = 2
CONV_WIDTH = 4
RG_C = 8.0
ROPE_THETA = 500000.0
ROPE_DIM = HEAD_DIM // 4
RET_THETA = 10000.0
PROJ_PIECES = 4
POST_PARTS = 4
XA_HEADS = 4
XA_WIDTH = XA_HEADS * HEAD_DIM
EPS = 1e-6
V7X_VMEM_BYTES = 64 * 1024 * 1024
VMEM_LIMIT = V7X_VMEM_BYTES * 7 // 8
PROJ_TN = 1024
POST_TM = 1024
RG_BLOCKS = 4
MASKED = -1e30

_NT = (((1,), (1,)), ((), ()))


def _dot(a, b):
    return jnp.dot(a, b, preferred_element_type=F32)


def _dot_nt(a, b):
    return lax.dot_general(a, b, _NT, preferred_element_type=F32)


LOG2E = math.log2(math.e)


def _sigmoid(x):
    return 1.0 / (1.0 + jnp.exp2(x * (-LOG2E)))


def _silu(x):
    return x * _sigmoid(x)


def _run_staggered(parts, n_stages):
    for step in range(n_stages + len(parts) - 1):
        for p, gen in enumerate(parts):
            if 0 <= step - p < n_stages:
                next(gen)


def _params(*sem):
    return pltpu.CompilerParams(dimension_semantics=sem, vmem_limit_bytes=VMEM_LIMIT)


def _norm_matmul_pieces(x_ref, g_ref, w_ref, o_ref, xn_ref, between=None):
    rows = x_ref.shape[0] // PROJ_PIECES

    def run(with_norm):
        w = w_ref[...].astype(BF16)
        for c in range(PROJ_PIECES):
            piece = slice(c * rows, (c + 1) * rows)
            if with_norm:
                x = x_ref[piece, :]
                ms = jnp.mean(x * x, axis=-1, keepdims=True)
                xn_ref[piece, :] = (x * lax.rsqrt(ms + EPS) * g_ref[...]).astype(BF16)
            o_ref[piece, :] = _dot(xn_ref[piece, :], w).astype(o_ref.dtype)
            if between is not None:
                between(c)

    @pl.when(pl.program_id(1) == 0)
    def _():
        run(True)

    @pl.when(pl.program_id(1) != 0)
    def _():
        run(False)


def _norm_matmul_kernel(x_ref, g_ref, w_ref, o_ref, xn_ref):
    _norm_matmul_pieces(x_ref, g_ref, w_ref, o_ref, xn_ref)


def norm_matmul(x, gain, w, out_dtype, tm, tn, name):
    t, d = x.shape
    n = w.shape[1]
    return pl.pallas_call(
        _norm_matmul_kernel,
        out_shape=jax.ShapeDtypeStruct((t, n), out_dtype),
        grid=(t // tm, n // tn),
        in_specs=[
            pl.BlockSpec((tm, d), lambda i, j: (i, 0)),
            pl.BlockSpec((1, d), lambda i, j: (0, 0)),
            pl.BlockSpec((d, tn), lambda i, j: (0, j)),
        ],
        out_specs=pl.BlockSpec((tm, tn), lambda i, j: (i, j)),
        scratch_shapes=[pltpu.VMEM((tm, d), BF16)],
        compiler_params=_params("arbitrary", "arbitrary"),
        name=name,
    )(x, gain.reshape(1, d), w)


def _split2(x):
    hi = x.astype(BF16)
    lo = (x - hi.astype(F32)).astype(BF16)
    return hi, lo


def _hgrn2_kernel(q_ref, f_ref, i_ref, lb_ref, gain_ref, o_ref, st_ref):
    seq = q_ref.shape[0]
    tile, chunk = HGRN_TILE, HGRN_CHUNK
    per_tile = tile // chunk
    part_rows = seq // HGRN_PARTS
    nt = part_rows // tile
    lb = lb_ref[...]
    gain = gain_ref[...]

    def tiles(x):
        return x.reshape(nt, tile, HEAD_DIM)

    def bmm(spec, a, b):
        return jnp.einsum(spec, a, b, preferred_element_type=F32)

    ti = lax.broadcasted_iota(jnp.int32, (tile, tile), 0)
    tj = lax.broadcasted_iota(jnp.int32, (tile, tile), 1)
    intra_mask = (((ti // chunk) == (tj // chunk)) & (tj <= ti))[None]
    sum_mat = jnp.broadcast_to(jnp.where(intra_mask, 1.0, 0.0).astype(BF16), (nt, tile, tile))
    row_chunk = (lax.broadcasted_iota(jnp.int32, (tile, HEAD_DIM), 0) // chunk)[None]

    def per_chunk_lanes(x):
        zero = jnp.zeros_like(x)
        return jnp.concatenate([jnp.where(row_chunk == ci, x, zero) for ci in range(per_tile)], axis=-1)

    carried = [jnp.zeros((HEAD_DIM, HEAD_DIM), F32)]

    def part(p):
        rows = slice(p * part_rows, (p + 1) * part_rows)
        f = lb + (1.0 - lb) * _sigmoid(f_ref[rows, :].astype(F32))
        k = tiles(1.0 - f)
        hi, lo = _split2(jnp.log2(f))
        yield
        b = bmm('trc,tcd->trd', sum_mat, tiles(hi)) + bmm('trc,tcd->trd', sum_mat, tiles(lo))
        b_chunks = b.reshape(part_rows // chunk, chunk, HEAD_DIM)
        b_tot = jnp.broadcast_to(b_chunks[:, chunk - 1:, :], b_chunks.shape).reshape(b.shape)
        yield
        q_dec = (tiles(q_ref[rows, :].astype(F32)) * jnp.exp2(b)).astype(BF16)
        k_inv = (k * jnp.exp2(-b)).astype(BF16)
        k_end = (k * jnp.exp2(b_tot - b)).astype(BF16)
        v = tiles(i_ref[rows, :].astype(F32))
        yield
        v_t = jnp.swapaxes(v, 1, 2).astype(BF16)
        incr = bmm('tvn,tnk->tvk', v_t, per_chunk_lanes(k_end))
        yield
        att = bmm('tid,tjd->tij', q_dec, k_inv)
        att = jnp.where(intra_mask, att, 0.0).astype(BF16)
        yield
        o_intra = bmm('tij,tjd->tid', att, v.astype(BF16))
        state_t = carried[0]
        for t in range(nt):
            for ci in range(per_tile):
                lanes = slice(ci * HEAD_DIM, (ci + 1) * HEAD_DIM)
                st_ref[p * nt + t, :, lanes] = state_t.astype(BF16)
                decay = jnp.exp2(b_tot[t, ci * chunk:ci * chunk + 1, :])
                state_t = state_t * decay + incr[t, :, lanes]
        carried[0] = state_t
        yield
        o = o_intra + bmm('tnk,tvk->tnv', per_chunk_lanes(q_dec), st_ref[p * nt:(p + 1) * nt])
        o = o.reshape(part_rows, HEAD_DIM)
        y = o * lax.rsqrt(jnp.mean(o * o, axis=-1, keepdims=True) + EPS) * gain
        o_ref[rows, :] = y.astype(o_ref.dtype)
        yield

    _run_staggered([part(p) for p in range(HGRN_PARTS)], n_stages=7)


def hgrn2_mixer(h3, lb, gain, col_q, col_f, col_i):
    bsz, seq, _ = h3.shape

    def col(off):
        return pl.BlockSpec((None, seq, HEAD_DIM), lambda b, h: (b, 0, off + h))

    vec = pl.BlockSpec((None, 1, HEAD_DIM), lambda b, h: (h, 0, 0))
    return pl.pallas_call(
        _hgrn2_kernel,
        out_shape=jax.ShapeDtypeStruct((bsz, seq, WIDTH), BF16),
        grid=(bsz, N_HEADS),
        in_specs=[col(col_q), col(col_f), col(col_i), vec, vec],
        out_specs=pl.BlockSpec((None, seq, HEAD_DIM), lambda b, h: (b, 0, h)),
        scratch_shapes=[pltpu.VMEM((seq // HGRN_TILE, HEAD_DIM, HGRN_TILE // HGRN_CHUNK * HEAD_DIM), BF16)],
        compiler_params=_params("arbitrary", "arbitrary"),
        name="hgrn2",
    )(h3, h3, h3, lb.reshape(N_HEADS, 1, HEAD_DIM), gain.reshape(N_HEADS, 1, HEAD_DIM))


RET_HALF = HEAD_DIM // 2
MOBA_HALF = ROPE_DIM // 2


def _rope_table_kernel(pos_ref, inv_ref, ret_cos_ref, ret_sin_ref, moba_cos_ref, moba_lo_ref, moba_hi_ref):
    ang = pos_ref[...] * inv_ref[...]
    cos = jnp.cos(ang)
    sin = jnp.sin(ang)
    lane = lax.broadcasted_iota(jnp.int32, ang.shape, 1)
    low = lane < RET_HALF
    cos_up = pltpu.roll(cos, RET_HALF, 1)
    sin_up = pltpu.roll(sin, RET_HALF, 1)
    ret_cos_ref[...] = jnp.where(low, cos, cos_up)
    ret_sin_ref[...] = jnp.where(low, -sin, sin_up)
    first = lane < MOBA_HALF
    second = (lane >= MOBA_HALF) & (lane < 2 * MOBA_HALF)
    moba_cos_ref[...] = jnp.where(first, cos_up, jnp.where(second, pltpu.roll(cos_up, MOBA_HALF, 1), 1.0))
    moba_lo_ref[...] = jnp.where(first, -sin_up, 0.0)
    moba_hi_ref[...] = jnp.where(second, pltpu.roll(sin_up, MOBA_HALF, 1), 0.0)


def _proj_tables_kernel(x_ref, g_ref, w_ref, pos_ref, inv_ref, o_ref, *table_and_scratch):
    table_refs, xn_ref = table_and_scratch[:-1], table_and_scratch[-1]
    trows = pos_ref.shape[0] // PROJ_PIECES

    def table_piece(c):
        piece = slice(c * trows, (c + 1) * trows)
        _rope_table_kernel(pos_ref.at[piece, :], inv_ref, *(r.at[piece, :] for r in table_refs))

    _norm_matmul_pieces(x_ref, g_ref, w_ref, o_ref, xn_ref, between=table_piece)


def project_with_rope_tables(x, gain, w, positions, tm, tn):
    t, d = x.shape
    n = w.shape[1]
    n_steps = n // tn
    rows = tm // n_steps
    inv_ret = RET_THETA ** (-jnp.arange(RET_HALF, dtype=F32) / RET_HALF)
    inv_moba = ROPE_THETA ** (-jnp.arange(MOBA_HALF, dtype=F32) / MOBA_HALF)
    inv = jnp.concatenate([inv_ret, inv_moba, jnp.zeros((HEAD_DIM - RET_HALF - MOBA_HALF,), F32)])
    posf = jnp.broadcast_to(positions.astype(F32).reshape(t, 1), (t, HEAD_DIM))
    table_spec = pl.BlockSpec((rows, HEAD_DIM), lambda i, j: (i * n_steps + j, 0))
    return pl.pallas_call(
        _proj_tables_kernel,
        out_shape=[jax.ShapeDtypeStruct((t, n), BF16)] + [jax.ShapeDtypeStruct((t, HEAD_DIM), F32)] * 5,
        grid=(t // tm, n_steps),
        in_specs=[
            pl.BlockSpec((tm, d), lambda i, j: (i, 0)),
            pl.BlockSpec((1, d), lambda i, j: (0, 0)),
            pl.BlockSpec((d, tn), lambda i, j: (0, j)),
            table_spec,
            pl.BlockSpec((1, HEAD_DIM), lambda i, j: (0, 0)),
        ],
        out_specs=[pl.BlockSpec((tm, tn), lambda i, j: (i, j))] + [table_spec] * 5,
        scratch_shapes=[pltpu.VMEM((tm, d), BF16)],
        compiler_params=_params("arbitrary", "arbitrary"),
        name="in_proj_even",
    )(x, gain.reshape(1, d), w, posf, inv.reshape(1, HEAD_DIM))


def _moba_kernel(q_ref, k_ref, v_ref, cos_ref, sin_lo_ref, sin_hi_ref, o_ref,
                 qr_ref, ka_ref, va_ref, kmean_ref):
    seq = q_ref.shape[0]
    blk = MOBA_BLOCK
    nb = seq // blk
    exp2_scale = HEAD_DIM ** -0.5 * LOG2E

    row = lax.broadcasted_iota(jnp.int32, (blk, blk), 0)
    colm = lax.broadcasted_iota(jnp.int32, (blk, blk), 1)
    causal = colm <= row
    eye = jnp.where(colm == row, 1.0, 0.0).astype(BF16)
    blk_row = lax.broadcasted_iota(jnp.int32, (nb, blk), 0)
    lane = lax.broadcasted_iota(jnp.int32, (blk, HEAD_DIM), 1)
    kmean_ref[...] = jnp.zeros(kmean_ref.shape, kmean_ref.dtype)

    def prepare(qb):
        rows = slice(qb * blk, (qb + 1) * blk)

        def rope(x):
            return (x * cos_ref[rows, :]
                    + pltpu.roll(x, HEAD_DIM - MOBA_HALF, 1) * sin_lo_ref[rows, :]
                    + pltpu.roll(x, MOBA_HALF, 1) * sin_hi_ref[rows, :])

        kr = rope(k_ref[rows, :].astype(F32))
        kmean_ref[qb:qb + 1, :] = jnp.mean(kr, axis=0, keepdims=True)
        ka_ref[rows, :HEAD_DIM] = kr.astype(BF16)
        ka_ref[rows, HEAD_DIM:] = jnp.where(lane == qb, MASKED, 0.0).astype(BF16)
        qr_ref[rows, :] = (rope(q_ref[rows, :].astype(F32)) * exp2_scale).astype(BF16)
        va_ref[rows, :HEAD_DIM] = v_ref[rows, :]
        va_ref[rows, HEAD_DIM:] = jnp.ones((blk, HEAD_DIM), BF16)

    def scores(qb):
        rows = slice(qb * blk, (qb + 1) * blk)
        q = qr_ref[rows, :]
        n_keys = (qb + 1) * blk
        if qb > MOBA_TOPK:
            gate = _dot_nt(kmean_ref[...].astype(BF16), q)
            rank = jnp.zeros((nb, blk), F32)
            for j in range(qb):
                gj = gate[j:j + 1, :]
                ahead = (gj > gate) | ((gj == gate) & (blk_row > j))
                rank = rank + jnp.where(ahead, 1.0, 0.0)
            drop = jnp.where((rank >= float(MOBA_TOPK)) & (blk_row < qb), 1.0, 0.0)
            drop = jnp.concatenate([drop, jnp.zeros((HEAD_DIM - nb, blk), F32)], axis=0).astype(BF16)
            drop_col = _dot_nt(eye, drop).astype(BF16)
            s = _dot_nt(jnp.concatenate([q, drop_col], axis=1), ka_ref[:n_keys, :])
        else:
            s = _dot_nt(q, ka_ref[:n_keys, :HEAD_DIM])
        own = jnp.where(causal, s[:, qb * blk:], MASKED)
        return jnp.concatenate([s[:, :qb * blk], own], axis=1) if qb else own

    prepare(0)
    s_next = scores(0)
    for qb in range(nb):
        rows = slice(qb * blk, (qb + 1) * blk)
        s = s_next
        if qb + 1 < nb:
            prepare(qb + 1)
            s_next = scores(qb + 1)
        m = jnp.max(s, axis=-1, keepdims=True)
        p = jnp.exp2(s - m).astype(BF16)
        o = _dot(p, va_ref[:(qb + 1) * blk, :])
        o_ref[rows, :] = (o[:, :HEAD_DIM] / o[:, HEAD_DIM:]).astype(o_ref.dtype)


def moba_mixer(h3, cos, sin_lo, sin_hi, col_q, col_k, col_v):
    bsz, seq, _ = h3.shape

    def col(off):
        return pl.BlockSpec((None, seq, HEAD_DIM), lambda b, h: (b, 0, off + h))

    table = pl.BlockSpec((None, seq, HEAD_DIM), lambda b, h: (b, 0, 0))
    return pl.pallas_call(
        _moba_kernel,
        out_shape=jax.ShapeDtypeStruct((bsz, seq, WIDTH), BF16),
        grid=(bsz, N_HEADS),
        in_specs=[col(col_q), col(col_k), col(col_v), table, table, table],
        out_specs=pl.BlockSpec((None, seq, HEAD_DIM), lambda b, h: (b, 0, h)),
        scratch_shapes=[pltpu.VMEM((seq, HEAD_DIM), BF16), pltpu.VMEM((seq, 2 * HEAD_DIM), BF16),
                        pltpu.VMEM((seq, 2 * HEAD_DIM), BF16), pltpu.VMEM((seq // MOBA_BLOCK, HEAD_DIM), F32)],
        compiler_params=_params("arbitrary", "arbitrary"),
        name="moba",
    )(h3, h3, h3, cos, sin_lo, sin_hi)


def _rglru_kernel(x_ref, cw_ref, cb_ref, wa_ref, ba_ref, wx_ref, bx_ref, lam_ref,
                  o_ref, a_ref, u_ref, xpad_ref):
    seq, width = x_ref.shape
    sub = 8
    groups = seq // sub

    xpad_ref[:sub, :] = jnp.zeros((sub, width), F32)
    xpad_ref[sub:, :] = x_ref[...].astype(F32)
    xf = cb_ref[...]
    for d in range(CONV_WIDTH):
        xf = xf + xpad_ref[sub - d:sub - d + seq, :] * cw_ref[CONV_WIDTH - 1 - d:CONV_WIDTH - d, :]

    xb = xf.astype(BF16)

    def gate(w_ref, b_ref):
        pre = [_dot(xb[:, j * HEAD_DIM:(j + 1) * HEAD_DIM], w_ref[j]) for j in range(width // HEAD_DIM)]
        return _sigmoid(jnp.concatenate(pre, axis=1) + b_ref[...])

    r = gate(wa_ref, ba_ref)
    ig = gate(wx_ref, bx_ref)
    nl = -lam_ref[...]
    softplus = jnp.maximum(nl, 0.0) + jnp.log1p(jnp.exp(-jnp.abs(nl)))
    a = jnp.exp2(r * ((-RG_C * LOG2E) * softplus))
    gap = 1.0 - a * a
    u = jnp.where(gap > 0.0, gap * lax.rsqrt(gap), 0.0) * ig * xf

    a3 = a.reshape(groups, sub, width)
    u3 = u.reshape(groups, sub, width)
    s_idx = lax.broadcasted_iota(jnp.int32, a3.shape, 1)
    for d in (1, 2, 4):
        keep = s_idx >= d
        a_prev = jnp.where(keep, pltpu.roll(a3, d, 1), 1.0)
        u_prev = jnp.where(keep, pltpu.roll(u3, d, 1), 0.0)
        u3 = u3 + a3 * u_prev
        a3 = a3 * a_prev
    a_ref[...] = a3
    u_ref[...] = u3

    def group_body(g, carry):
        hg = a_ref[g] * carry + u_ref[g]
        u_ref[g] = hg
        return jnp.broadcast_to(hg[sub - 1:sub, :], (sub, width))

    lax.fori_loop(0, groups, group_body, jnp.zeros((sub, width), F32), unroll=8)
    o_ref[...] = u_ref[...].reshape(seq, width).astype(o_ref.dtype)


def rglru_mixer(h3, conv_w, conv_b, wa, ba, wx, bx, lam, col_x, blocks_per_step=RG_BLOCKS):
    bsz, seq, _ = h3.shape
    width = blocks_per_step * HEAD_DIM

    def col(off):
        return pl.BlockSpec((None, seq, width), lambda b, c: (b, 0, off // blocks_per_step + c))

    def vec(rows):
        return pl.BlockSpec((rows, width), lambda b, c: (0, c))

    wspec = pl.BlockSpec((blocks_per_step, HEAD_DIM, HEAD_DIM), lambda b, c: (c, 0, 0))
    return pl.pallas_call(
        _rglru_kernel,
        out_shape=jax.ShapeDtypeStruct((bsz, seq, WIDTH), BF16),
        grid=(bsz, WIDTH // width),
        in_specs=[col(col_x), vec(CONV_WIDTH), vec(1), wspec, vec(1), wspec, vec(1), vec(1)],
        out_specs=pl.BlockSpec((None, seq, width), lambda b, c: (b, 0, c)),
        scratch_shapes=[pltpu.VMEM((seq // 8, 8, width), F32)] * 2 + [pltpu.VMEM((seq + 8, width), F32)],
        compiler_params=_params("arbitrary", "arbitrary"),
        name="rglru",
    )(h3, conv_w, conv_b.reshape(1, WIDTH), wa.astype(BF16), ba.reshape(1, WIDTH),
      wx.astype(BF16), bx.reshape(1, WIDTH), lam.reshape(1, WIDTH))


def _retention_kernel(q_ref, k_ref, v_ref, cos_ref, sin_ref, logg_ref, gain_ref, bias_ref,
                      o_ref, st_ref):
    seq = q_ref.shape[0]
    ck = RET_CHUNK
    half = RET_HALF
    nc = seq // ck // RET_PARTS

    def chunks(x):
        return x.reshape(nc, ck, HEAD_DIM)

    def bmm(spec, a, b):
        return jnp.einsum(spec, a, b, preferred_element_type=F32)

    log_g = logg_ref[...]
    ri = lax.broadcasted_iota(jnp.int32, (ck, ck), 0)
    ci = lax.broadcasted_iota(jnp.int32, (ck, ck), 1)
    diff = (ri - ci).astype(F32)
    dmask = jnp.where(ri >= ci, jnp.exp(jnp.maximum(diff, 0.0) * log_g[:, :1]), 0.0)
    idx = lax.broadcasted_iota(jnp.int32, (ck, HEAD_DIM), 0).astype(F32)
    q_fac = jnp.exp((idx + 1.0) * log_g)
    k_fac = jnp.exp((ck - 1.0 - idx) * log_g)
    chunk_decay = jnp.exp(float(ck) * log_g)

    carried = [jnp.zeros((HEAD_DIM, HEAD_DIM), F32)]

    def part(p):
        rows = slice(p * nc * ck, (p + 1) * nc * ck)
        cos = cos_ref[rows, :]
        sin = sin_ref[rows, :]
        q = q_ref[rows, :].astype(F32)
        k = k_ref[rows, :].astype(F32)
        q = chunks(q * cos + pltpu.roll(q, half, 1) * sin)
        k = chunks((k * cos + pltpu.roll(k, half, 1) * sin) * (HEAD_DIM ** -0.5))
        v = chunks(v_ref[rows, :])
        yield
        att = bmm('cid,cjd->cij', q.astype(BF16), k.astype(BF16)) * dmask[None]
        k_end_t = jnp.swapaxes(k * k_fac[None], 1, 2).astype(BF16)
        incr = bmm('ckn,cnv->ckv', k_end_t, v)
        yield
        o_intra = bmm('cij,cjd->cid', att.astype(BF16), v)
        state = carried[0]
        for c in range(nc):
            st_ref[p * nc + c] = state.astype(BF16)
            state = state * chunk_decay + incr[c]
        carried[0] = state
        yield
        o = o_intra + bmm('cnk,ckv->cnv', (q * q_fac[None]).astype(BF16), st_ref[p * nc:(p + 1) * nc])
        o = o.reshape(nc * ck, HEAD_DIM)
        mu = jnp.mean(o, axis=-1, keepdims=True)
        oc = o - mu
        var = jnp.mean(oc * oc, axis=-1, keepdims=True)
        o_ref[rows, :] = (oc * lax.rsqrt(var + EPS) * gain_ref[...] + bias_ref[...]).astype(o_ref.dtype)
        yield

    _run_staggered([part(p) for p in range(RET_PARTS)], n_stages=4)


def retention_mixer(h3, cos, sin, log_g, gain, bias, col_q, col_k, col_v):
    bsz, seq, _ = h3.shape

    def col(off):
        return pl.BlockSpec((None, seq, HEAD_DIM), lambda b, h: (b, 0, off + h))

    table = pl.BlockSpec((None, seq, HEAD_DIM), lambda b, h: (b, 0, 0))
    vec = pl.BlockSpec((None, 1, HEAD_DIM), lambda b, h: (h, 0, 0))
    return pl.pallas_call(
        _retention_kernel,
        out_shape=jax.ShapeDtypeStruct((bsz, seq, WIDTH), BF16),
        grid=(bsz, N_HEADS),
        in_specs=[col(col_q), col(col_k), col(col_v), table, table, vec, vec, vec],
        out_specs=pl.BlockSpec((None, seq, HEAD_DIM), lambda b, h: (b, 0, h)),
        scratch_shapes=[pltpu.VMEM((seq // RET_CHUNK, HEAD_DIM, HEAD_DIM), BF16)],
        compiler_params=_params("arbitrary", "arbitrary"),
        name="retention",
    )(h3, h3, h3, cos, sin, log_g,
      gain.reshape(N_HEADS, 1, HEAD_DIM), bias.reshape(N_HEADS, 1, HEAD_DIM))


def _post_kernel(x_ref, ya_ref, yb_ref, z_ref, woa_ref, wob_ref, g_ref, wq_ref, kv_ref, wo_ref, fg_ref,
                 o_ref, *, final):
    part_rows = x_ref.shape[0] // POST_PARTS
    scale = HEAD_DIM ** -0.5

    def gated(y_ref, rows, cols):
        return (y_ref[rows, :].astype(F32) * _silu(z_ref[rows, cols].astype(F32))).astype(BF16)

    def part(p):
        rows = slice(p * part_rows, (p + 1) * part_rows)
        ga = gated(ya_ref, rows, slice(0, WIDTH))
        yield
        gb = gated(yb_ref, rows, slice(WIDTH, 2 * WIDTH))
        x1 = x_ref[rows, :] + _dot(ga, woa_ref[...])
        yield
        x1 = x1 + _dot(gb, wob_ref[...])
        yield
        ms = jnp.mean(x1 * x1, axis=-1, keepdims=True)
        xn = (x1 * lax.rsqrt(ms + EPS) * g_ref[...]).astype(BF16)
        q = _dot(xn, wq_ref[...]).astype(BF16)
        yield
        head_cols = [slice(hd * HEAD_DIM, (hd + 1) * HEAD_DIM) for hd in range(XA_HEADS)]
        scores = [_dot_nt(q[:, cs], kv_ref[:, cs]) * scale for cs in head_cols]
        yield
        probs = []
        for s in scores:
            pr = jnp.exp(s - jnp.max(s, axis=-1, keepdims=True))
            probs.append((pr / jnp.sum(pr, axis=-1, keepdims=True)).astype(BF16))
        yield
        o = jnp.concatenate([_dot(pr, kv_ref[:, XA_WIDTH + cs.start:XA_WIDTH + cs.stop]).astype(BF16)
                             for pr, cs in zip(probs, head_cols)], axis=1)
        yield
        x2 = x1 + _dot(o, wo_ref[...])
        if final:
            ms2 = jnp.mean(x2 * x2, axis=-1, keepdims=True)
            x2 = x2 * lax.rsqrt(ms2 + EPS) * fg_ref[...]
        o_ref[rows, :] = x2
        yield

    _run_staggered([part(p) for p in range(POST_PARTS)], n_stages=8)


def post_mixer(x3, ya, yb, h3, z_block, w_out, xa_gain, wq, kv, wo, final_gain, final, tm=POST_TM):
    bsz, seq, d = x3.shape
    full = lambda shape: pl.BlockSpec(shape, lambda b, i: (0,) * len(shape))
    tile = lambda w: pl.BlockSpec((None, tm, w), lambda b, i: (b, i, 0))
    return pl.pallas_call(
        functools.partial(_post_kernel, final=final),
        out_shape=jax.ShapeDtypeStruct((bsz, seq, d), F32),
        grid=(bsz, seq // tm),
        in_specs=[tile(d), tile(WIDTH), tile(WIDTH),
                  pl.BlockSpec((None, tm, 2 * WIDTH), lambda b, i: (b, i, z_block)),
                  full((WIDTH, d)), full((WIDTH, d)), full((1, d)), full((d, XA_WIDTH)),
                  pl.BlockSpec((None, MEM_LEN, 2 * XA_WIDTH), lambda b, i: (b, 0, 0)),
                  full((XA_WIDTH, d)), full((1, d))],
        out_specs=tile(d),
        compiler_params=_params("arbitrary", "arbitrary"),
        name="post_final" if final else "post",
    )(x3, ya, yb, h3, w_out[:WIDTH].astype(BF16), w_out[WIDTH:].astype(BF16), xa_gain.reshape(1, d),
      wq.astype(BF16), kv, wo.astype(BF16), final_gain.reshape(1, d))


def kernel(x, mem, positions, hgrn_lb_logits, ev_norm, ev_w_in, ev_hgrn_gain, ev_w_out, od_norm, od_w_in, od_conv_w, od_conv_b, od_rg_wa, od_rg_ba, od_rg_wx, od_rg_bx, od_rg_lambda, od_ret_gain, od_ret_bias, od_w_out, xa_norm, xa_mem_norm, xa_wq, xa_wkv, xa_wo, final_norm):
    bsz, seq, d = x.shape
    tokens = bsz * seq
    lb_all = jnp.cumsum(jax.nn.softmax(hgrn_lb_logits.astype(F32), axis=0), axis=0)
    log_g = jnp.log(1.0 - 2.0 ** (-5.0 - jnp.arange(N_HEADS, dtype=F32)))
    log_g = jnp.broadcast_to(log_g[:, None, None], (N_HEADS, 1, HEAD_DIM))
    mem2 = mem.reshape(bsz * MEM_LEN, d)

    def memory_kv(layer):
        kv = norm_matmul(mem2, xa_mem_norm[layer], xa_wkv[layer], BF16,
                         tm=bsz * MEM_LEN // 2, tn=2 * XA_WIDTH, name="mem_kv")
        return kv.reshape(bsz, MEM_LEN, 2 * XA_WIDTH)

    h, *tables = project_with_rope_tables(x.reshape(tokens, d), ev_norm[0], ev_w_in.reshape(ev_w_in.shape[1:]),
                                          positions, tm=seq, tn=PROJ_TN)
    ret_cos, ret_sin, moba_cos, moba_lo, moba_hi = (tb.reshape(bsz, seq, HEAD_DIM) for tb in tables)
    h3 = h.reshape(bsz, seq, -1)
    nb = WIDTH // HEAD_DIM
    ya = hgrn2_mixer(h3, lb_all[0], ev_hgrn_gain[0], 0, nb, 2 * nb)
    yb = moba_mixer(h3, moba_cos, moba_lo, moba_hi, 3 * nb, 4 * nb, 5 * nb)
    x1 = post_mixer(x, ya, yb, h3, 3, ev_w_out[0], xa_norm[0], xa_wq[0], memory_kv(0), xa_wo[0],
                    final_norm, final=False)

    h = norm_matmul(x1.reshape(tokens, d), od_norm[0], od_w_in.reshape(od_w_in.shape[1:]), BF16,
                    tm=seq, tn=PROJ_TN, name="in_proj_odd")
    h3 = h.reshape(bsz, seq, -1)
    yc = rglru_mixer(h3, od_conv_w[0], od_conv_b[0], od_rg_wa[0], od_rg_ba[0], od_rg_wx[0],
                     od_rg_bx[0], od_rg_lambda[0], 0)
    yd = retention_mixer(h3, ret_cos, ret_sin, log_g, od_ret_gain[0], od_ret_bias[0],
                         nb, 2 * nb, 3 * nb)
    return post_mixer(x1, yc, yd, h3, 2, od_w_out[0], xa_norm[1], xa_wq[1], memory_kv(1), xa_wo[1],
                      final_norm, final=True)
```

```python
import functools
import math

import jax
import jax.numpy as jnp
from jax import lax
from jax.experimental import pallas as pl
from jax.experimental.pallas import tpu as pltpu

F32 = jnp.float32
BF16 = jnp.bfloat16

D_MODEL = 1024
HEAD_DIM = 128
N_HEADS = 8
WIDTH = N_HEADS * HEAD_DIM
MEM_LEN = 256
HGRN_CHUNK = 32
HGRN_TILE = 128
HGRN_PARTS = 16
MOBA_BLOCK = 256
MOBA_TOPK = 3
RET_CHUNK = 256
RET_PARTS = 2
CONV_WIDTH = 4
RG_C = 8.0
ROPE_THETA = 500000.0
ROPE_DIM = HEAD_DIM // 4
RET_THETA = 10000.0
PROJ_PIECES = 4
POST_PARTS = 4
XA_HEADS = 4
XA_WIDTH = XA_HEADS * HEAD_DIM
EPS = 1e-6
V7X_VMEM_BYTES = 64 * 1024 * 1024
VMEM_LIMIT = V7X_VMEM_BYTES * 7 // 8
PROJ_TN = 1024
POST_TM = 1024
RG_BLOCKS = 4
MASKED = -1e30

_NT = (((1,), (1,)), ((), ()))


def _dot(a, b):
    return jnp.dot(a, b, preferred_element_type=F32)


def _dot_nt(a, b):
    return lax.dot_general(a, b, _NT, preferred_element_type=F32)


LOG2E = math.log2(math.e)


def _sigmoid(x):
    return 1.0 / (1.0 + jnp.exp2(x * (-LOG2E)))


def _silu(x):
    return x * _sigmoid(x)


def _run_staggered(parts, n_stages):
    for step in range(n_stages + len(parts) - 1):
        for p, gen in reversed(list(enumerate(parts))):
            if 0 <= step - p < n_stages:
                next(gen)


def _params(*sem):
    return pltpu.CompilerParams(dimension_semantics=sem, vmem_limit_bytes=VMEM_LIMIT)


def _norm_matmul_pieces(x_ref, g_ref, w_ref, o_ref, xn_ref, between=None):
    rows = x_ref.shape[0] // PROJ_PIECES

    def run(with_norm):
        w = w_ref[...].astype(BF16)
        for c in range(PROJ_PIECES):
            piece = slice(c * rows, (c + 1) * rows)
            if with_norm:
                x = x_ref[piece, :]
                ms = jnp.mean(x * x, axis=-1, keepdims=True)
                xn_ref[piece, :] = (x * lax.rsqrt(ms + EPS) * g_ref[...]).astype(BF16)
            o_ref[piece, :] = _dot(xn_ref[piece, :], w).astype(o_ref.dtype)
            if between is not None:
                between(c)

    @pl.when(pl.program_id(1) == 0)
    def _():
        run(True)

    @pl.when(pl.program_id(1) != 0)
    def _():
        run(False)


def _norm_matmul_kernel(x_ref, g_ref, w_ref, o_ref, xn_ref):
    _norm_matmul_pieces(x_ref, g_ref, w_ref, o_ref, xn_ref)


def norm_matmul(x, gain, w, out_dtype, tm, tn, name):
    t, d = x.shape
    n = w.shape[1]
    return pl.pallas_call(
        _norm_matmul_kernel,
        out_shape=jax.ShapeDtypeStruct((t, n), out_dtype),
        grid=(t // tm, n // tn),
        in_specs=[
            pl.BlockSpec((tm, d), lambda i, j: (i, 0)),
            pl.BlockSpec((1, d), lambda i, j: (0, 0)),
            pl.BlockSpec((d, tn), lambda i, j: (0, j)),
        ],
        out_specs=pl.BlockSpec((tm, tn), lambda i, j: (i, j)),
        scratch_shapes=[pltpu.VMEM((tm, d), BF16)],
        compiler_params=_params("arbitrary", "arbitrary"),
        name=name,
    )(x, gain.reshape(1, d), w)


def _split2(x):
    hi = x.astype(BF16)
    lo = (x - hi.astype(F32)).astype(BF16)
    return hi, lo


def _hgrn2_kernel(q_ref, f_ref, i_ref, lb_ref, gain_ref, o_ref, st_ref):
    seq = q_ref.shape[0]
    tile, chunk = HGRN_TILE, HGRN_CHUNK
    per_tile = tile // chunk
    part_rows = seq // HGRN_PARTS
    nt = part_rows // tile
    lb = lb_ref[...]
    gain = gain_ref[...]

    def tiles(x):
        return x.reshape(nt, tile, HEAD_DIM)

    def bmm(spec, a, b):
        return jnp.einsum(spec, a, b, preferred_element_type=F32)

    ti = lax.broadcasted_iota(jnp.int32, (tile, tile), 0)
    tj = lax.broadcasted_iota(jnp.int32, (tile, tile), 1)
    intra_mask = (((ti // chunk) == (tj // chunk)) & (tj <= ti))[None]
    sum_mat = jnp.broadcast_to(jnp.where(intra_mask, 1.0, 0.0).astype(BF16), (nt, tile, tile))
    row_chunk = (lax.broadcasted_iota(jnp.int32, (tile, HEAD_DIM), 0) // chunk)[None]

    def per_chunk_lanes(x):
        zero = jnp.zeros_like(x)
        return jnp.concatenate([jnp.where(row_chunk == ci, x, zero) for ci in range(per_tile)], axis=-1)

    carried = [jnp.zeros((HEAD_DIM, HEAD_DIM), F32)]

    def part(p):
        rows = slice(p * part_rows, (p + 1) * part_rows)
        f = lb + (1.0 - lb) * _sigmoid(f_ref[rows, :].astype(F32))
        k = tiles(1.0 - f)
        hi, lo = _split2(jnp.log2(f))
        yield
        b = bmm('trc,tcd->trd', sum_mat, tiles(hi)) + bmm('trc,tcd->trd', sum_mat, tiles(lo))
        b_chunks = b.reshape(part_rows // chunk, chunk, HEAD_DIM)
        b_tot = jnp.broadcast_to(b_chunks[:, chunk - 1:, :], b_chunks.shape).reshape(b.shape)
        yield
        q_dec = (tiles(q_ref[rows, :].astype(F32)) * jnp.exp2(b)).astype(BF16)
        k_inv = (k * jnp.exp2(-b)).astype(BF16)
        k_end = (k * jnp.exp2(b_tot - b)).astype(BF16)
        v = tiles(i_ref[rows, :].astype(F32))
        yield
        v_t = jnp.swapaxes(v, 1, 2).astype(BF16)
        incr = bmm('tvn,tnk->tvk', v_t, per_chunk_lanes(k_end))
        yield
        att = bmm('tid,tjd->tij', q_dec, k_inv)
        att = jnp.where(intra_mask, att, 0.0).astype(BF16)
        yield
        o_intra = bmm('tij,tjd->tid', att, v.astype(BF16))
        state_t = carried[0]
        for t in range(nt):
            for ci in range(per_tile):
                lanes = slice(ci * HEAD_DIM, (ci + 1) * HEAD_DIM)
                st_ref[p * nt + t, :, lanes] = state_t.astype(BF16)
                decay = jnp.exp2(b_tot[t, ci * chunk:ci * chunk + 1, :])
                state_t = state_t * decay + incr[t, :, lanes]
        carried[0] = state_t
        yield
        o = o_intra + bmm('tnk,tvk->tnv', per_chunk_lanes(q_dec), st_ref[p * nt:(p + 1) * nt])
        o = o.reshape(part_rows, HEAD_DIM)
        y = o * lax.rsqrt(jnp.mean(o * o, axis=-1, keepdims=True) + EPS) * gain
        o_ref[rows, :] = y.astype(o_ref.dtype)
        yield

    _run_staggered([part(p) for p in range(HGRN_PARTS)], n_stages=7)


def hgrn2_mixer(h3, lb, gain, col_q, col_f, col_i):
    bsz, seq, _ = h3.shape

    def col(off):
        return pl.BlockSpec((None, seq, HEAD_DIM), lambda b, h: (b, 0, off + h))

    vec = pl.BlockSpec((None, 1, HEAD_DIM), lambda b, h: (h, 0, 0))
    return pl.pallas_call(
        _hgrn2_kernel,
        out_shape=jax.ShapeDtypeStruct((bsz, seq, WIDTH), BF16),
        grid=(bsz, N_HEADS),
        in_specs=[col(col_q), col(col_f), col(col_i), vec, vec],
        out_specs=pl.BlockSpec((None, seq, HEAD_DIM), lambda b, h: (b, 0, h)),
        scratch_shapes=[pltpu.VMEM((seq // HGRN_TILE, HEAD_DIM, HGRN_TILE // HGRN_CHUNK * HEAD_DIM), BF16)],
        compiler_params=_params("arbitrary", "arbitrary"),
        name="hgrn2",
    )(h3, h3, h3, lb.reshape(N_HEADS, 1, HEAD_DIM), gain.reshape(N_HEADS, 1, HEAD_DIM))


RET_HALF = HEAD_DIM // 2
MOBA_HALF = ROPE_DIM // 2


def _rope_table_kernel(pos_ref, inv_ref, ret_cos_ref, ret_sin_ref, moba_cos_ref, moba_lo_ref, moba_hi_ref):
    ang = pos_ref[...] * inv_ref[...]
    cos = jnp.cos(ang)
    sin = jnp.sin(ang)
    lane = lax.broadcasted_iota(jnp.int32, ang.shape, 1)
    low = lane < RET_HALF
    cos_up = pltpu.roll(cos, RET_HALF, 1)
    sin_up = pltpu.roll(sin, RET_HALF, 1)
    ret_cos_ref[...] = jnp.where(low, cos, cos_up)
    ret_sin_ref[...] = jnp.where(low, -sin, sin_up)
    first = lane < MOBA_HALF
    second = (lane >= MOBA_HALF) & (lane < 2 * MOBA_HALF)
    moba_cos_ref[...] = jnp.where(first, cos_up, jnp.where(second, pltpu.roll(cos_up, MOBA_HALF, 1), 1.0))
    moba_lo_ref[...] = jnp.where(first, -sin_up, 0.0)
    moba_hi_ref[...] = jnp.where(second, pltpu.roll(sin_up, MOBA_HALF, 1), 0.0)


def _proj_tables_kernel(x_ref, g_ref, w_ref, pos_ref, inv_ref, o_ref, *table_and_scratch):
    table_refs, xn_ref = table_and_scratch[:-1], table_and_scratch[-1]
    trows = pos_ref.shape[0] // PROJ_PIECES

    def table_piece(c):
        piece = slice(c * trows, (c + 1) * trows)
        _rope_table_kernel(pos_ref.at[piece, :], inv_ref, *(r.at[piece, :] for r in table_refs))

    _norm_matmul_pieces(x_ref, g_ref, w_ref, o_ref, xn_ref, between=table_piece)


def project_with_rope_tables(x, gain, w, positions, tm, tn):
    t, d = x.shape
    n = w.shape[1]
    n_steps = n // tn
    rows = tm // n_steps
    inv_ret = RET_THETA ** (-jnp.arange(RET_HALF, dtype=F32) / RET_HALF)
    inv_moba = ROPE_THETA ** (-jnp.arange(MOBA_HALF, dtype=F32) / MOBA_HALF)
    inv = jnp.concatenate([inv_ret, inv_moba, jnp.zeros((HEAD_DIM - RET_HALF - MOBA_HALF,), F32)])
    posf = jnp.broadcast_to(positions.astype(F32).reshape(t, 1), (t, HEAD_DIM))
    table_spec = pl.BlockSpec((rows, HEAD_DIM), lambda i, j: (i * n_steps + j, 0))
    return pl.pallas_call(
        _proj_tables_kernel,
        out_shape=[jax.ShapeDtypeStruct((t, n), BF16)] + [jax.ShapeDtypeStruct((t, HEAD_DIM), F32)] * 5,
        grid=(t // tm, n_steps),
        in_specs=[
            pl.BlockSpec((tm, d), lambda i, j: (i, 0)),
            pl.BlockSpec((1, d), lambda i, j: (0, 0)),
            pl.BlockSpec((d, tn), lambda i, j: (0, j)),
            table_spec,
            pl.BlockSpec((1, HEAD_DIM), lambda i, j: (0, 0)),
        ],
        out_specs=[pl.BlockSpec((tm, tn), lambda i, j: (i, j))] + [table_spec] * 5,
        scratch_shapes=[pltpu.VMEM((tm, d), BF16)],
        compiler_params=_params("arbitrary", "arbitrary"),
        name="in_proj_even",
    )(x, gain.reshape(1, d), w, posf, inv.reshape(1, HEAD_DIM))


def _moba_kernel(q_ref, k_ref, v_ref, cos_ref, sin_lo_ref, sin_hi_ref, o_ref,
                 qr_ref, ka_ref, va_ref, kmean_ref):
    seq = q_ref.shape[0]
    blk = MOBA_BLOCK
    nb = seq // blk
    exp2_scale = HEAD_DIM ** -0.5 * LOG2E

    row = lax.broadcasted_iota(jnp.int32, (blk, blk), 0)
    colm = lax.broadcasted_iota(jnp.int32, (blk, blk), 1)
    causal = colm <= row
    eye = jnp.where(colm == row, 1.0, 0.0).astype(BF16)
    blk_row = lax.broadcasted_iota(jnp.int32, (nb, blk), 0)
    lane = lax.broadcasted_iota(jnp.int32, (blk, HEAD_DIM), 1)
    kmean_ref[...] = jnp.zeros(kmean_ref.shape, kmean_ref.dtype)

    def prepare(qb):
        rows = slice(qb * blk, (qb + 1) * blk)

        def rope(x):
            return (x * cos_ref[rows, :]
                    + pltpu.roll(x, HEAD_DIM - MOBA_HALF, 1) * sin_lo_ref[rows, :]
                    + pltpu.roll(x, MOBA_HALF, 1) * sin_hi_ref[rows, :])

        kr = rope(k_ref[rows, :].astype(F32))
        kmean_ref[qb:qb + 1, :] = jnp.mean(kr, axis=0, keepdims=True)
        ka_ref[rows, :HEAD_DIM] = kr.astype(BF16)
        ka_ref[rows, HEAD_DIM:] = jnp.where(lane == qb, MASKED, 0.0).astype(BF16)
        qr_ref[rows, :] = (rope(q_ref[rows, :].astype(F32)) * exp2_scale).astype(BF16)
        va_ref[rows, :HEAD_DIM] = v_ref[rows, :]
        va_ref[rows, HEAD_DIM:] = jnp.ones((blk, HEAD_DIM), BF16)

    def scores(qb):
        rows = slice(qb * blk, (qb + 1) * blk)
        q = qr_ref[rows, :]
        n_keys = (qb + 1) * blk
        if qb > MOBA_TOPK:
            gate = _dot_nt(kmean_ref[...].astype(BF16), q)
            rank = jnp.zeros((nb, blk), F32)
            for j in range(qb):
                gj = gate[j:j + 1, :]
                ahead = (gj > gate) | ((gj == gate) & (blk_row > j))
                rank = rank + jnp.where(ahead, 1.0, 0.0)
            drop = jnp.where((rank >= float(MOBA_TOPK)) & (blk_row < qb), 1.0, 0.0)
            drop = jnp.concatenate([drop, jnp.zeros((HEAD_DIM - nb, blk), F32)], axis=0).astype(BF16)
            drop_col = _dot_nt(eye, drop).astype(BF16)
            s = _dot_nt(jnp.concatenate([q, drop_col], axis=1), ka_ref[:n_keys, :])
        else:
            s = _dot_nt(q, ka_ref[:n_keys, :HEAD_DIM])
        own = jnp.where(causal, s[:, qb * blk:], MASKED)
        return jnp.concatenate([s[:, :qb * blk], own], axis=1) if qb else own

    prepare(0)
    s_next = scores(0)
    for qb in range(nb):
        rows = slice(qb * blk, (qb + 1) * blk)
        s = s_next
        if qb + 1 < nb:
            prepare(qb + 1)
            s_next = scores(qb + 1)
        m = jnp.max(s, axis=-1, keepdims=True)
        p = jnp.exp2(s - m).astype(BF16)
        o = _dot(p, va_ref[:(qb + 1) * blk, :])
        o_ref[rows, :] = (o[:, :HEAD_DIM] / o[:, HEAD_DIM:]).astype(o_ref.dtype)


def moba_mixer(h3, cos, sin_lo, sin_hi, col_q, col_k, col_v):
    bsz, seq, _ = h3.shape

    def col(off):
        return pl.BlockSpec((None, seq, HEAD_DIM), lambda b, h: (b, 0, off + h))

    table = pl.BlockSpec((None, seq, HEAD_DIM), lambda b, h: (b, 0, 0))
    return pl.pallas_call(
        _moba_kernel,
        out_shape=jax.ShapeDtypeStruct((bsz, seq, WIDTH), BF16),
        grid=(bsz, N_HEADS),
        in_specs=[col(col_q), col(col_k), col(col_v), table, table, table],
        out_specs=pl.BlockSpec((None, seq, HEAD_DIM), lambda b, h: (b, 0, h)),
        scratch_shapes=[pltpu.VMEM((seq, HEAD_DIM), BF16), pltpu.VMEM((seq, 2 * HEAD_DIM), BF16),
                        pltpu.VMEM((seq, 2 * HEAD_DIM), BF16), pltpu.VMEM((seq // MOBA_BLOCK, HEAD_DIM), F32)],
        compiler_params=_params("arbitrary", "arbitrary"),
        name="moba",
    )(h3, h3, h3, cos, sin_lo, sin_hi)


def _rglru_kernel(x_ref, cw_ref, cb_ref, wa_ref, ba_ref, wx_ref, bx_ref, lam_ref,
                  o_ref, a_ref, u_ref, xpad_ref):
    seq, width = x_ref.shape
    sub = 8
    groups = seq // sub

    xpad_ref[:sub, :] = jnp.zeros((sub, width), F32)
    xpad_ref[sub:, :] = x_ref[...].astype(F32)
    xf = cb_ref[...]
    for d in range(CONV_WIDTH):
        xf = xf + xpad_ref[sub - d:sub - d + seq, :] * cw_ref[CONV_WIDTH - 1 - d:CONV_WIDTH - d, :]

    xb = xf.astype(BF16)

    def gate(w_ref, b_ref):
        pre = [_dot(xb[:, j * HEAD_DIM:(j + 1) * HEAD_DIM], w_ref[j]) for j in range(width // HEAD_DIM)]
        return _sigmoid(jnp.concatenate(pre, axis=1) + b_ref[...])

    r = gate(wa_ref, ba_ref)
    ig = gate(wx_ref, bx_ref)
    nl = -lam_ref[...]
    softplus = jnp.maximum(nl, 0.0) + jnp.log1p(jnp.exp(-jnp.abs(nl)))
    a = jnp.exp2(r * ((-RG_C * LOG2E) * softplus))
    gap = 1.0 - a * a
    u = jnp.where(gap > 0.0, gap * lax.rsqrt(gap), 0.0) * ig * xf

    a3 = a.reshape(groups, sub, width)
    u3 = u.reshape(groups, sub, width)
    s_idx = lax.broadcasted_iota(jnp.int32, a3.shape, 1)
    for d in (1, 2, 4):
        keep = s_idx >= d
        a_prev = jnp.where(keep, pltpu.roll(a3, d, 1), 1.0)
        u_prev = jnp.where(keep, pltpu.roll(u3, d, 1), 0.0)
        u3 = u3 + a3 * u_prev
        a3 = a3 * a_prev
    a_ref[...] = a3
    u_ref[...] = u3

    def group_body(g, carry):
        hg = a_ref[g] * carry + u_ref[g]
        u_ref[g] = hg
        return jnp.broadcast_to(hg[sub - 1:sub, :], (sub, width))

    lax.fori_loop(0, groups, group_body, jnp.zeros((sub, width), F32), unroll=8)
    o_ref[...] = u_ref[...].reshape(seq, width).astype(o_ref.dtype)


def rglru_mixer(h3, conv_w, conv_b, wa, ba, wx, bx, lam, col_x, blocks_per_step=RG_BLOCKS):
    bsz, seq, _ = h3.shape
    width = blocks_per_step * HEAD_DIM

    def col(off):
        return pl.BlockSpec((None, seq, width), lambda b, c: (b, 0, off // blocks_per_step + c))

    def vec(rows):
        return pl.BlockSpec((rows, width), lambda b, c: (0, c))

    wspec = pl.BlockSpec((blocks_per_step, HEAD_DIM, HEAD_DIM), lambda b, c: (c, 0, 0))
    return pl.pallas_call(
        _rglru_kernel,
        out_shape=jax.ShapeDtypeStruct((bsz, seq, WIDTH), BF16),
        grid=(bsz, WIDTH // width),
        in_specs=[col(col_x), vec(CONV_WIDTH), vec(1), wspec, vec(1), wspec, vec(1), vec(1)],
        out_specs=pl.BlockSpec((None, seq, width), lambda b, c: (b, 0, c)),
        scratch_shapes=[pltpu.VMEM((seq // 8, 8, width), F32)] * 2 + [pltpu.VMEM((seq + 8, width), F32)],
        compiler_params=_params("arbitrary", "arbitrary"),
        name="rglru",
    )(h3, conv_w, conv_b.reshape(1, WIDTH), wa.astype(BF16), ba.reshape(1, WIDTH),
      wx.astype(BF16), bx.reshape(1, WIDTH), lam.reshape(1, WIDTH))


def _retention_kernel(q_ref, k_ref, v_ref, cos_ref, sin_ref, logg_ref, gain_ref, bias_ref,
                      o_ref, st_ref):
    seq = q_ref.shape[0]
    ck = RET_CHUNK
    half = RET_HALF
    nc = seq // ck // RET_PARTS

    def chunks(x):
        return x.reshape(nc, ck, HEAD_DIM)

    def bmm(spec, a, b):
        return jnp.einsum(spec, a, b, preferred_element_type=F32)

    log_g = logg_ref[...]
    ri = lax.broadcasted_iota(jnp.int32, (ck, ck), 0)
    ci = lax.broadcasted_iota(jnp.int32, (ck, ck), 1)
    diff = (ri - ci).astype(F32)
    dmask = jnp.where(ri >= ci, jnp.exp(jnp.maximum(diff, 0.0) * log_g[:, :1]), 0.0)
    idx = lax.broadcasted_iota(jnp.int32, (ck, HEAD_DIM), 0).astype(F32)
    q_fac = jnp.exp((idx + 1.0) * log_g)
    k_fac = jnp.exp((ck - 1.0 - idx) * log_g)
    chunk_decay = jnp.exp(float(ck) * log_g)

    carried = [jnp.zeros((HEAD_DIM, HEAD_DIM), F32)]

    def part(p):
        rows = slice(p * nc * ck, (p + 1) * nc * ck)
        cos = cos_ref[rows, :]
        sin = sin_ref[rows, :]
        q = q_ref[rows, :].astype(F32)
        k = k_ref[rows, :].astype(F32)
        q = chunks(q * cos + pltpu.roll(q, half, 1) * sin)
        k = chunks((k * cos + pltpu.roll(k, half, 1) * sin) * (HEAD_DIM ** -0.5))
        v = chunks(v_ref[rows, :])
        yield
        att = bmm('cid,cjd->cij', q.astype(BF16), k.astype(BF16)) * dmask[None]
        k_end_t = jnp.swapaxes(k * k_fac[None], 1, 2).astype(BF16)
        incr = bmm('ckn,cnv->ckv', k_end_t, v)
        yield
        o_intra = bmm('cij,cjd->cid', att.astype(BF16), v)
        state = carried[0]
        for c in range(nc):
            st_ref[p * nc + c] = state.astype(BF16)
            state = state * chunk_decay + incr[c]
        carried[0] = state
        yield
        o = o_intra + bmm('cnk,ckv->cnv', (q * q_fac[None]).astype(BF16), st_ref[p * nc:(p + 1) * nc])
        o = o.reshape(nc * ck, HEAD_DIM)
        mu = jnp.mean(o, axis=-1, keepdims=True)
        oc = o - mu
        var = jnp.mean(oc * oc, axis=-1, keepdims=True)
        o_ref[rows, :] = (oc * lax.rsqrt(var + EPS) * gain_ref[...] + bias_ref[...]).astype(o_ref.dtype)
        yield

    _run_staggered([part(p) for p in range(RET_PARTS)], n_stages=4)


def retention_mixer(h3, cos, sin, log_g, gain, bias, col_q, col_k, col_v):
    bsz, seq, _ = h3.shape

    def col(off):
        return pl.BlockSpec((None, seq, HEAD_DIM), lambda b, h: (b, 0, off + h))

    table = pl.BlockSpec((None, seq, HEAD_DIM), lambda b, h: (b, 0, 0))
    vec = pl.BlockSpec((None, 1, HEAD_DIM), lambda b, h: (h, 0, 0))
    return pl.pallas_call(
        _retention_kernel,
        out_shape=jax.ShapeDtypeStruct((bsz, seq, WIDTH), BF16),
        grid=(bsz, N_HEADS),
        in_specs=[col(col_q), col(col_k), col(col_v), table, table, vec, vec, vec],
        out_specs=pl.BlockSpec((None, seq, HEAD_DIM), lambda b, h: (b, 0, h)),
        scratch_shapes=[pltpu.VMEM((seq // RET_CHUNK, HEAD_DIM, HEAD_DIM), BF16)],
        compiler_params=_params("arbitrary", "arbitrary"),
        name="retention",
    )(h3, h3, h3, cos, sin, log_g,
      gain.reshape(N_HEADS, 1, HEAD_DIM), bias.reshape(N_HEADS, 1, HEAD_DIM))


def _post_kernel(x_ref, ya_ref, yb_ref, z_ref, woa_ref, wob_ref, g_ref, wq_ref, kv_ref, wo_ref, fg_ref,
                 o_ref, *, final):
    part_rows = x_ref.shape[0] // POST_PARTS
    scale = HEAD_DIM ** -0.5

    def gated(y_ref, rows, cols):
        return (y_ref[rows, :].astype(F32) * _silu(z_ref[rows, cols].astype(F32))).astype(BF16)

    def part(p):
        rows = slice(p * part_rows, (p + 1) * part_rows)
        ga = gated(ya_ref, rows, slice(0, WIDTH))
        yield
        gb = gated(yb_ref, rows, slice(WIDTH, 2 * WIDTH))
        x1 = x_ref[rows, :] + _dot(ga, woa_ref[...])
        yield
        x1 = x1 + _dot(gb, wob_ref[...])
        yield
        ms = jnp.mean(x1 * x1, axis=-1, keepdims=True)
        xn = (x1 * lax.rsqrt(ms + EPS) * g_ref[...]).astype(BF16)
        q = _dot(xn, wq_ref[...]).astype(BF16)
        yield
        head_cols = [slice(hd * HEAD_DIM, (hd + 1) * HEAD_DIM) for hd in range(XA_HEADS)]
        scores = [_dot_nt(q[:, cs], kv_ref[:, cs]) * scale for cs in head_cols]
        yield
        probs = []
        for s in scores:
            pr = jnp.exp(s - jnp.max(s, axis=-1, keepdims=True))
            probs.append((pr / jnp.sum(pr, axis=-1, keepdims=True)).astype(BF16))
        yield
        o = jnp.concatenate([_dot(pr, kv_ref[:, XA_WIDTH + cs.start:XA_WIDTH + cs.stop]).astype(BF16)
                             for pr, cs in zip(probs, head_cols)], axis=1)
        yield
        x2 = x1 + _dot(o, wo_ref[...])
        if final:
            ms2 = jnp.mean(x2 * x2, axis=-1, keepdims=True)
            x2 = x2 * lax.rsqrt(ms2 + EPS) * fg_ref[...]
        o_ref[rows, :] = x2
        yield

    _run_staggered([part(p) for p in range(POST_PARTS)], n_stages=8)


def post_mixer(x3, ya, yb, h3, z_block, w_out, xa_gain, wq, kv, wo, final_gain, final, tm=POST_TM):
    bsz, seq, d = x3.shape
    full = lambda shape: pl.BlockSpec(shape, lambda b, i: (0,) * len(shape))
    tile = lambda w: pl.BlockSpec((None, tm, w), lambda b, i: (b, i, 0))
    return pl.pallas_call(
        functools.partial(_post_kernel, final=final),
        out_shape=jax.ShapeDtypeStruct((bsz, seq, d), F32),
        grid=(bsz, seq // tm),
        in_specs=[tile(d), tile(WIDTH), tile(WIDTH),
                  pl.BlockSpec((None, tm, 2 * WIDTH), lambda b, i: (b, i, z_block)),
                  full((WIDTH, d)), full((WIDTH, d)), full((1, d)), full((d, XA_WIDTH)),
                  pl.BlockSpec((None, MEM_LEN, 2 * XA_WIDTH), lambda b, i: (b, 0, 0)),
                  full((XA_WIDTH, d)), full((1, d))],
        out_specs=tile(d),
        compiler_params=_params("arbitrary", "arbitrary"),
        name="post_final" if final else "post",
    )(x3, ya, yb, h3, w_out[:WIDTH].astype(BF16), w_out[WIDTH:].astype(BF16), xa_gain.reshape(1, d),
      wq.astype(BF16), kv, wo.astype(BF16), final_gain.reshape(1, d))


def kernel(x, mem, positions, hgrn_lb_logits, ev_norm, ev_w_in, ev_hgrn_gain, ev_w_out, od_norm, od_w_in, od_conv_w, od_conv_b, od_rg_wa, od_rg_ba, od_rg_wx, od_rg_bx, od_rg_lambda, od_ret_gain, od_ret_bias, od_w_out, xa_norm, xa_mem_norm, xa_wq, xa_wkv, xa_wo, final_norm):
    bsz, seq, d = x.shape
    tokens = bsz * seq
    lb_all = jnp.cumsum(jax.nn.softmax(hgrn_lb_logits.astype(F32), axis=0), axis=0)
    log_g = jnp.log(1.0 - 2.0 ** (-5.0 - jnp.arange(N_HEADS, dtype=F32)))
    log_g = jnp.broadcast_to(log_g[:, None, None], (N_HEADS, 1, HEAD_DIM))
    mem2 = mem.reshape(bsz * MEM_LEN, d)

    def memory_kv(layer):
        kv = norm_matmul(mem2, xa_mem_norm[layer], xa_wkv[layer], BF16,
                         tm=bsz * MEM_LEN // 2, tn=2 * XA_WIDTH, name="mem_kv")
        return kv.reshape(bsz, MEM_LEN, 2 * XA_WIDTH)

    h, *tables = project_with_rope_tables(x.reshape(tokens, d), ev_norm[0], ev_w_in.reshape(ev_w_in.shape[1:]),
                                          positions, tm=seq, tn=PROJ_TN)
    ret_cos, ret_sin, moba_cos, moba_lo, moba_hi = (tb.reshape(bsz, seq, HEAD_DIM) for tb in tables)
    h3 = h.reshape(bsz, seq, -1)
    nb = WIDTH // HEAD_DIM
    ya = hgrn2_mixer(h3, lb_all[0], ev_hgrn_gain[0], 0, nb, 2 * nb)
    yb = moba_mixer(h3, moba_cos, moba_lo, moba_hi, 3 * nb, 4 * nb, 5 * nb)
    x1 = post_mixer(x, ya, yb, h3, 3, ev_w_out[0], xa_norm[0], xa_wq[0], memory_kv(0), xa_wo[0],
                    final_norm, final=False)

    h = norm_matmul(x1.reshape(tokens, d), od_norm[0], od_w_in.reshape(od_w_in.shape[1:]), BF16,
                    tm=seq, tn=PROJ_TN, name="in_proj_odd")
    h3 = h.reshape(bsz, seq, -1)
    yc = rglru_mixer(h3, od_conv_w[0], od_conv_b[0], od_rg_wa[0], od_rg_ba[0], od_rg_wx[0],
                     od_rg_bx[0], od_rg_lambda[0], 0)
    yd = retention_mixer(h3, ret_cos, ret_sin, log_g, od_ret_gain[0], od_ret_bias[0],
                         nb, 2 * nb, 3 * nb)
    return post_mixer(x1, yc, yd, h3, 2, od_w_out[0], xa_norm[1], xa_wq[1], memory_kv(1), xa_wo[1],
                      final_norm, final=True)
```
